```python
import math
import jax
import jax.numpy as jnp
from jax import lax
import numpy as np

D_MODEL = 2048
BATCH = 4
SEQ = 2048
DEPTH = 2

CTX_LEN = 256
GRID_W = 64
EPS = 1e-6
N_BRANCH = 3

POOL_WINDOWS = (2, 4, 8, 16)
POOL_WIDTH = D_MODEL // 2
POOL_GROUP = POOL_WIDTH // len(POOL_WINDOWS)

DIFF_HEADS = 8
DIFF_HEAD_DIM = 64
DIFF_VDIM = 2 * DIFF_HEAD_DIM
DIFF_WIDTH = DIFF_HEADS * DIFF_VDIM
ROPE_THETA = 10000.0
Q_BLOCK = 128

GLA_HEADS = 4
GLA_DV = D_MODEL // 2 // GLA_HEADS
GLA_DK = GLA_DV // 2
GLA_KW = GLA_HEADS * GLA_DK
GLA_VW = GLA_HEADS * GLA_DV
GLA_RANK = 16
GLA_NORMALIZER = 16.0
GLA_CHUNK = 64

IN_SIZES = (POOL_WIDTH, POOL_WIDTH,
            DIFF_WIDTH, DIFF_WIDTH, DIFF_WIDTH, DIFF_WIDTH,
            GLA_KW, GLA_KW, GLA_VW, GLA_VW,
            GLA_RANK, GLA_RANK,
            N_BRANCH * D_MODEL)
D_IN = sum(IN_SIZES)

kernel_name = "hybrid_pool_diffattn_gla_prefix_block"


def split_in(z):
    idx = np.cumsum(IN_SIZES)[:-1].tolist()
    return jnp.split(z, idx, axis=-1)


def rms_norm(x, gain):
    xf = x.astype(jnp.float32)
    y = xf * lax.rsqrt(jnp.mean(xf * xf, axis=-1, keepdims=True) + EPS)
    return (y * gain.astype(jnp.float32)).astype(x.dtype)


def axial_rope(rows, cols):
    n_freq = DIFF_HEAD_DIM // 4
    inv = ROPE_THETA ** (-jnp.arange(n_freq, dtype=jnp.float32) / n_freq)
    ang = jnp.concatenate([rows.astype(jnp.float32)[:, None] * inv,
                           cols.astype(jnp.float32)[:, None] * inv], axis=-1)
    return jnp.cos(ang), jnp.sin(ang)


def apply_rope(t, cos, sin):
    half = DIFF_HEAD_DIM // 2
    tf = t.astype(jnp.float32)
    t1, t2 = tf[..., :half], tf[..., half:]
    cs = cos[None, :, None, None, :]
    sn = sin[None, :, None, None, :]
    return jnp.concatenate([t1 * cs - t2 * sn, t2 * cs + t1 * sn], axis=-1).astype(t.dtype)


def pool_mix(u, w_grp, scale):
    B, L, _ = u.shape
    uf = u.astype(jnp.float32)
    cs = jnp.concatenate([jnp.zeros_like(uf[:, :1]), jnp.cumsum(uf, axis=1)], axis=1)
    t = jnp.arange(L)
    groups = []
    for gi, w in enumerate(POOL_WINDOWS):
        sl = slice(gi * POOL_GROUP, (gi + 1) * POOL_GROUP)
        hi = jnp.clip(t + w // 2, 0, L)
        lo = jnp.clip(t - w // 2, 0, L)
        cnt = (hi - lo).astype(jnp.float32)[None, :, None]
        mean = (cs[:, hi, sl] - cs[:, lo, sl]) / cnt
        groups.append(mean - uf[:, :, sl])
    d = jnp.stack(groups, axis=2)
    y = jnp.einsum('blgc,gcd->blgd', d, w_grp.astype(jnp.float32)).reshape(B, L, POOL_WIDTH)
    return (y * scale.astype(jnp.float32)).astype(u.dtype)


def diff_attend(q, k, v, lam):
    s = jnp.einsum('bqhjd,bkhjd->bhjqk', q, k).astype(jnp.float32) * DIFF_HEAD_DIM ** -0.5
    p = jax.nn.softmax(s, axis=-1)
    a = p[:, :, 0] - lam * p[:, :, 1]
    return jnp.einsum('bhqk,bkhe->bqhe', a.astype(v.dtype), v)


def diff_post(o, subln, lambda_init):
    B, L = o.shape[:2]
    return (rms_norm(o, subln) * (1.0 - lambda_init)).reshape(B, L, DIFF_WIDTH)


def gla_log_decay(lr, w2, b):
    return jax.nn.log_sigmoid((lr @ w2 + b).astype(jnp.float32)) / GLA_NORMALIZER


def gla_chunk_scan(q, k, v, g, s0):
    B, L, H, _ = q.shape
    n = L // GLA_CHUNK
    C = GLA_CHUNK

    def chunks(t):
        return jnp.moveaxis(t.reshape(B, n, C, H, t.shape[-1]), 1, 0)

    lower = jnp.tril(jnp.ones((C, C), dtype=bool))[None, :, :, None, None]

    def step(s, inp):
        qc, kc, vc, gc = inp
        b = jnp.cumsum(gc, axis=1)
        o = jnp.einsum('bthk,bhkv->bthv', qc * jnp.exp(b), s)
        rel = jnp.exp(jnp.where(lower, b[:, :, None] - b[:, None], -jnp.inf))
        att = jnp.einsum('bthk,btshk,bshk->bhts', qc, rel, kc)
        o = o + jnp.einsum('bhts,bshv->bthv', att, vc)
        b_last = b[:, -1]
        s = s * jnp.exp(b_last)[..., None] + jnp.einsum(
            'bshk,bshv->bhkv', kc * jnp.exp(b_last[:, None] - b), vc)
        return s, o

    s, o = lax.scan(step, s0, (chunks(q), chunks(k), chunks(v), chunks(g)))
    return jnp.moveaxis(o, 0, 1).reshape(B, L, H, v.shape[-1]), s


def gla_final_state(k, v, g):
    b = jnp.cumsum(g, axis=1)
    return jnp.einsum('blhk,blhv->bhkv', k * jnp.exp(b[:, -1:] - b), v)


def gla_post(o, gain, gate):
    B, L = o.shape[:2]
    return rms_norm(o, gain).reshape(B, L, GLA_VW).astype(gate.dtype) * jax.nn.silu(gate)


def flip(t):
    return jnp.flip(t, axis=1)


def merge_branches(pool_o, diff_o, gla_o, mg, wbp, wbd, wbg, w_out):
    gp, gd, gg = jnp.split(jax.nn.sigmoid(mg), N_BRANCH, axis=-1)
    y = gp * (pool_o @ wbp) + gd * (diff_o @ wbd) + gg * (gla_o @ wbg)
    return y @ w_out


def hybrid_layer(x, ctx, c, c_ctx, cos, sin, norm_g, w_ada, b_ada, w_in, pool_w, pool_scale,
                 q_norm, k_norm, lam_q1, lam_k1, lam_q2, lam_k2, subln,
                 wgf, bgf, wgb, bgb, gla_norm, wbp, wbd, wbg, w_out,
                 lambda_init, need_ctx_out):
    B, L, _ = x.shape
    Lc = ctx.shape[1]
    shift, scale, gate = jnp.split(jax.nn.silu(c) @ w_ada + b_ada, 3, axis=-1)
    shift_c, scale_c, gate_c = jnp.split(jax.nn.silu(c_ctx) @ w_ada + b_ada, 3, axis=-1)
    h = rms_norm(x, norm_g) * (1.0 + scale[:, None]) + shift[:, None]
    hc = rms_norm(ctx, norm_g) * (1.0 + scale_c) + shift_c
    (pu, pg, dq, dk, dv, dg, gq, gk, gv, gg, glf, glb, mg) = split_in(h @ w_in)
    (pu_c, pg_c, dq_c, dk_c, dv_c, dg_c, gq_c, gk_c, gv_c, gg_c, glf_c, glb_c, mg_c) = split_in(hc @ w_in)

    pool_l = pool_mix(pu, pool_w, pool_scale) * jax.nn.silu(pg)

    def qk_heads(t, gain):
        return rms_norm(t.reshape(t.shape[0], t.shape[1], DIFF_HEADS, 2, DIFF_HEAD_DIM), gain)

    lam = (jnp.exp(jnp.sum(lam_q1.astype(jnp.float32) * lam_k1.astype(jnp.float32)))
           - jnp.exp(jnp.sum(lam_q2.astype(jnp.float32) * lam_k2.astype(jnp.float32)))
           + lambda_init)
    q_l = apply_rope(qk_heads(dq, q_norm), cos, sin)
    k_l = apply_rope(qk_heads(dk, k_norm), cos, sin)
    v_l = dv.reshape(B, L, DIFF_HEADS, DIFF_VDIM)
    k_c = qk_heads(dk_c, k_norm)
    v_c = dv_c.reshape(B, Lc, DIFF_HEADS, DIFF_VDIM)
    k_all = jnp.concatenate([k_c, k_l], axis=1)
    v_all = jnp.concatenate([v_c, v_l], axis=1)
    nb = L // Q_BLOCK
    qb = jnp.swapaxes(q_l.reshape(B, nb, Q_BLOCK, DIFF_HEADS, 2, DIFF_HEAD_DIM), 0, 1)
    o_l = lax.map(lambda qq: diff_attend(qq, k_all, v_all, lam), qb)
    o_l = jnp.swapaxes(o_l, 0, 1).reshape(B, L, DIFF_HEADS, DIFF_VDIM)
    diff_l = diff_post(o_l, subln, lambda_init) * jax.nn.silu(dg)

    def gla_inputs(tq, tk, tv, tlf, tlb):
        n = tq.shape[1]
        q_ = (tq.astype(jnp.float32) * GLA_DK ** -0.5).reshape(B, n, GLA_HEADS, GLA_DK)
        k_ = tk.astype(jnp.float32).reshape(B, n, GLA_HEADS, GLA_DK)
        v_ = tv.astype(jnp.float32).reshape(B, n, GLA_HEADS, GLA_DV)
        gf_ = gla_log_decay(tlf, wgf, bgf).reshape(B, n, GLA_HEADS, GLA_DK)
        gb_ = gla_log_decay(tlb, wgb, bgb).reshape(B, n, GLA_HEADS, GLA_DK)
        return q_, k_, v_, gf_, gb_

    ql, kl, vl, gfl, gbl = gla_inputs(gq, gk, gv, glf, glb)
    qc, kc, vc, gfc, gbc = gla_inputs(gq_c, gk_c, gv_c, glf_c, glb_c)
    s0 = jnp.zeros((B, GLA_HEADS, GLA_DK, GLA_DV), jnp.float32)
    if need_ctx_out:
        oc_f, s_f = gla_chunk_scan(qc, kc, vc, gfc, s0)
        oc_b, s_b = gla_chunk_scan(flip(qc), flip(kc), flip(vc), flip(gbc), s0)
        gla_c = gla_post(oc_f + flip(oc_b), gla_norm, gg_c)
    else:
        s_f = gla_final_state(kc, vc, gfc)
        s_b = gla_final_state(flip(kc), flip(vc), flip(gbc))
    ol_f, _ = gla_chunk_scan(ql, kl, vl, gfl, s_f)
    ol_b, _ = gla_chunk_scan(flip(ql), flip(kl), flip(vl), flip(gbl), s_b)
    gla_l = gla_post(ol_f + flip(ol_b), gla_norm, gg)

    x_new = x + gate[:, None] * merge_branches(pool_l, diff_l, gla_l, mg, wbp, wbd, wbg, w_out)
    if need_ctx_out:
        pool_c = pool_mix(pu_c, pool_w, pool_scale) * jax.nn.silu(pg_c)
        q_c = qk_heads(dq_c, q_norm)
        o_c = diff_attend(q_c, k_c, v_c, lam)
        diff_c = diff_post(o_c, subln, lambda_init) * jax.nn.silu(dg_c)
        ctx_new = ctx + gate_c * merge_branches(pool_c, diff_c, gla_c, mg_c, wbp, wbd, wbg, w_out)
    else:
        ctx_new = ctx
    return x_new, ctx_new


def setup_inputs(seed: int = 0) -> dict:
    key = jax.random.key(seed)
    ks = jax.random.split(key, 32)
    D = D_MODEL
    NL = DEPTH

    def nrm(k, shape, s):
        return jax.random.normal(k, shape, jnp.float32) * s

    return {
        "x": nrm(ks[0], (BATCH, SEQ, D), 1.0),
        "c": nrm(ks[1], (BATCH, D), 1.0),
        "ctx": nrm(ks[2], (BATCH, CTX_LEN, D), 1.0),
        "c_ctx": nrm(ks[3], (D,), 1.0),
        "norm_g": 1.0 + nrm(ks[4], (NL, D), 0.02),
        "w_ada": nrm(ks[5], (NL, D, 3 * D), 0.5 * D ** -0.5),
        "b_ada": nrm(ks[6], (NL, 3 * D), 0.01),
        "w_in": nrm(ks[7], (NL, D, D_IN), D ** -0.5),
        "pool_w": nrm(ks[8], (NL, len(POOL_WINDOWS), POOL_GROUP, POOL_GROUP), POOL_GROUP ** -0.5),
        "pool_scale": 1.0 + nrm(ks[9], (NL, POOL_WIDTH), 0.02),
        "diff_q_norm": 1.0 + nrm(ks[10], (NL, DIFF_HEAD_DIM), 0.02),
        "diff_k_norm": 1.0 + nrm(ks[11], (NL, DIFF_HEAD_DIM), 0.02),
        "diff_lam_q1": nrm(ks[12], (NL, DIFF_HEAD_DIM), 0.1),
        "diff_lam_k1": nrm(ks[13], (NL, DIFF_HEAD_DIM), 0.1),
        "diff_lam_q2": nrm(ks[14], (NL, DIFF_HEAD_DIM), 0.1),
        "diff_lam_k2": nrm(ks[15], (NL, DIFF_HEAD_DIM), 0.1),
        "diff_subln": 1.0 + nrm(ks[16], (NL, DIFF_VDIM), 0.02),
        "gla_w_gate_f": nrm(ks[17], (NL, GLA_RANK, GLA_KW), GLA_RANK ** -0.5),
        "gla_b_gate_f": nrm(ks[18], (NL, GLA_KW), 0.01),
        "gla_w_gate_b": nrm(ks[19], (NL, GLA_RANK, GLA_KW), GLA_RANK ** -0.5),
        "gla_b_gate_b": nrm(ks[20], (NL, GLA_KW), 0.01),
        "gla_norm": 1.0 + nrm(ks[21], (NL, GLA_DV), 0.02),
        "w_branch_pool": nrm(ks[22], (NL, POOL_WIDTH, D), POOL_WIDTH ** -0.5),
        "w_branch_diff": nrm(ks[23], (NL, DIFF_WIDTH, D), DIFF_WIDTH ** -0.5),
        "w_branch_gla": nrm(ks[24], (NL, GLA_VW, D), GLA_VW ** -0.5),
        "w_out": nrm(ks[25], (NL, D, D), D ** -0.5),
    }


def reference(x, c, ctx, c_ctx, norm_g, w_ada, b_ada, w_in, pool_w, pool_scale,
              diff_q_norm, diff_k_norm, diff_lam_q1, diff_lam_k1, diff_lam_q2, diff_lam_k2,
              diff_subln, gla_w_gate_f, gla_b_gate_f, gla_w_gate_b, gla_b_gate_b, gla_norm,
              w_branch_pool, w_branch_diff, w_branch_gla, w_out):
    L = x.shape[1]
    ROWS = L // GRID_W
    rows = jnp.repeat(jnp.arange(ROWS), GRID_W)
    cols = jnp.tile(jnp.arange(GRID_W), ROWS)
    cos, sin = axial_rope(rows, cols)
    for l in range(DEPTH):
        lambda_init = 0.8 - 0.6 * math.exp(-0.3 * l)
        x, ctx = hybrid_layer(
            x, ctx, c, c_ctx, cos, sin, norm_g[l], w_ada[l], b_ada[l], w_in[l],
            pool_w[l], pool_scale[l], diff_q_norm[l], diff_k_norm[l],
            diff_lam_q1[l], diff_lam_k1[l], diff_lam_q2[l], diff_lam_k2[l], diff_subln[l],
            gla_w_gate_f[l], gla_b_gate_f[l], gla_w_gate_b[l], gla_b_gate_b[l], gla_norm[l],
            w_branch_pool[l], w_branch_diff[l], w_branch_gla[l], w_out[l],
            lambda_init, l < DEPTH - 1)
    return x
```

```python
import functools
import math

import jax
import jax.numpy as jnp
from jax import lax
from jax.experimental import pallas as pl
from jax.experimental.pallas import tpu as pltpu

F32 = jnp.float32
BF16 = jnp.bfloat16

EPS = 1e-6
GRID_W = 64
ROPE_THETA = 10000.0

POOL_WINDOWS = (2, 4, 8, 16)
POOL_HALO = 16
DIFF_HEADS = 8
DIFF_HEAD_DIM = 64
GLA_HEADS = 4
GLA_DK = 128
GLA_DV = 256
GLA_RANK = 16
GLA_NORMALIZER = 16.0
GLA_CHUNK = 64
GLA_SUB = 16
LANES = 128

VMEM_LIMIT = 48 * 1024 * 1024


def _cparams(*sem):
    return pltpu.CompilerParams(dimension_semantics=sem, vmem_limit_bytes=VMEM_LIMIT)


def _sigmoid(x):
    return 1.0 / (1.0 + jnp.exp(-x))


def _silu(x):
    return x * _sigmoid(x)


def _dot(a, b):
    return jnp.dot(a, b, preferred_element_type=F32)


def _dot_nt(a, b):
    return lax.dot_general(a, b, (((1,), (1,)), ((), ())), preferred_element_type=F32)


def _dot_tn(a, b):
    return lax.dot_general(a, b, (((0,), (0,)), ((), ())), preferred_element_type=F32)


def _split3(x):
    x1 = x.astype(BF16)
    r1 = x - x1.astype(F32)
    x2 = r1.astype(BF16)
    x3 = (r1 - x2.astype(F32)).astype(BF16)
    return x1, x2, x3


def _ada_kernel(cc_ref, w_ref, b_ref, o_ref):
    a = _silu(cc_ref[...]).astype(BF16)
    o_ref[...] = _dot(a, w_ref[...].astype(BF16)) + b_ref[...]


def _ada(cc, w_ada, b_ada, tn=768):
    rows, d = cc.shape
    n = w_ada.shape[1]
    return pl.pallas_call(
        _ada_kernel,
        grid=(n // tn,),
        in_specs=[pl.BlockSpec((rows, d), lambda j: (0, 0)),
                  pl.BlockSpec((d, tn), lambda j: (0, j)),
                  pl.BlockSpec((1, tn), lambda j: (0, j))],
        out_specs=pl.BlockSpec((rows, tn), lambda j: (0, j)),
        out_shape=jax.ShapeDtypeStruct((rows, n), F32),
        compiler_params=_cparams("arbitrary"),
        name="ada",
    )(cc, w_ada, b_ada.reshape(1, n))


def _inproj_kernel(x_ref, mod_ref, g_ref, w_ref, wlr_ref, z_ref, lr_ref, h_ref):
    @pl.when(pl.program_id(1) == 0)
    def _():
        x = x_ref[...]
        ms = jnp.mean(x * x, axis=-1, keepdims=True)
        y = x * lax.rsqrt(ms + EPS) * g_ref[...]
        h = (y * (1.0 + mod_ref[1:2, :]) + mod_ref[0:1, :]).astype(BF16)
        h_ref[...] = h
        lr_ref[...] = _dot(h, wlr_ref[...]).astype(lr_ref.dtype)

    z_ref[...] = _dot(h_ref[...], w_ref[...]).astype(z_ref.dtype)


def _inproj(x2d, mod3, norm_g, w, wlr, row_of_tile, tm, tn):
    m, d = x2d.shape
    n = w.shape[1]
    return pl.pallas_call(
        _inproj_kernel,
        grid=(m // tm, n // tn),
        in_specs=[pl.BlockSpec((tm, d), lambda i, j: (i, 0)),
                  pl.BlockSpec((None, 3, d), lambda i, j: (row_of_tile(i), 0, 0)),
                  pl.BlockSpec((1, d), lambda i, j: (0, 0)),
                  pl.BlockSpec((d, tn), lambda i, j: (0, j)),
                  pl.BlockSpec((d, LANES), lambda i, j: (0, 0))],
        out_specs=[pl.BlockSpec((tm, tn), lambda i, j: (i, j)),
                   pl.BlockSpec((tm, LANES), lambda i, j: (i, 0))],
        out_shape=[jax.ShapeDtypeStruct((m, n), BF16),
                   jax.ShapeDtypeStruct((m, LANES), BF16)],
        scratch_shapes=[pltpu.VMEM((tm, d), BF16)],
        compiler_params=_cparams("arbitrary", "arbitrary"),
        name="inproj",
    )(x2d, mod3, norm_g.reshape(1, d), w, wlr)


def _pool_kernel(up_ref, uc_ref, un_ref, pg_ref, w_ref, sc_ref, o_ref, *, tile, seq_len):
    r = pl.program_id(1)
    half = jnp.left_shift(1, pl.program_id(2))
    base = r * tile
    u_all = jnp.concatenate([up_ref[...], uc_ref[...], un_ref[...]], axis=0)
    width = tile + 2 * POOL_HALO
    t = base + lax.broadcasted_iota(jnp.int32, (tile, width), 0)
    s = base - POOL_HALO + lax.broadcasted_iota(jnp.int32, (tile, width), 1)
    inside = (s >= jnp.maximum(t - half, 0)) & (s < jnp.minimum(t + half, seq_len))
    band = jnp.where(inside, 1.0, 0.0).astype(BF16)
    wsum = _dot(band, u_all)
    tc = base + lax.broadcasted_iota(jnp.int32, (tile, 1), 0)
    cnt = (jnp.minimum(tc + half, seq_len) - jnp.maximum(tc - half, 0)).astype(F32)
    dcen = wsum / cnt - uc_ref[...].astype(F32)
    y = _dot(dcen.astype(BF16), w_ref[...].astype(BF16)) * sc_ref[...]
    o_ref[...] = (y * _silu(pg_ref[...].astype(F32))).astype(o_ref.dtype)


def _pool(z3, col_u, col_g, pool_w, pool_scale, tile):
    nseq, seq_len, _ = z3.shape
    ngrp = len(POOL_WINDOWS)
    gw = pool_w.shape[-1]
    cu, cg = col_u // gw, col_g // gw
    nhalo = seq_len // POOL_HALO
    per = tile // POOL_HALO
    kern = functools.partial(_pool_kernel, tile=tile, seq_len=seq_len)
    return pl.pallas_call(
        kern,
        grid=(nseq, seq_len // tile, ngrp),
        in_specs=[
            pl.BlockSpec((None, POOL_HALO, gw), lambda s, r, g: (s, jnp.maximum(r * per - 1, 0), cu + g)),
            pl.BlockSpec((None, tile, gw), lambda s, r, g: (s, r, cu + g)),
            pl.BlockSpec((None, POOL_HALO, gw), lambda s, r, g: (s, jnp.minimum((r + 1) * per, nhalo - 1), cu + g)),
            pl.BlockSpec((None, tile, gw), lambda s, r, g: (s, r, cg + g)),
            pl.BlockSpec((None, gw, gw), lambda s, r, g: (g, 0, 0)),
            pl.BlockSpec((1, gw), lambda s, r, g: (0, g)),
        ],
        out_specs=pl.BlockSpec((None, tile, gw), lambda s, r, g: (s, r, g)),
        out_shape=jax.ShapeDtypeStruct((nseq, seq_len, ngrp * gw), BF16),
        compiler_params=_cparams("arbitrary", "arbitrary", "arbitrary"),
        name="pool",
    )(z3, z3, z3, z3, pool_w, pool_scale.reshape(1, ngrp * gw))


def _head_norm(t, gain, gmat):
    sq = t * t
    hi = sq.astype(BF16)
    lo = (sq - hi.astype(F32)).astype(BF16)
    ssq = _dot(hi, gmat) + _dot(lo, gmat)
    return t * lax.rsqrt(ssq * (1.0 / DIFF_HEAD_DIM) + EPS) * gain


def _rope(t, cos_t, sin_s, first_half):
    half = DIFF_HEAD_DIM // 2
    lower = pltpu.roll(t, half, 1)
    upper = pltpu.roll(t, LANES - half, 1)
    return t * cos_t + jnp.where(first_half, upper, lower) * sin_s


def _attn_kernel(*refs, tq, rope_q, has_latent, lambda_init):
    if has_latent:
        (q_ref, dg_ref, kc_ref, vc_ref, kl_ref, vl_ref, cosq_ref, sinq_ref, cosk_ref, sink_ref,
         qn_ref, kn_ref, sub_ref, lam_ref, o_ref, kcs_ref, kls_ref) = refs
    else:
        (q_ref, dg_ref, kc_ref, vc_ref, qn_ref, kn_ref, sub_ref, lam_ref, o_ref, kcs_ref) = refs

    lane = lax.broadcasted_iota(jnp.int32, (1, LANES), 1)
    first_half = (lane & (DIFF_HEAD_DIM - 1)) < (DIFF_HEAD_DIM // 2)
    sub1 = lane < DIFF_HEAD_DIM
    gi = jnp.where(lax.broadcasted_iota(jnp.int32, (LANES, LANES), 0) < DIFF_HEAD_DIM, 1.0, 0.0)
    gj = jnp.where(lax.broadcasted_iota(jnp.int32, (LANES, LANES), 1) < DIFF_HEAD_DIM, 1.0, 0.0)
    gmat = (gi * gj + (1.0 - gi) * (1.0 - gj)).astype(BF16)

    @pl.when(pl.program_id(2) == 0)
    def _():
        kcs_ref[...] = _head_norm(kc_ref[...].astype(F32), kn_ref[...], gmat).astype(BF16)
        if has_latent:
            kl = _head_norm(kl_ref[...].astype(F32), kn_ref[...], gmat)
            kls_ref[...] = _rope(kl, cosk_ref[...], sink_ref[...], first_half).astype(BF16)

    q = _head_norm(q_ref[...].astype(F32), qn_ref[...], gmat)
    if rope_q:
        q = _rope(q, cosq_ref[...], sinq_ref[...], first_half)
    q = q * (DIFF_HEAD_DIM ** -0.5)
    qq = jnp.concatenate([jnp.where(sub1, q, 0.0), jnp.where(sub1, 0.0, q)], axis=0).astype(BF16)

    segs = [(kcs_ref, vc_ref)] + ([(kls_ref, vl_ref)] if has_latent else [])
    scores = [_dot_nt(qq, k_ref[...]) for k_ref, _ in segs]
    mx = functools.reduce(jnp.maximum, [jnp.max(s, axis=-1, keepdims=True) for s in scores])
    es = [jnp.exp(s - mx) for s in scores]
    den = functools.reduce(jnp.add, [jnp.sum(e, axis=-1, keepdims=True) for e in es])

    lam_p = lam_ref[...]
    lam = (jnp.exp(jnp.sum(lam_p[0:1] * lam_p[1:2], axis=-1, keepdims=True))
           - jnp.exp(jnp.sum(lam_p[2:3] * lam_p[3:4], axis=-1, keepdims=True)) + lambda_init)
    inv = 1.0 / den
    inv1 = inv[:tq]
    inv2 = inv[tq:] * lam
    o = None
    for e, (_, v_ref) in zip(es, segs):
        a = (e[:tq] * inv1 - e[tq:] * inv2).astype(BF16)
        part = _dot(a, v_ref[...])
        o = part if o is None else o + part

    ms = jnp.mean(o * o, axis=-1, keepdims=True)
    y = o * lax.rsqrt(ms + EPS) * sub_ref[...] * (1.0 - lambda_init)
    o_ref[...] = (y * _silu(dg_ref[...].astype(F32))).astype(o_ref.dtype)


def _attn(zq3, cq, cdg, zc3, ckc, cvc, zl3, ckl, cvl, rope_tabs, q_norm, k_norm, subln, lam_p,
          lambda_init, tq):
    nb, lq, _ = zq3.shape
    lc = zc3.shape[1]
    has_latent = zl3 is not None
    hw = 2 * DIFF_HEAD_DIM
    b_of = lambda c: c // hw
    qmap = lambda off: (lambda b, h, i: (b, i, b_of(off) + h))
    kmap = lambda off: (lambda b, h, i: (b, 0, b_of(off) + h))
    const = lambda b, h, i: (0, 0)
    in_specs = [pl.BlockSpec((None, tq, hw), qmap(cq)),
                pl.BlockSpec((None, tq, hw), qmap(cdg)),
                pl.BlockSpec((None, lc, hw), kmap(ckc)),
                pl.BlockSpec((None, lc, hw), kmap(cvc))]
    args = [zq3, zq3, zc3, zc3]
    scratch = [pltpu.VMEM((lc, hw), BF16)]
    if has_latent:
        ll = zl3.shape[1]
        cos_t, sin_s = rope_tabs
        in_specs += [pl.BlockSpec((None, ll, hw), kmap(ckl)),
                     pl.BlockSpec((None, ll, hw), kmap(cvl)),
                     pl.BlockSpec((tq, hw), lambda b, h, i: (i, 0)),
                     pl.BlockSpec((tq, hw), lambda b, h, i: (i, 0)),
                     pl.BlockSpec((ll, hw), const),
                     pl.BlockSpec((ll, hw), const)]
        args += [zl3, zl3, cos_t, sin_s, cos_t, sin_s]
        scratch += [pltpu.VMEM((ll, hw), BF16)]
    in_specs += [pl.BlockSpec((1, hw), const)] * 3 + [pl.BlockSpec((4, DIFF_HEAD_DIM), const)]
    args += [jnp.tile(q_norm, 2).reshape(1, hw), jnp.tile(k_norm, 2).reshape(1, hw),
             subln.reshape(1, hw), lam_p]
    kern = functools.partial(_attn_kernel, tq=tq, rope_q=has_latent, has_latent=has_latent,
                             lambda_init=lambda_init)
    return pl.pallas_call(
        kern,
        grid=(nb, DIFF_HEADS, lq // tq),
        in_specs=in_specs,
        out_specs=pl.BlockSpec((None, tq, hw), lambda b, h, i: (b, i, h)),
        out_shape=jax.ShapeDtypeStruct((nb, lq, DIFF_HEADS * hw), BF16),
        scratch_shapes=scratch,
        compiler_params=_cparams("arbitrary", "arbitrary", "arbitrary"),
        name="diff_attn",
    )(*args)


def _log_sigmoid(x):
    return jnp.minimum(x, 0.0) - jnp.log1p(jnp.exp(-jnp.abs(x)))


def _gla_chunk(q, k, v, g, st, rev, need_out):
    c = GLA_CHUNK
    nsub = c // GLA_SUB
    ri = lax.broadcasted_iota(jnp.int32, (c, c), 0)
    ci = lax.broadcasted_iota(jnp.int32, (c, c), 1)
    seen = (ci >= ri) if rev else (ci <= ri)
    cum = jnp.where(seen, 1.0, 0.0).astype(BF16)
    g1, g2, g3 = _split3(g)
    b = _dot(cum, g1) + _dot(cum, g2) + _dot(cum, g3)
    b_tot = jnp.sum(g, axis=0, keepdims=True)

    kd = (k * jnp.exp(b_tot - b)).astype(BF16)
    st_new = st * jnp.exp(b_tot) + _dot_tn(v, kd)
    if not need_out:
        return None, st_new

    out = _dot_nt((q * jnp.exp(b)).astype(BF16), st.astype(BF16))

    qparts, kparts = [], []
    blocks = range(nsub - 1) if rev else range(1, nsub)
    for i in blocks:
        lo, hi = i * GLA_SUB, (i + 1) * GLA_SUB
        anchor = b[hi:hi + 1] if rev else b[lo - 1:lo]
        qi = q[lo:hi] * jnp.exp(b[lo:hi] - anchor)
        pieces = []
        if lo > 0:
            pieces.append(jnp.zeros((lo, GLA_DK), F32))
        pieces.append(qi)
        if hi < c:
            pieces.append(jnp.zeros((c - hi, GLA_DK), F32))
        qparts.append(jnp.concatenate(pieces, axis=0))
        kparts.append(k * jnp.exp(jnp.minimum(anchor - b, 0.0)))
    qcat = jnp.concatenate(qparts, axis=1).astype(BF16)
    kcat = jnp.concatenate(kparts, axis=1).astype(BF16)
    att_off = _dot_nt(qcat, kcat)

    wrows = []
    for i in range(nsub):
        lo, hi = i * GLA_SUB, (i + 1) * GLA_SUB
        qb, kb, bb = q[lo:hi], k[lo:hi], b[lo:hi]
        for s in range(GLA_SUB):
            wrows.append(qb * kb[s:s + 1] * jnp.exp(jnp.minimum(bb - bb[s:s + 1], 0.0)))
    wst = jnp.concatenate(wrows, axis=0).astype(BF16)
    red = _dot(wst, jnp.ones((GLA_DK, LANES), BF16))
    lane = lax.broadcasted_iota(jnp.int32, (GLA_SUB, LANES), 1)
    drows = []
    for i in range(nsub):
        acc = jnp.zeros((GLA_SUB, LANES), F32)
        for s in range(GLA_SUB):
            r0 = (i * GLA_SUB + s) * GLA_SUB
            acc = jnp.where(lane == i * GLA_SUB + s, red[r0:r0 + GLA_SUB], acc)
        drows.append(acc)
    att_diag = jnp.concatenate(drows, axis=0)[:, :c]

    shift = GLA_SUB.bit_length() - 1
    rblk, cblk = jnp.right_shift(ri, shift), jnp.right_shift(ci, shift)
    off_keep = (cblk > rblk) if rev else (cblk < rblk)
    att = jnp.where(off_keep, att_off, jnp.where((cblk == rblk) & seen, att_diag, 0.0))
    out = out + _dot(att.astype(BF16), v)
    return out, st_new


def _gla_kernel(*refs, n_ctx, n_lat, ctx_out):
    (qc_ref, kc_ref, vc_ref, ggc_ref, lrc_ref, ql_ref, kl_ref, vl_ref, ggl_ref, lrl_ref,
     w2_ref, b2_ref, gain_ref) = refs[:13]
    rest = refs[13:]
    if ctx_out:
        ol_ref, oc_ref, gl_ref, gc_ref, accl_ref, accc_ref, stf_ref, stb_ref = rest
    else:
        ol_ref, gl_ref, gc_ref, accl_ref, stf_ref, stb_ref = rest
        oc_ref = accc_ref = None

    w2 = w2_ref[...]
    b2 = b2_ref[...]
    inv_norm = 1.0 / GLA_NORMALIZER
    gl_ref[...] = _log_sigmoid(_dot(lrl_ref[...], w2) + b2) * inv_norm
    gc_ref[...] = _log_sigmoid(_dot(lrc_ref[...], w2) + b2) * inv_norm
    stf_ref[...] = jnp.zeros_like(stf_ref)
    stb_ref[...] = jnp.zeros_like(stb_ref)
    scale = GLA_DK ** -0.5

    def scan(n, q_ref, k_ref, v_ref, g_ref, acc_ref):
        need_out = acc_ref is not None

        def step(i, first_visit):
            for c_idx, rev, st_ref, col in ((i, False, stf_ref, 0), (n - 1 - i, True, stb_ref, GLA_DK)):
                rows = pl.ds(pl.multiple_of(c_idx * GLA_CHUNK, GLA_CHUNK), GLA_CHUNK)
                q = q_ref[rows, :].astype(F32) * scale
                k = k_ref[rows, :].astype(F32)
                v = v_ref[rows, :]
                g = g_ref[rows, col:col + GLA_DK]
                out, st_new = _gla_chunk(q, k, v, g, st_ref[...], rev, need_out)
                st_ref[...] = st_new
                if need_out:
                    if first_visit:
                        acc_ref[rows, :] = out
                    else:
                        acc_ref[rows, :] += out

        def first(i, carry):
            step(i, True)
            return carry

        def second(i, carry):
            step(i, False)
            return carry

        lax.fori_loop(0, n // 2, first, 0)
        lax.fori_loop(n // 2, n, second, 0)

    def finish(acc_ref, gg_ref, o_ref):
        o = acc_ref[...]
        ms = jnp.mean(o * o, axis=-1, keepdims=True)
        y = o * lax.rsqrt(ms + EPS) * gain_ref[...]
        o_ref[...] = (y * _silu(gg_ref[...].astype(F32))).astype(o_ref.dtype)

    scan(n_ctx, qc_ref, kc_ref, vc_ref, gc_ref, accc_ref)
    scan(n_lat, ql_ref, kl_ref, vl_ref, gl_ref, accl_ref)
    finish(accl_ref, ggl_ref, ol_ref)
    if ctx_out:
        finish(accc_ref, ggc_ref, oc_ref)


def _gla(zc3, cc, lrc3, zl3, cl, lrl3, w2, b2, gain, ctx_out):
    nb, lc, _ = zc3.shape
    ll = zl3.shape[1]
    assert lc % (2 * GLA_CHUNK) == 0 and ll % (2 * GLA_CHUNK) == 0

    def seq_specs(z3, cols, lr3, slen):
        qo, ko, vo, go = cols
        return ([pl.BlockSpec((None, slen, GLA_DK), lambda b, h: (b, 0, qo // GLA_DK + h)),
                 pl.BlockSpec((None, slen, GLA_DK), lambda b, h: (b, 0, ko // GLA_DK + h)),
                 pl.BlockSpec((None, slen, GLA_DV), lambda b, h: (b, 0, vo // GLA_DV + h)),
                 pl.BlockSpec((None, slen, GLA_DV), lambda b, h: (b, 0, (0 if go is None else go) // GLA_DV + h)),
                 pl.BlockSpec((None, slen, LANES), lambda b, h: (b, 0, 0))],
                [z3, z3, z3, z3, lr3])

    sc, ac = seq_specs(zc3, cc, lrc3, lc)
    sl, al = seq_specs(zl3, cl, lrl3, ll)
    in_specs = sc + sl + [pl.BlockSpec((None, LANES, 2 * GLA_DK), lambda b, h: (h, 0, 0)),
                          pl.BlockSpec((None, 1, 2 * GLA_DK), lambda b, h: (h, 0, 0)),
                          pl.BlockSpec((1, GLA_DV), lambda b, h: (0, 0))]
    args = ac + al + [w2, b2, gain.reshape(1, GLA_DV)]
    out_specs = [pl.BlockSpec((None, ll, GLA_DV), lambda b, h: (b, 0, h))]
    out_shape = [jax.ShapeDtypeStruct((nb, ll, GLA_HEADS * GLA_DV), BF16)]
    scratch = [pltpu.VMEM((ll, 2 * GLA_DK), F32), pltpu.VMEM((lc, 2 * GLA_DK), F32),
               pltpu.VMEM((ll, GLA_DV), F32)]
    if ctx_out:
        out_specs.append(pl.BlockSpec((None, lc, GLA_DV), lambda b, h: (b, 0, h)))
        out_shape.append(jax.ShapeDtypeStruct((nb, lc, GLA_HEADS * GLA_DV), BF16))
        scratch.append(pltpu.VMEM((lc, GLA_DV), F32))
    scratch += [pltpu.VMEM((GLA_DV, GLA_DK), F32), pltpu.VMEM((GLA_DV, GLA_DK), F32)]
    kern = functools.partial(_gla_kernel, n_ctx=lc // GLA_CHUNK, n_lat=ll // GLA_CHUNK, ctx_out=ctx_out)
    outs = pl.pallas_call(
        kern,
        grid=(nb, GLA_HEADS),
        in_specs=in_specs,
        out_specs=out_specs,
        out_shape=out_shape,
        scratch_shapes=scratch,
        compiler_params=_cparams("arbitrary", "arbitrary"),
        name="gla",
    )(*args)
    return (outs[0], outs[1]) if ctx_out else (outs[0], None)


def _merge_kernel(p_ref, d_ref, g_ref, wp_ref, wd_ref, wg_ref, mp_ref, md_ref, mg_ref, y_ref):
    y = (_sigmoid(mp_ref[...].astype(F32)) * _dot(p_ref[...], wp_ref[...])
         + _sigmoid(md_ref[...].astype(F32)) * _dot(d_ref[...], wd_ref[...])
         + _sigmoid(mg_ref[...].astype(F32)) * _dot(g_ref[...], wg_ref[...]))
    y_ref[...] = y.astype(y_ref.dtype)


def _merge(pool_o, diff_o, gla_o, z2d, col_mg, wbp, wbd, wbg, tm, tn):
    m, kw = pool_o.shape
    d = wbp.shape[1]
    act = pl.BlockSpec((tm, kw), lambda i, j: (i, 0))
    wsp = pl.BlockSpec((kw, tn), lambda i, j: (0, j))
    gate = lambda k: pl.BlockSpec((tm, tn), lambda i, j: (i, (col_mg + k * d) // tn + j))
    return pl.pallas_call(
        _merge_kernel,
        grid=(m // tm, d // tn),
        in_specs=[act, act, act, wsp, wsp, wsp, gate(0), gate(1), gate(2)],
        out_specs=pl.BlockSpec((tm, tn), lambda i, j: (i, j)),
        out_shape=jax.ShapeDtypeStruct((m, d), BF16),
        compiler_params=_cparams("arbitrary", "arbitrary"),
        name="merge",
    )(pool_o, diff_o, gla_o, wbp, wbd, wbg, z2d, z2d, z2d)


def _outproj_kernel(y_ref, w_ref, x_ref, mod_ref, o_ref):
    o_ref[...] = x_ref[...] + mod_ref[2:3, :] * _dot(y_ref[...], w_ref[...])


def _outproj(y, w_out, x2d, mod3, row_of_tile, tm, tn):
    m, d = x2d.shape
    return pl.pallas_call(
        _outproj_kernel,
        grid=(m // tm, d // tn),
        in_specs=[pl.BlockSpec((tm, d), lambda i, j: (i, 0)),
                  pl.BlockSpec((d, tn), lambda i, j: (0, j)),
                  pl.BlockSpec((tm, tn), lambda i, j: (i, j)),
                  pl.BlockSpec((None, 3, tn), lambda i, j: (row_of_tile(i), 0, j))],
        out_specs=pl.BlockSpec((tm, tn), lambda i, j: (i, j)),
        out_shape=jax.ShapeDtypeStruct((m, d), F32),
        compiler_params=_cparams("arbitrary", "arbitrary"),
        name="outproj",
    )(y, w_out, x2d, mod3)


def _rope_tables(seq_len):
    n_freq = DIFF_HEAD_DIM // 4
    t = jnp.arange(seq_len)
    inv = ROPE_THETA ** (-jnp.arange(n_freq, dtype=F32) / n_freq)
    ang = jnp.concatenate([(t // GRID_W).astype(F32)[:, None] * inv,
                           (t % GRID_W).astype(F32)[:, None] * inv], axis=-1)
    cos, sin = jnp.cos(ang), jnp.sin(ang)
    return jnp.tile(cos, (1, 4)), jnp.tile(jnp.concatenate([-sin, sin], axis=-1), (1, 2))


def _largest_tile(n, cap, mult):
    t = min(n, cap)
    while n % t or t % mult:
        t -= mult
    return t


def kernel(x, c, ctx, c_ctx, norm_g, w_ada, b_ada, w_in, pool_w, pool_scale, diff_q_norm, diff_k_norm, diff_lam_q1, diff_lam_k1, diff_lam_q2, diff_lam_k2, diff_subln, gla_w_gate_f, gla_b_gate_f, gla_w_gate_b, gla_b_gate_b, gla_norm, w_branch_pool, w_branch_diff, w_branch_gla, w_out):
    nb, seq, d = x.shape
    lc = ctx.shape[1]
    depth = w_in.shape[0]
    pw = pool_scale.shape[1]
    dw = DIFF_HEADS * 2 * DIFF_HEAD_DIM
    gkw, gvw = GLA_HEADS * GLA_DK, GLA_HEADS * GLA_DV

    sizes = dict(pu=pw, pg=pw, dq=dw, dk=dw, dv=dw, dg=dw, gq=gkw, gk=gkw, gv=gvw, gg=gvw)
    col, off = {}, 0
    for name, size in sizes.items():
        col[name] = off
        off += size
    lr_lo, lr_hi = off, off + 2 * GLA_RANK
    col["mg"] = off
    n_main = off + 3 * d

    rope_tabs = _rope_tables(seq)
    cc = jnp.zeros((8, d), F32).at[:nb].set(c).at[nb].set(c_ctx)
    x2d = x.reshape(nb * seq, d)
    ctx2d = ctx.reshape(nb * lc, d)

    tm_x = _largest_tile(seq, 1024, 16)
    tm_c = _largest_tile(lc, 1024, 16)
    row_x = lambda tm: (lambda i: (i * tm) // seq)
    row_c = lambda i: nb

    for l in range(depth):
        last = l == depth - 1
        lambda_init = 0.8 - 0.6 * math.exp(-0.3 * l)
        mod3 = _ada(cc, w_ada[l], b_ada[l]).reshape(8, 3, d)

        wl = w_in[l]
        w_main = jnp.concatenate([wl[:, :lr_lo], wl[:, lr_hi:]], axis=1).astype(BF16)
        w_lr = jnp.pad(wl[:, lr_lo:lr_hi], ((0, 0), (0, LANES - 2 * GLA_RANK))).astype(BF16)
        if last:
            w_ctx = jnp.concatenate([w_main[:, col["dk"]:col["dg"]], w_main[:, col["gk"]:col["gg"]]], axis=1)
            ccol = dict(dk=0, dv=dw, gk=2 * dw, gv=2 * dw + gkw, gq=2 * dw, gg=None)
        else:
            w_ctx, ccol = w_main, col

        z, lr = _inproj(x2d, mod3, norm_g[l], w_main, w_lr, row_x(tm_x), tm_x,
                        _largest_tile(n_main, 1024, 256))
        zc, lrc = _inproj(ctx2d, mod3, norm_g[l], w_ctx, w_lr, row_c, tm_c,
                          _largest_tile(w_ctx.shape[1], 1024, 256))
        z3, zc3 = z.reshape(nb, seq, -1), zc.reshape(nb, lc, -1)
        lr3, lrc3 = lr.reshape(nb, seq, LANES), lrc.reshape(nb, lc, LANES)

        wbp, wbd, wbg = (w.astype(BF16) for w in (w_branch_pool[l], w_branch_diff[l], w_branch_gla[l]))
        wo = w_out[l].astype(BF16)
        lam_p = jnp.stack([diff_lam_q1[l], diff_lam_k1[l], diff_lam_q2[l], diff_lam_k2[l]])

        w2 = jnp.zeros((GLA_HEADS, LANES, 2 * GLA_DK), F32)
        w2 = w2.at[:, :GLA_RANK, :GLA_DK].set(gla_w_gate_f[l].reshape(GLA_RANK, GLA_HEADS, GLA_DK).transpose(1, 0, 2))
        w2 = w2.at[:, GLA_RANK:2 * GLA_RANK, GLA_DK:].set(
            gla_w_gate_b[l].reshape(GLA_RANK, GLA_HEADS, GLA_DK).transpose(1, 0, 2))
        b2 = jnp.concatenate([gla_b_gate_f[l].reshape(GLA_HEADS, 1, GLA_DK),
                              gla_b_gate_b[l].reshape(GLA_HEADS, 1, GLA_DK)], axis=-1)

        pool_l = _pool(z3, col["pu"], col["pg"], pool_w[l], pool_scale[l], _largest_tile(seq, 512, 16))
        diff_l = _attn(z3, col["dq"], col["dg"], zc3, ccol["dk"], ccol["dv"], z3, col["dk"], col["dv"],
                       rope_tabs, diff_q_norm[l], diff_k_norm[l], diff_subln[l], lam_p, lambda_init,
                       _largest_tile(seq, 256, 16))
        gla_l, gla_c = _gla(zc3, (ccol["gq"], ccol["gk"], ccol["gv"], ccol["gg"]), lrc3,
                            z3, (col["gq"], col["gk"], col["gv"], col["gg"]), lr3,
                            w2.astype(BF16), b2, gla_norm[l], ctx_out=not last)

        tm_m = _largest_tile(seq, 512, 16)
        y = _merge(pool_l.reshape(nb * seq, pw), diff_l.reshape(nb * seq, dw), gla_l.reshape(nb * seq, gvw),
                   z, col["mg"], wbp, wbd, wbg, tm_m, 1024)
        x2d_new = _outproj(y, wo, x2d, mod3, row_x(tm_m), tm_m, 1024)

        if not last:
            pool_c = _pool(zc3, col["pu"], col["pg"], pool_w[l], pool_scale[l], _largest_tile(lc, 512, 16))
            diff_c = _attn(zc3, col["dq"], col["dg"], zc3, col["dk"], col["dv"], None, None, None,
                           None, diff_q_norm[l], diff_k_norm[l], diff_subln[l], lam_p, lambda_init,
                           _largest_tile(lc, 256, 16))
            tm_mc = _largest_tile(lc, 512, 16)
            y_c = _merge(pool_c.reshape(nb * lc, pw), diff_c.reshape(nb * lc, dw), gla_c.reshape(nb * lc, gvw),
                         zc, col["mg"], wbp, wbd, wbg, tm_mc, 1024)
            ctx2d = _outproj(y_c, wo, ctx2d, mod3, row_c, tm_mc, 1024)
        x2d = x2d_new

    return x2d.reshape(nb, seq, d)
```

```python
import functools
import math

import jax
import jax.numpy as jnp
import numpy as np
from jax import lax
from jax.experimental import pallas as pl
from jax.experimental.pallas import tpu as pltpu

F32 = jnp.float32
BF16 = jnp.bfloat16

EPS = 1e-6
GRID_W = 64
ROPE_THETA = 10000.0

POOL_WINDOWS = (2, 4, 8, 16)
POOL_HALO = 16
DIFF_HEADS = 8
DIFF_HEAD_DIM = 64
GLA_HEADS = 4
GLA_DK = 128
GLA_DV = 256
GLA_RANK = 16
GLA_NORMALIZER = 16.0
GLA_CHUNK = 64
LANES = 128
LOG2E = math.log2(math.e)

VMEM_LIMIT = 48 * 1024 * 1024


def _cparams(*sem):
    return pltpu.CompilerParams(dimension_semantics=sem, vmem_limit_bytes=VMEM_LIMIT)


def _sigmoid(x):
    return 1.0 / (1.0 + jnp.exp(-x))


def _silu(x):
    return x * _sigmoid(x)


def _dot(a, b):
    return jnp.dot(a, b, preferred_element_type=F32)


def _dot_nt(a, b):
    return lax.dot_general(a, b, (((1,), (1,)), ((), ())), preferred_element_type=F32)


def _dot_tn(a, b):
    return lax.dot_general(a, b, (((0,), (0,)), ((), ())), preferred_element_type=F32)


def _ada_kernel(cc_ref, w_ref, b_ref, o_ref):
    a = _silu(cc_ref[...]).astype(BF16)
    o_ref[...] = _dot(a, w_ref[...].astype(BF16)) + b_ref[...]


def _ada(cc, w_ada, b_ada, tn=768):
    rows, d = cc.shape
    n = w_ada.shape[1]
    return pl.pallas_call(
        _ada_kernel,
        grid=(n // tn,),
        in_specs=[pl.BlockSpec((rows, d), lambda j: (0, 0)),
                  pl.BlockSpec((d, tn), lambda j: (0, j)),
                  pl.BlockSpec((1, tn), lambda j: (0, j))],
        out_specs=pl.BlockSpec((rows, tn), lambda j: (0, j)),
        out_shape=jax.ShapeDtypeStruct((rows, n), F32),
        compiler_params=_cparams("arbitrary"),
        name="ada",
    )(cc, w_ada, b_ada.reshape(1, n))


def _inproj_kernel(x_ref, mod_ref, g_ref, w_ref, wlr_ref, z_ref, lr_ref, h_ref):
    @pl.when(pl.program_id(1) == 0)
    def _():
        x = x_ref[...]
        ms = jnp.mean(x * x, axis=-1, keepdims=True)
        y = x * lax.rsqrt(ms + EPS) * g_ref[...]
        h = (y * (1.0 + mod_ref[1:2, :]) + mod_ref[0:1, :]).astype(BF16)
        h_ref[...] = h
        lr_ref[...] = _dot(h, wlr_ref[...]).astype(lr_ref.dtype)

    z_ref[...] = _dot(h_ref[...], w_ref[...]).astype(z_ref.dtype)


def _inproj(x2d, mod3, norm_g, w, wlr, row_of_tile, tm, tn):
    m, d = x2d.shape
    n = w.shape[1]
    return pl.pallas_call(
        _inproj_kernel,
        grid=(m // tm, n // tn),
        in_specs=[pl.BlockSpec((tm, d), lambda i, j: (i, 0)),
                  pl.BlockSpec((None, 3, d), lambda i, j: (row_of_tile(i), 0, 0)),
                  pl.BlockSpec((1, d), lambda i, j: (0, 0)),
                  pl.BlockSpec((d, tn), lambda i, j: (0, j)),
                  pl.BlockSpec((d, LANES), lambda i, j: (0, 0))],
        out_specs=[pl.BlockSpec((tm, tn), lambda i, j: (i, j)),
                   pl.BlockSpec((tm, LANES), lambda i, j: (i, 0))],
        out_shape=[jax.ShapeDtypeStruct((m, n), BF16),
                   jax.ShapeDtypeStruct((m, LANES), BF16)],
        scratch_shapes=[pltpu.VMEM((tm, d), BF16)],
        compiler_params=_cparams("arbitrary", "arbitrary"),
        name="inproj",
    )(x2d, mod3, norm_g.reshape(1, d), w, wlr)


def _pool_kernel(up_ref, uc_ref, un_ref, pg_ref, w_ref, sc_ref, o_ref, *, tile, seq_len):
    r = pl.program_id(1)
    half = jnp.left_shift(1, pl.program_id(2))
    base = r * tile
    u_all = jnp.concatenate([up_ref[...], uc_ref[...], un_ref[...]], axis=0)
    width = tile + 2 * POOL_HALO
    t = base + lax.broadcasted_iota(jnp.int32, (tile, width), 0)
    s = base - POOL_HALO + lax.broadcasted_iota(jnp.int32, (tile, width), 1)
    inside = (s >= jnp.maximum(t - half, 0)) & (s < jnp.minimum(t + half, seq_len))
    band = jnp.where(inside, 1.0, 0.0).astype(BF16)
    wsum = _dot(band, u_all)
    tc = base + lax.broadcasted_iota(jnp.int32, (tile, 1), 0)
    cnt = (jnp.minimum(tc + half, seq_len) - jnp.maximum(tc - half, 0)).astype(F32)
    dcen = wsum / cnt - uc_ref[...].astype(F32)
    y = _dot(dcen.astype(BF16), w_ref[...].astype(BF16)) * sc_ref[...]
    o_ref[...] = (y * _silu(pg_ref[...].astype(F32))).astype(o_ref.dtype)


def _pool(z3, col_u, col_g, pool_w, pool_scale, tile):
    nseq, seq_len, _ = z3.shape
    ngrp = len(POOL_WINDOWS)
    gw = pool_w.shape[-1]
    cu, cg = col_u // gw, col_g // gw
    nhalo = seq_len // POOL_HALO
    per = tile // POOL_HALO
    kern = functools.partial(_pool_kernel, tile=tile, seq_len=seq_len)
    return pl.pallas_call(
        kern,
        grid=(nseq, seq_len // tile, ngrp),
        in_specs=[
            pl.BlockSpec((None, POOL_HALO, gw), lambda s, r, g: (s, jnp.maximum(r * per - 1, 0), cu + g)),
            pl.BlockSpec((None, tile, gw), lambda s, r, g: (s, r, cu + g)),
            pl.BlockSpec((None, POOL_HALO, gw), lambda s, r, g: (s, jnp.minimum((r + 1) * per, nhalo - 1), cu + g)),
            pl.BlockSpec((None, tile, gw), lambda s, r, g: (s, r, cg + g)),
            pl.BlockSpec((None, gw, gw), lambda s, r, g: (g, 0, 0)),
            pl.BlockSpec((1, gw), lambda s, r, g: (0, g)),
        ],
        out_specs=pl.BlockSpec((None, tile, gw), lambda s, r, g: (s, r, g)),
        out_shape=jax.ShapeDtypeStruct((nseq, seq_len, ngrp * gw), BF16),
        compiler_params=_cparams("arbitrary", "arbitrary", "arbitrary"),
        name="pool",
    )(z3, z3, z3, z3, pool_w, pool_scale.reshape(1, ngrp * gw))


def _head_norm(t, gain, gmat):
    sq = t * t
    hi = sq.astype(BF16)
    lo = (sq - hi.astype(F32)).astype(BF16)
    ssq = _dot(hi, gmat) + _dot(lo, gmat)
    return t * lax.rsqrt(ssq * (1.0 / DIFF_HEAD_DIM) + EPS) * gain


def _rope(t, cos_t, sin_s, first_half):
    half = DIFF_HEAD_DIM // 2
    lower = pltpu.roll(t, half, 1)
    upper = pltpu.roll(t, LANES - half, 1)
    return t * cos_t + jnp.where(first_half, upper, lower) * sin_s


def _attn_kernel(*refs, tq, row_group, rope_q, has_latent, lambda_init):
    if has_latent:
        (q_ref, dg_ref, kc_ref, vc_ref, kl_ref, vl_ref, cosq_ref, sinq_ref, cosk_ref, sink_ref,
         qn_ref, kn_ref, sub_ref, lam_ref, o_ref, kcs_ref, vcs_ref, kls_ref, vls_ref) = refs
    else:
        (q_ref, dg_ref, kc_ref, vc_ref, qn_ref, kn_ref, sub_ref, lam_ref, o_ref, kcs_ref, vcs_ref) = refs
    hw = 2 * DIFF_HEAD_DIM

    lane = lax.broadcasted_iota(jnp.int32, (1, LANES), 1)
    first_half = (lane & (DIFF_HEAD_DIM - 1)) < (DIFF_HEAD_DIM // 2)
    sub1 = lane < DIFF_HEAD_DIM
    gi = jnp.where(lax.broadcasted_iota(jnp.int32, (LANES, LANES), 0) < DIFF_HEAD_DIM, 1.0, 0.0)
    gj = jnp.where(lax.broadcasted_iota(jnp.int32, (LANES, LANES), 1) < DIFF_HEAD_DIM, 1.0, 0.0)
    gmat = (gi * gj + (1.0 - gi) * (1.0 - gj)).astype(BF16)

    @pl.when(pl.program_id(2) == 0)
    def _():
        kcs_ref[...] = _head_norm(kc_ref[...].astype(F32), kn_ref[...], gmat).astype(BF16)
        vcs_ref[:, :hw] = vc_ref[...]
        vcs_ref[:, hw:] = jnp.ones((vc_ref.shape[0], hw), BF16)
        if has_latent:
            kl = _head_norm(kl_ref[...].astype(F32), kn_ref[...], gmat)
            kls_ref[...] = _rope(kl, cosk_ref[...], sink_ref[...], first_half).astype(BF16)
            vls_ref[:, :hw] = vl_ref[...]
            vls_ref[:, hw:] = jnp.ones((vl_ref.shape[0], hw), BF16)

    q = _head_norm(q_ref[...].astype(F32), qn_ref[...], gmat)
    if rope_q:
        q = _rope(q, cosq_ref[...], sinq_ref[...], first_half)
    q = q * (DIFF_HEAD_DIM ** -0.5 * LOG2E)
    qq = jnp.concatenate([jnp.where(sub1, q, 0.0), jnp.where(sub1, 0.0, q)], axis=0).astype(BF16)

    segs = [(kcs_ref, vcs_ref)] + ([(kls_ref, vls_ref)] if has_latent else [])
    pvs = []
    for r0 in range(0, 2 * tq, row_group):
        qg = qq[r0:r0 + row_group]
        scores = [_dot_nt(qg, k_ref[...]) for k_ref, _ in segs]
        mx = functools.reduce(jnp.maximum, [jnp.max(s, axis=-1, keepdims=True) for s in scores])
        acc = None
        for s, (_, v_ref) in zip(scores, segs):
            part = _dot(jnp.exp2(s - mx).astype(BF16), v_ref[...])
            acc = part if acc is None else acc + part
        pvs.append(acc[:, :hw] / acc[:, hw:hw + 1])
    pv = jnp.concatenate(pvs, axis=0)

    lam_p = lam_ref[...]
    lam = (jnp.exp(jnp.sum(lam_p[0:1] * lam_p[1:2], axis=-1, keepdims=True))
           - jnp.exp(jnp.sum(lam_p[2:3] * lam_p[3:4], axis=-1, keepdims=True)) + lambda_init)
    o = pv[:tq] - lam * pv[tq:]

    ms = jnp.mean(o * o, axis=-1, keepdims=True)
    y = o * lax.rsqrt(ms + EPS) * sub_ref[...] * (1.0 - lambda_init)
    o_ref[...] = (y * _silu(dg_ref[...].astype(F32))).astype(o_ref.dtype)


def _attn(zq3, cq, cdg, zc3, ckc, cvc, zl3, ckl, cvl, rope_tabs, q_norm, k_norm, subln, lam_p,
          lambda_init, tq):
    nb, lq, _ = zq3.shape
    lc = zc3.shape[1]
    has_latent = zl3 is not None
    hw = 2 * DIFF_HEAD_DIM
    b_of = lambda c: c // hw
    qmap = lambda off: (lambda b, h, i: (b, i, b_of(off) + h))
    kmap = lambda off: (lambda b, h, i: (b, 0, b_of(off) + h))
    const = lambda b, h, i: (0, 0)
    in_specs = [pl.BlockSpec((None, tq, hw), qmap(cq)),
                pl.BlockSpec((None, tq, hw), qmap(cdg)),
                pl.BlockSpec((None, lc, hw), kmap(ckc)),
                pl.BlockSpec((None, lc, hw), kmap(cvc))]
    args = [zq3, zq3, zc3, zc3]
    scratch = [pltpu.VMEM((lc, hw), BF16), pltpu.VMEM((lc, 2 * hw), BF16)]
    if has_latent:
        ll = zl3.shape[1]
        cos_t, sin_s = rope_tabs
        in_specs += [pl.BlockSpec((None, ll, hw), kmap(ckl)),
                     pl.BlockSpec((None, ll, hw), kmap(cvl)),
                     pl.BlockSpec((tq, hw), lambda b, h, i: (i, 0)),
                     pl.BlockSpec((tq, hw), lambda b, h, i: (i, 0)),
                     pl.BlockSpec((ll, hw), const),
                     pl.BlockSpec((ll, hw), const)]
        args += [zl3, zl3, cos_t, sin_s, cos_t, sin_s]
        scratch += [pltpu.VMEM((ll, hw), BF16), pltpu.VMEM((ll, 2 * hw), BF16)]
    in_specs += [pl.BlockSpec((1, hw), const)] * 3 + [pl.BlockSpec((4, DIFF_HEAD_DIM), const)]
    args += [jnp.tile(q_norm, 2).reshape(1, hw), jnp.tile(k_norm, 2).reshape(1, hw),
             subln.reshape(1, hw), lam_p]
    kern = functools.partial(_attn_kernel, tq=tq, row_group=min(2 * tq, 128), rope_q=has_latent, has_latent=has_latent,
                             lambda_init=lambda_init)
    return pl.pallas_call(
        kern,
        grid=(nb, DIFF_HEADS, lq // tq),
        in_specs=in_specs,
        out_specs=pl.BlockSpec((None, tq, hw), lambda b, h, i: (b, i, h)),
        out_shape=jax.ShapeDtypeStruct((nb, lq, DIFF_HEADS * hw), BF16),
        scratch_shapes=scratch,
        compiler_params=_cparams("arbitrary", "arbitrary", "arbitrary"),
        name="diff_attn",
    )(*args)


def _log_sigmoid(x):
    return jnp.minimum(x, 0.0) - jnp.log1p(jnp.exp(-jnp.abs(x)))


GLA_LEVELS = tuple(1 << i for i in range(GLA_CHUNK.bit_length() - 1))
GLA_SLOT_Q = len(GLA_LEVELS)
GLA_SLOT_K = GLA_SLOT_Q + 1
GLA_TOT_ROWS = 16


def _gla_tables():
    c = GLA_CHUNK
    t = np.arange(c)[:, None]
    j = np.arange(c)[None, :]
    sums, masks = [], []
    for rev in (False, True):
        rows, mk = [], []
        for m in GLA_LEVELS:
            base = t & ~(2 * m - 1)
            same = base == (j & ~(2 * m - 1))
            if not rev:
                bd = base + m - 1
                is_q = (t & m) != 0
                rows.append(np.where(is_q, (j > bd) & (j <= t), (j > t) & (j <= bd)))
                mk.append(same & is_q & ((j & m) == 0))
            else:
                bd = base + m
                is_q = (t & m) == 0
                rows.append(np.where(is_q, (j >= t) & (j < bd), (j >= bd) & (j < t)))
                mk.append(same & is_q & ((j & m) != 0))
        rows += [j >= t, j < t] if rev else [j <= t, j > t]
        rows.append(np.ones((GLA_TOT_ROWS, c), bool))
        mk.append(t == j)
        sums.append(np.concatenate(rows, axis=0))
        masks.append(np.stack(mk))
    return (jnp.asarray(np.stack(sums), BF16), jnp.asarray(np.stack(masks), F32))


def _gla_chunk(q, k, v, ghl, st, tab_ref, mask_ref, need_out):
    c = GLA_CHUNK
    first = 0 if need_out else GLA_SLOT_K
    x2 = _dot(tab_ref[first * c:, :], ghl)
    e = jnp.exp2(x2[:, :GLA_DK] + x2[:, GLA_DK:])
    slot = lambda i: e[(i - first) * c:(i - first + 1) * c]
    tot = e[-GLA_TOT_ROWS:1 - GLA_TOT_ROWS]
    kd = (k * slot(GLA_SLOT_K)).astype(BF16)
    st_new = st * tot + _dot_tn(v, kd)
    if not need_out:
        return None, st_new

    out = _dot_nt((q * slot(GLA_SLOT_Q)).astype(BF16), st.astype(BF16))
    att = _dot_nt(q.astype(BF16), k.astype(BF16)) * mask_ref[len(GLA_LEVELS)]
    for lvl in range(len(GLA_LEVELS)):
        el = slot(lvl)
        att = att + _dot_nt((q * el).astype(BF16), (k * el).astype(BF16)) * mask_ref[lvl]
    return out + _dot(att.astype(BF16), v), st_new


def _gla_kernel(*refs, n_ctx, n_lat, ctx_out):
    (qc_ref, kc_ref, vc_ref, ggc_ref, lrc_ref, ql_ref, kl_ref, vl_ref, ggl_ref, lrl_ref,
     w2_ref, b2_ref, gain_ref, tab_ref, mask_ref) = refs[:15]
    rest = refs[15:]
    if ctx_out:
        ol_ref, oc_ref, gl_ref, gc_ref, accl_ref, accc_ref, stf_ref, stb_ref = rest
    else:
        ol_ref, gl_ref, gc_ref, accl_ref, stf_ref, stb_ref = rest
        oc_ref = accc_ref = None

    def decays(lr_ref, g_ref):
        g = _log_sigmoid(_dot(lr_ref[...], w2_ref[...]) + b2_ref[...]) * (LOG2E / GLA_NORMALIZER)
        hi = g.astype(BF16)
        lo = (g - hi.astype(F32)).astype(BF16)
        for d in range(2):
            g_ref[:, (2 * d) * GLA_DK:(2 * d + 1) * GLA_DK] = hi[:, d * GLA_DK:(d + 1) * GLA_DK]
            g_ref[:, (2 * d + 1) * GLA_DK:(2 * d + 2) * GLA_DK] = lo[:, d * GLA_DK:(d + 1) * GLA_DK]

    decays(lrl_ref, gl_ref)
    decays(lrc_ref, gc_ref)
    stf_ref[...] = jnp.zeros_like(stf_ref)
    stb_ref[...] = jnp.zeros_like(stb_ref)
    scale = GLA_DK ** -0.5

    def scan(n, q_ref, k_ref, v_ref, g_ref, acc_ref):
        need_out = acc_ref is not None

        def step(i, first_visit):
            for c_idx, d, st_ref in ((i, 0, stf_ref), (n - 1 - i, 1, stb_ref)):
                rows = pl.ds(pl.multiple_of(c_idx * GLA_CHUNK, GLA_CHUNK), GLA_CHUNK)
                q = q_ref[rows, :].astype(F32) * scale
                k = k_ref[rows, :].astype(F32)
                ghl = g_ref[rows, 2 * d * GLA_DK:(2 * d + 2) * GLA_DK]
                out, st_new = _gla_chunk(q, k, v_ref[rows, :], ghl, st_ref[...], tab_ref.at[d], mask_ref.at[d],
                                         need_out)
                st_ref[...] = st_new
                if need_out:
                    if first_visit:
                        acc_ref[rows, :] = out
                    else:
                        acc_ref[rows, :] += out

        def first(i, carry):
            step(i, True)
            return carry

        def second(i, carry):
            step(i, False)
            return carry

        lax.fori_loop(0, n // 2, first, 0, unroll=2)
        lax.fori_loop(n // 2, n, second, 0, unroll=2)

    def finish(acc_ref, gg_ref, o_ref):
        o = acc_ref[...]
        ms = jnp.mean(o * o, axis=-1, keepdims=True)
        y = o * lax.rsqrt(ms + EPS) * gain_ref[...]
        o_ref[...] = (y * _silu(gg_ref[...].astype(F32))).astype(o_ref.dtype)

    scan(n_ctx, qc_ref, kc_ref, vc_ref, gc_ref, accc_ref)
    scan(n_lat, ql_ref, kl_ref, vl_ref, gl_ref, accl_ref)
    finish(accl_ref, ggl_ref, ol_ref)
    if ctx_out:
        finish(accc_ref, ggc_ref, oc_ref)


def _gla(zc3, cc, lrc3, zl3, cl, lrl3, w2, b2, gain, ctx_out):
    nb, lc, _ = zc3.shape
    ll = zl3.shape[1]
    assert lc % (2 * GLA_CHUNK) == 0 and ll % (2 * GLA_CHUNK) == 0
    tabs, masks = _gla_tables()

    def seq_specs(z3, cols, lr3, slen):
        qo, ko, vo, go = cols
        return ([pl.BlockSpec((None, slen, GLA_DK), lambda b, h: (b, 0, qo // GLA_DK + h)),
                 pl.BlockSpec((None, slen, GLA_DK), lambda b, h: (b, 0, ko // GLA_DK + h)),
                 pl.BlockSpec((None, slen, GLA_DV), lambda b, h: (b, 0, vo // GLA_DV + h)),
                 pl.BlockSpec((None, slen, GLA_DV), lambda b, h: (b, 0, (0 if go is None else go) // GLA_DV + h)),
                 pl.BlockSpec((None, slen, LANES), lambda b, h: (b, 0, 0))],
                [z3, z3, z3, z3, lr3])

    sc, ac = seq_specs(zc3, cc, lrc3, lc)
    sl, al = seq_specs(zl3, cl, lrl3, ll)
    in_specs = sc + sl + [pl.BlockSpec((None, LANES, 2 * GLA_DK), lambda b, h: (h, 0, 0)),
                          pl.BlockSpec((None, 1, 2 * GLA_DK), lambda b, h: (h, 0, 0)),
                          pl.BlockSpec((1, GLA_DV), lambda b, h: (0, 0)),
                          pl.BlockSpec(tabs.shape, lambda b, h: (0, 0, 0)),
                          pl.BlockSpec(masks.shape, lambda b, h: (0, 0, 0, 0))]
    args = ac + al + [w2, b2, gain.reshape(1, GLA_DV), tabs, masks]
    out_specs = [pl.BlockSpec((None, ll, GLA_DV), lambda b, h: (b, 0, h))]
    out_shape = [jax.ShapeDtypeStruct((nb, ll, GLA_HEADS * GLA_DV), BF16)]
    scratch = [pltpu.VMEM((ll, 4 * GLA_DK), BF16), pltpu.VMEM((lc, 4 * GLA_DK), BF16),
               pltpu.VMEM((ll, GLA_DV), F32)]
    if ctx_out:
        out_specs.append(pl.BlockSpec((None, lc, GLA_DV), lambda b, h: (b, 0, h)))
        out_shape.append(jax.ShapeDtypeStruct((nb, lc, GLA_HEADS * GLA_DV), BF16))
        scratch.append(pltpu.VMEM((lc, GLA_DV), F32))
    scratch += [pltpu.VMEM((GLA_DV, GLA_DK), F32), pltpu.VMEM((GLA_DV, GLA_DK), F32)]
    kern = functools.partial(_gla_kernel, n_ctx=lc // GLA_CHUNK, n_lat=ll // GLA_CHUNK, ctx_out=ctx_out)
    outs = pl.pallas_call(
        kern,
        grid=(nb, GLA_HEADS),
        in_specs=in_specs,
        out_specs=out_specs,
        out_shape=out_shape,
        scratch_shapes=scratch,
        compiler_params=_cparams("arbitrary", "arbitrary"),
        name="gla",
    )(*args)
    return (outs[0], outs[1]) if ctx_out else (outs[0], None)


def _merge_kernel(p_ref, d_ref, g_ref, wp_ref, wd_ref, wg_ref, mp_ref, md_ref, mg_ref, y_ref):
    y = (_sigmoid(mp_ref[...].astype(F32)) * _dot(p_ref[...], wp_ref[...])
         + _sigmoid(md_ref[...].astype(F32)) * _dot(d_ref[...], wd_ref[...])
         + _sigmoid(mg_ref[...].astype(F32)) * _dot(g_ref[...], wg_ref[...]))
    y_ref[...] = y.astype(y_ref.dtype)


def _merge(pool_o, diff_o, gla_o, z2d, col_mg, wbp, wbd, wbg, tm, tn):
    m, kw = pool_o.shape
    d = wbp.shape[1]
    act = pl.BlockSpec((tm, kw), lambda i, j: (i, 0))
    wsp = pl.BlockSpec((kw, tn), lambda i, j: (0, j))
    gate = lambda k: pl.BlockSpec((tm, tn), lambda i, j: (i, (col_mg + k * d) // tn + j))
    return pl.pallas_call(
        _merge_kernel,
        grid=(m // tm, d // tn),
        in_specs=[act, act, act, wsp, wsp, wsp, gate(0), gate(1), gate(2)],
        out_specs=pl.BlockSpec((tm, tn), lambda i, j: (i, j)),
        out_shape=jax.ShapeDtypeStruct((m, d), BF16),
        compiler_params=_cparams("arbitrary", "arbitrary"),
        name="merge",
    )(pool_o, diff_o, gla_o, wbp, wbd, wbg, z2d, z2d, z2d)


def _outproj_kernel(y_ref, w_ref, x_ref, mod_ref, o_ref):
    o_ref[...] = x_ref[...] + mod_ref[2:3, :] * _dot(y_ref[...], w_ref[...])


def _outproj(y, w_out, x2d, mod3, row_of_tile, tm, tn):
    m, d = x2d.shape
    return pl.pallas_call(
        _outproj_kernel,
        grid=(m // tm, d // tn),
        in_specs=[pl.BlockSpec((tm, d), lambda i, j: (i, 0)),
                  pl.BlockSpec((d, tn), lambda i, j: (0, j)),
                  pl.BlockSpec((tm, tn), lambda i, j: (i, j)),
                  pl.BlockSpec((None, 3, tn), lambda i, j: (row_of_tile(i), 0, j))],
        out_specs=pl.BlockSpec((tm, tn), lambda i, j: (i, j)),
        out_shape=jax.ShapeDtypeStruct((m, d), F32),
        compiler_params=_cparams("arbitrary", "arbitrary"),
        name="outproj",
    )(y, w_out, x2d, mod3)


def _rope_tables(seq_len):
    n_freq = DIFF_HEAD_DIM // 4
    t = jnp.arange(seq_len)
    inv = ROPE_THETA ** (-jnp.arange(n_freq, dtype=F32) / n_freq)
    ang = jnp.concatenate([(t // GRID_W).astype(F32)[:, None] * inv,
                           (t % GRID_W).astype(F32)[:, None] * inv], axis=-1)
    cos, sin = jnp.cos(ang), jnp.sin(ang)
    return jnp.tile(cos, (1, 4)), jnp.tile(jnp.concatenate([-sin, sin], axis=-1), (1, 2))


def _largest_tile(n, cap, mult):
    t = min(n, cap)
    while n % t or t % mult:
        t -= mult
    return t


def kernel(x, c, ctx, c_ctx, norm_g, w_ada, b_ada, w_in, pool_w, pool_scale, diff_q_norm, diff_k_norm, diff_lam_q1, diff_lam_k1, diff_lam_q2, diff_lam_k2, diff_subln, gla_w_gate_f, gla_b_gate_f, gla_w_gate_b, gla_b_gate_b, gla_norm, w_branch_pool, w_branch_diff, w_branch_gla, w_out):
    nb, seq, d = x.shape
    lc = ctx.shape[1]
    depth = w_in.shape[0]
    pw = pool_scale.shape[1]
    dw = DIFF_HEADS * 2 * DIFF_HEAD_DIM
    gkw, gvw = GLA_HEADS * GLA_DK, GLA_HEADS * GLA_DV

    sizes = dict(pu=pw, pg=pw, dq=dw, dk=dw, dv=dw, dg=dw, gq=gkw, gk=gkw, gv=gvw, gg=gvw)
    col, off = {}, 0
    for name, size in sizes.items():
        col[name] = off
        off += size
    lr_lo, lr_hi = off, off + 2 * GLA_RANK
    col["mg"] = off
    n_main = off + 3 * d

    rope_tabs = _rope_tables(seq)
    cc = jnp.zeros((8, d), F32).at[:nb].set(c).at[nb].set(c_ctx)
    x2d = x.reshape(nb * seq, d)
    ctx2d = ctx.reshape(nb * lc, d)

    tm_x = _largest_tile(seq, 1024, 16)
    tm_c = _largest_tile(lc, 1024, 16)
    row_x = lambda tm: (lambda i: (i * tm) // seq)
    row_c = lambda i: nb

    for l in range(depth):
        last = l == depth - 1
        lambda_init = 0.8 - 0.6 * math.exp(-0.3 * l)
        mod3 = _ada(cc, w_ada[l], b_ada[l]).reshape(8, 3, d)

        wl = w_in[l]
        w_main = jnp.concatenate([wl[:, :lr_lo], wl[:, lr_hi:]], axis=1).astype(BF16)
        w_lr = jnp.pad(wl[:, lr_lo:lr_hi], ((0, 0), (0, LANES - 2 * GLA_RANK))).astype(BF16)
        if last:
            w_ctx = jnp.concatenate([w_main[:, col["dk"]:col["dg"]], w_main[:, col["gk"]:col["gg"]]], axis=1)
            ccol = dict(dk=0, dv=dw, gk=2 * dw, gv=2 * dw + gkw, gq=2 * dw, gg=None)
        else:
            w_ctx, ccol = w_main, col

        z, lr = _inproj(x2d, mod3, norm_g[l], w_main, w_lr, row_x(tm_x), tm_x,
                        _largest_tile(n_main, 1024, 256))
        zc, lrc = _inproj(ctx2d, mod3, norm_g[l], w_ctx, w_lr, row_c, tm_c,
                          _largest_tile(w_ctx.shape[1], 1024, 256))
        z3, zc3 = z.reshape(nb, seq, -1), zc.reshape(nb, lc, -1)
        lr3, lrc3 = lr.reshape(nb, seq, LANES), lrc.reshape(nb, lc, LANES)

        wbp, wbd, wbg = (w.astype(BF16) for w in (w_branch_pool[l], w_branch_diff[l], w_branch_gla[l]))
        wo = w_out[l].astype(BF16)
        lam_p = jnp.stack([diff_lam_q1[l], diff_lam_k1[l], diff_lam_q2[l], diff_lam_k2[l]])

        w2 = jnp.zeros((GLA_HEADS, LANES, 2 * GLA_DK), F32)
        w2 = w2.at[:, :GLA_RANK, :GLA_DK].set(gla_w_gate_f[l].reshape(GLA_RANK, GLA_HEADS, GLA_DK).transpose(1, 0, 2))
        w2 = w2.at[:, GLA_RANK:2 * GLA_RANK, GLA_DK:].set(
            gla_w_gate_b[l].reshape(GLA_RANK, GLA_HEADS, GLA_DK).transpose(1, 0, 2))
        b2 = jnp.concatenate([gla_b_gate_f[l].reshape(GLA_HEADS, 1, GLA_DK),
                              gla_b_gate_b[l].reshape(GLA_HEADS, 1, GLA_DK)], axis=-1)

        pool_l = _pool(z3, col["pu"], col["pg"], pool_w[l], pool_scale[l], _largest_tile(seq, 512, 16))
        diff_l = _attn(z3, col["dq"], col["dg"], zc3, ccol["dk"], ccol["dv"], z3, col["dk"], col["dv"],
                       rope_tabs, diff_q_norm[l], diff_k_norm[l], diff_subln[l], lam_p, lambda_init,
                       _largest_tile(seq, 512, 16))
        gla_l, gla_c = _gla(zc3, (ccol["gq"], ccol["gk"], ccol["gv"], ccol["gg"]), lrc3,
                            z3, (col["gq"], col["gk"], col["gv"], col["gg"]), lr3,
                            w2.astype(BF16), b2, gla_norm[l], ctx_out=not last)

        tm_m = _largest_tile(seq, 512, 16)
        y = _merge(pool_l.reshape(nb * seq, pw), diff_l.reshape(nb * seq, dw), gla_l.reshape(nb * seq, gvw),
                   z, col["mg"], wbp, wbd, wbg, tm_m, 1024)
        x2d_new = _outproj(y, wo, x2d, mod3, row_x(tm_m), tm_m, 1024)

        if not last:
            pool_c = _pool(zc3, col["pu"], col["pg"], pool_w[l], pool_scale[l], _largest_tile(lc, 512, 16))
            diff_c = _attn(zc3, col["dq"], col["dg"], zc3, col["dk"], col["dv"], None, None, None,
                           None, diff_q_norm[l], diff_k_norm[l], diff_subln[l], lam_p, lambda_init,
                           _largest_tile(lc, 256, 16))
            tm_mc = _largest_tile(lc, 512, 16)
            y_c = _merge(pool_c.reshape(nb * lc, pw), diff_c.reshape(nb * lc, dw), gla_c.reshape(nb * lc, gvw),
                         zc, col["mg"], wbp, wbd, wbg, tm_mc, 1024)
            ctx2d = _outproj(y_c, wo, ctx2d, mod3, row_c, tm_mc, 1024)
        x2d = x2d_new

    return x2d.reshape(nb, seq, d)
```

```python
import functools
import math

import jax
import jax.numpy as jnp
import numpy as np
from jax import lax
from jax.experimental import pallas as pl
from jax.experimental.pallas import tpu as pltpu

F32 = jnp.float32
BF16 = jnp.bfloat16

EPS = 1e-6
GRID_W = 64
ROPE_THETA = 10000.0

POOL_WINDOWS = (2, 4, 8, 16)
POOL_HALO = 16
DIFF_HEADS = 8
DIFF_HEAD_DIM = 64
GLA_HEADS = 4
GLA_DK = 128
GLA_DV = 256
GLA_RANK = 16
GLA_NORMALIZER = 16.0
GLA_CHUNK = 64
LANES = 128
LOG2E = math.log2(math.e)

VMEM_LIMIT = 48 * 1024 * 1024


def _cparams(*sem):
    return pltpu.CompilerParams(dimension_semantics=sem, vmem_limit_bytes=VMEM_LIMIT)


def _sigmoid(x):
    return 1.0 / (1.0 + jnp.exp(-x))


def _silu(x):
    return x * _sigmoid(x)


def _dot(a, b):
    return jnp.dot(a, b, preferred_element_type=F32)


def _dot_nt(a, b):
    return lax.dot_general(a, b, (((1,), (1,)), ((), ())), preferred_element_type=F32)


def _dot_tn(a, b):
    return lax.dot_general(a, b, (((0,), (0,)), ((), ())), preferred_element_type=F32)


def _ada_kernel(cc_ref, w_ref, b_ref, o_ref):
    a = _silu(cc_ref[...]).astype(BF16)
    o_ref[...] = _dot(a, w_ref[...].astype(BF16)) + b_ref[...]


def _ada(cc, w_ada, b_ada3, l, tn=768):
    rows, d = cc.shape
    n = w_ada.shape[2]
    return pl.pallas_call(
        _ada_kernel,
        grid=(n // tn,),
        in_specs=[pl.BlockSpec((rows, d), lambda j: (0, 0)),
                  pl.BlockSpec((None, d, tn), lambda j: (l, 0, j)),
                  pl.BlockSpec((None, 1, tn), lambda j: (l, 0, j))],
        out_specs=pl.BlockSpec((rows, tn), lambda j: (0, j)),
        out_shape=jax.ShapeDtypeStruct((rows, n), F32),
        compiler_params=_cparams("arbitrary"),
        name="ada",
    )(cc, w_ada, b_ada3)


def _inproj_kernel(x_ref, mod_ref, g_ref, w_ref, wlr_ref, z_ref, lr_ref, h_ref):
    @pl.when(pl.program_id(1) == 0)
    def _():
        x = x_ref[...]
        ms = jnp.mean(x * x, axis=-1, keepdims=True)
        y = x * lax.rsqrt(ms + EPS) * g_ref[...]
        h = (y * (1.0 + mod_ref[1:2, :]) + mod_ref[0:1, :]).astype(BF16)
        h_ref[...] = h
        lr_ref[...] = _dot(h, wlr_ref[...]).astype(lr_ref.dtype)

    z_ref[...] = _dot(h_ref[...], w_ref[...]).astype(z_ref.dtype)


def _inproj(x2d, mod3, norm_g3, l, w_all, lr_col, row_of_tile, tm, tn, tile_of, n_tiles):
    m, d = x2d.shape
    return pl.pallas_call(
        _inproj_kernel,
        grid=(m // tm, n_tiles),
        in_specs=[pl.BlockSpec((tm, d), lambda i, j: (i, 0)),
                  pl.BlockSpec((None, 3, d), lambda i, j: (row_of_tile(i), 0, 0)),
                  pl.BlockSpec((None, 1, d), lambda i, j: (l, 0, 0)),
                  pl.BlockSpec((None, d, tn), lambda i, j: (l, 0, tile_of(j))),
                  pl.BlockSpec((None, d, LANES), lambda i, j: (l, 0, lr_col // LANES))],
        out_specs=[pl.BlockSpec((tm, tn), lambda i, j: (i, j)),
                   pl.BlockSpec((tm, LANES), lambda i, j: (i, 0))],
        out_shape=[jax.ShapeDtypeStruct((m, n_tiles * tn), BF16),
                   jax.ShapeDtypeStruct((m, LANES), BF16)],
        scratch_shapes=[pltpu.VMEM((tm, d), BF16)],
        compiler_params=_cparams("arbitrary", "arbitrary"),
        name="inproj",
    )(x2d, mod3, norm_g3, w_all, w_all)


def _pool_kernel(up_ref, uc_ref, un_ref, pg_ref, w_ref, sc_ref, o_ref, *, tile, seq_len):
    r = pl.program_id(1)
    half = jnp.left_shift(1, pl.program_id(2))
    base = r * tile
    u_all = jnp.concatenate([up_ref[...], uc_ref[...], un_ref[...]], axis=0)
    width = tile + 2 * POOL_HALO
    t = base + lax.broadcasted_iota(jnp.int32, (tile, width), 0)
    s = base - POOL_HALO + lax.broadcasted_iota(jnp.int32, (tile, width), 1)
    inside = (s >= jnp.maximum(t - half, 0)) & (s < jnp.minimum(t + half, seq_len))
    band = jnp.where(inside, 1.0, 0.0).astype(BF16)
    wsum = _dot(band, u_all)
    tc = base + lax.broadcasted_iota(jnp.int32, (tile, 1), 0)
    cnt = (jnp.minimum(tc + half, seq_len) - jnp.maximum(tc - half, 0)).astype(F32)
    dcen = wsum / cnt - uc_ref[...].astype(F32)
    y = _dot(dcen.astype(BF16), w_ref[...].astype(BF16)) * sc_ref[...]
    o_ref[...] = (y * _silu(pg_ref[...].astype(F32))).astype(o_ref.dtype)


def _pool(z3, col_u, col_g, pool_w, pool_scale3, l, tile):
    nseq, seq_len, _ = z3.shape
    ngrp = len(POOL_WINDOWS)
    gw = pool_w.shape[-1]
    cu, cg = col_u // gw, col_g // gw
    nhalo = seq_len // POOL_HALO
    per = tile // POOL_HALO
    kern = functools.partial(_pool_kernel, tile=tile, seq_len=seq_len)
    return pl.pallas_call(
        kern,
        grid=(nseq, seq_len // tile, ngrp),
        in_specs=[
            pl.BlockSpec((None, POOL_HALO, gw), lambda s, r, g: (s, jnp.maximum(r * per - 1, 0), cu + g)),
            pl.BlockSpec((None, tile, gw), lambda s, r, g: (s, r, cu + g)),
            pl.BlockSpec((None, POOL_HALO, gw), lambda s, r, g: (s, jnp.minimum((r + 1) * per, nhalo - 1), cu + g)),
            pl.BlockSpec((None, tile, gw), lambda s, r, g: (s, r, cg + g)),
            pl.BlockSpec((None, None, gw, gw), lambda s, r, g: (l, g, 0, 0)),
            pl.BlockSpec((None, 1, gw), lambda s, r, g: (l, 0, g)),
        ],
        out_specs=pl.BlockSpec((None, tile, gw), lambda s, r, g: (s, r, g)),
        out_shape=jax.ShapeDtypeStruct((nseq, seq_len, ngrp * gw), BF16),
        compiler_params=_cparams("arbitrary", "arbitrary", "arbitrary"),
        name="pool",
    )(z3, z3, z3, z3, pool_w, pool_scale3)


def _head_norm(t, gain, gmat):
    sq = t * t
    hi = sq.astype(BF16)
    lo = (sq - hi.astype(F32)).astype(BF16)
    ssq = _dot(hi, gmat) + _dot(lo, gmat)
    return t * lax.rsqrt(ssq * (1.0 / DIFF_HEAD_DIM) + EPS) * gain


def _rope(t, cos_t, sin_s, first_half):
    half = DIFF_HEAD_DIM // 2
    lower = pltpu.roll(t, half, 1)
    upper = pltpu.roll(t, LANES - half, 1)
    return t * cos_t + jnp.where(first_half, upper, lower) * sin_s


def _attn_kernel(*refs, tq, row_group, rope_q, has_latent, lambda_init):
    if has_latent:
        (q_ref, dg_ref, kc_ref, vc_ref, kl_ref, vl_ref, cosq_ref, sinq_ref, cosk_ref, sink_ref,
         gains_ref, lam_ref, o_ref, kcs_ref, vcs_ref, kls_ref, vls_ref) = refs
    else:
        (q_ref, dg_ref, kc_ref, vc_ref, gains_ref, lam_ref, o_ref, kcs_ref, vcs_ref) = refs
    hw = 2 * DIFF_HEAD_DIM
    q_gain, k_gain, sub_gain = gains_ref[0:1, :], gains_ref[1:2, :], gains_ref[2:3, :]

    lane = lax.broadcasted_iota(jnp.int32, (1, LANES), 1)
    first_half = (lane & (DIFF_HEAD_DIM - 1)) < (DIFF_HEAD_DIM // 2)
    sub1 = lane < DIFF_HEAD_DIM
    gi = jnp.where(lax.broadcasted_iota(jnp.int32, (LANES, LANES), 0) < DIFF_HEAD_DIM, 1.0, 0.0)
    gj = jnp.where(lax.broadcasted_iota(jnp.int32, (LANES, LANES), 1) < DIFF_HEAD_DIM, 1.0, 0.0)
    gmat = (gi * gj + (1.0 - gi) * (1.0 - gj)).astype(BF16)

    @pl.when(pl.program_id(2) == 0)
    def _():
        kcs_ref[...] = _head_norm(kc_ref[...].astype(F32), k_gain, gmat).astype(BF16)
        vcs_ref[:, :hw] = vc_ref[...]
        vcs_ref[:, hw:] = jnp.ones((vc_ref.shape[0], hw), BF16)
        if has_latent:
            kl = _head_norm(kl_ref[...].astype(F32), k_gain, gmat)
            kls_ref[...] = _rope(kl, cosk_ref[...], sink_ref[...], first_half).astype(BF16)
            vls_ref[:, :hw] = vl_ref[...]
            vls_ref[:, hw:] = jnp.ones((vl_ref.shape[0], hw), BF16)

    q = _head_norm(q_ref[...].astype(F32), q_gain, gmat)
    if rope_q:
        q = _rope(q, cosq_ref[...], sinq_ref[...], first_half)
    q = q * (DIFF_HEAD_DIM ** -0.5 * LOG2E)
    qq = jnp.concatenate([jnp.where(sub1, q, 0.0), jnp.where(sub1, 0.0, q)], axis=0).astype(BF16)

    segs = [(kcs_ref, vcs_ref)] + ([(kls_ref, vls_ref)] if has_latent else [])
    pvs = []
    for r0 in range(0, 2 * tq, row_group):
        qg = qq[r0:r0 + row_group]
        scores = [_dot_nt(qg, k_ref[...]) for k_ref, _ in segs]
        mx = functools.reduce(jnp.maximum, [jnp.max(s, axis=-1, keepdims=True) for s in scores])
        acc = None
        for s, (_, v_ref) in zip(scores, segs):
            part = _dot(jnp.exp2(s - mx).astype(BF16), v_ref[...])
            acc = part if acc is None else acc + part
        pvs.append(acc[:, :hw] / acc[:, hw:hw + 1])
    pv = jnp.concatenate(pvs, axis=0)

    lam_p = lam_ref[...]
    lam = (jnp.exp(jnp.sum(lam_p[0:1] * lam_p[1:2], axis=-1, keepdims=True))
           - jnp.exp(jnp.sum(lam_p[2:3] * lam_p[3:4], axis=-1, keepdims=True)) + lambda_init)
    o = pv[:tq] - lam * pv[tq:]

    ms = jnp.mean(o * o, axis=-1, keepdims=True)
    y = o * lax.rsqrt(ms + EPS) * sub_gain * (1.0 - lambda_init)
    o_ref[...] = (y * _silu(dg_ref[...].astype(F32))).astype(o_ref.dtype)


def _attn(zq3, cq, cdg, zc3, ckc, cvc, zl3, ckl, cvl, rope_tabs, gains3, lam4, l, lambda_init, tq):
    nb, lq, _ = zq3.shape
    lc = zc3.shape[1]
    has_latent = zl3 is not None
    hw = 2 * DIFF_HEAD_DIM
    b_of = lambda c: c // hw
    qmap = lambda off: (lambda b, h, i: (b, i, b_of(off) + h))
    kmap = lambda off: (lambda b, h, i: (b, 0, b_of(off) + h))
    const = lambda b, h, i: (0, 0)
    in_specs = [pl.BlockSpec((None, tq, hw), qmap(cq)),
                pl.BlockSpec((None, tq, hw), qmap(cdg)),
                pl.BlockSpec((None, lc, hw), kmap(ckc)),
                pl.BlockSpec((None, lc, hw), kmap(cvc))]
    args = [zq3, zq3, zc3, zc3]
    scratch = [pltpu.VMEM((lc, hw), BF16), pltpu.VMEM((lc, 2 * hw), BF16)]
    if has_latent:
        ll = zl3.shape[1]
        cos_t, sin_s = rope_tabs
        in_specs += [pl.BlockSpec((None, ll, hw), kmap(ckl)),
                     pl.BlockSpec((None, ll, hw), kmap(cvl)),
                     pl.BlockSpec((tq, hw), lambda b, h, i: (i, 0)),
                     pl.BlockSpec((tq, hw), lambda b, h, i: (i, 0)),
                     pl.BlockSpec((ll, hw), const),
                     pl.BlockSpec((ll, hw), const)]
        args += [zl3, zl3, cos_t, sin_s, cos_t, sin_s]
        scratch += [pltpu.VMEM((ll, hw), BF16), pltpu.VMEM((ll, 2 * hw), BF16)]
    in_specs += [pl.BlockSpec((None, 3, hw), lambda b, h, i: (l, 0, 0)),
                 pl.BlockSpec((None, 4, DIFF_HEAD_DIM), lambda b, h, i: (l, 0, 0))]
    args += [gains3, lam4]
    kern = functools.partial(_attn_kernel, tq=tq, row_group=min(2 * tq, 128), rope_q=has_latent, has_latent=has_latent,
                             lambda_init=lambda_init)
    return pl.pallas_call(
        kern,
        grid=(nb, DIFF_HEADS, lq // tq),
        in_specs=in_specs,
        out_specs=pl.BlockSpec((None, tq, hw), lambda b, h, i: (b, i, h)),
        out_shape=jax.ShapeDtypeStruct((nb, lq, DIFF_HEADS * hw), BF16),
        scratch_shapes=scratch,
        compiler_params=_cparams("arbitrary", "arbitrary", "arbitrary"),
        name="diff_attn",
    )(*args)


def _log_sigmoid(x):
    return jnp.minimum(x, 0.0) - jnp.log1p(jnp.exp(-jnp.abs(x)))


GLA_LEVELS = tuple(1 << i for i in range(GLA_CHUNK.bit_length() - 1))
GLA_TAB_LEVELS = tuple(m for m in GLA_LEVELS if m < 8)
GLA_ROW_LEVELS = tuple(m for m in GLA_LEVELS if m >= 8)
GLA_SLOT_CUM = len(GLA_TAB_LEVELS)
GLA_UNROLL = 4
GLA_TOT_ROWS = 16


def _gla_boundary(m, rev):
    return m if rev else m - 1


def _gla_tables():
    c = GLA_CHUNK
    t = np.arange(c)[:, None]
    j = np.arange(c)[None, :]
    sums, masks = [], []
    for rev in (False, True):
        rows, mk = [], []
        for m in GLA_LEVELS:
            base = t & ~(2 * m - 1)
            bd = base + _gla_boundary(m, rev)
            is_q = ((t & m) == 0) if rev else ((t & m) != 0)
            if rev:
                between = np.where(is_q, (j >= t) & (j < bd), (j >= bd) & (j < t))
            else:
                between = np.where(is_q, (j > bd) & (j <= t), (j > t) & (j <= bd))
            if m in GLA_TAB_LEVELS:
                rows.append(between)
            mk.append((base == (j & ~(2 * m - 1))) & is_q & (((j & m) != 0) if rev else ((j & m) == 0)))
        rows.append(j >= t if rev else j <= t)
        rows.append(np.ones((GLA_TOT_ROWS, c), bool))
        mk.append(t == j)
        sums.append(np.concatenate(rows, axis=0))
        masks.append(np.stack(mk))
    return (jnp.asarray(np.stack(sums), BF16), jnp.asarray(np.stack(masks), F32))


def _gla_chunk(q, k, v, ghl, st, tab_ref, mask_ref, rev, need_out):
    c = GLA_CHUNK
    first = 0 if need_out else GLA_SLOT_CUM
    x2 = _dot(tab_ref[first * c:, :], ghl)
    x = x2[:, :GLA_DK] + x2[:, GLA_DK:]
    cum = x[(GLA_SLOT_CUM - first) * c:(GLA_SLOT_CUM - first + 1) * c]
    tot = x[-GLA_TOT_ROWS:1 - GLA_TOT_ROWS]
    kd = (k * jnp.exp2(tot - cum)).astype(BF16)
    st_new = st * jnp.exp2(tot) + _dot_tn(v, kd)
    if not need_out:
        return None, st_new

    out = _dot_nt((q * jnp.exp2(cum)).astype(BF16), st.astype(BF16))
    att = _dot_nt(q.astype(BF16), k.astype(BF16)) * mask_ref[len(GLA_LEVELS)]
    for lvl, m in enumerate(GLA_LEVELS):
        if m in GLA_TAB_LEVELS:
            xl = x[lvl * c:(lvl + 1) * c]
        else:
            anchor = jnp.concatenate(
                [jnp.broadcast_to(cum[b0 + _gla_boundary(m, rev):b0 + _gla_boundary(m, rev) + 1], (2 * m, GLA_DK))
                 for b0 in range(0, c, 2 * m)], axis=0)
            xl = -jnp.abs(cum - anchor)
        el = jnp.exp2(xl)
        att = att + _dot_nt((q * el).astype(BF16), (k * el).astype(BF16)) * mask_ref[lvl]
    return out + _dot(att.astype(BF16), v), st_new


def _gla_kernel(*refs, n_ctx, n_lat, ctx_out):
    (qc_ref, kc_ref, vc_ref, ggc_ref, lrc_ref, ql_ref, kl_ref, vl_ref, ggl_ref, lrl_ref,
     w2_ref, b2_ref, gain_ref, tab_ref, mask_ref) = refs[:15]
    rest = refs[15:]
    if ctx_out:
        ol_ref, oc_ref, gl_ref, gc_ref, accl_ref, accc_ref, stf_ref, stb_ref = rest
    else:
        ol_ref, gl_ref, gc_ref, accl_ref, stf_ref, stb_ref = rest
        oc_ref = accc_ref = None

    def decays(lr_ref, g_ref):
        g = _log_sigmoid(_dot(lr_ref[...], w2_ref[...]) + b2_ref[...]) * (LOG2E / GLA_NORMALIZER)
        hi = g.astype(BF16)
        lo = (g - hi.astype(F32)).astype(BF16)
        for d in range(2):
            g_ref[:, (2 * d) * GLA_DK:(2 * d + 1) * GLA_DK] = hi[:, d * GLA_DK:(d + 1) * GLA_DK]
            g_ref[:, (2 * d + 1) * GLA_DK:(2 * d + 2) * GLA_DK] = lo[:, d * GLA_DK:(d + 1) * GLA_DK]

    decays(lrl_ref, gl_ref)
    decays(lrc_ref, gc_ref)
    stf_ref[...] = jnp.zeros_like(stf_ref)
    stb_ref[...] = jnp.zeros_like(stb_ref)
    scale = GLA_DK ** -0.5

    def scan(n, q_ref, k_ref, v_ref, g_ref, acc_ref):
        need_out = acc_ref is not None

        def step(i, first_visit):
            for c_idx, d, st_ref in ((i, 0, stf_ref), (n - 1 - i, 1, stb_ref)):
                rows = pl.ds(pl.multiple_of(c_idx * GLA_CHUNK, GLA_CHUNK), GLA_CHUNK)
                q = q_ref[rows, :].astype(F32) * scale
                k = k_ref[rows, :].astype(F32)
                ghl = g_ref[rows, 2 * d * GLA_DK:(2 * d + 2) * GLA_DK]
                out, st_new = _gla_chunk(q, k, v_ref[rows, :], ghl, st_ref[...], tab_ref.at[d], mask_ref.at[d],
                                         d == 1, need_out)
                st_ref[...] = st_new
                if need_out:
                    if first_visit:
                        acc_ref[rows, :] = out
                    else:
                        acc_ref[rows, :] += out

        def first(i, carry):
            step(i, True)
            return carry

        def second(i, carry):
            step(i, False)
            return carry

        lax.fori_loop(0, n // 2, first, 0, unroll=min(GLA_UNROLL, n // 2))
        lax.fori_loop(n // 2, n, second, 0, unroll=min(GLA_UNROLL, n // 2))

    def finish(acc_ref, gg_ref, o_ref):
        o = acc_ref[...]
        ms = jnp.mean(o * o, axis=-1, keepdims=True)
        y = o * lax.rsqrt(ms + EPS) * gain_ref[...]
        o_ref[...] = (y * _silu(gg_ref[...].astype(F32))).astype(o_ref.dtype)

    scan(n_ctx, qc_ref, kc_ref, vc_ref, gc_ref, accc_ref)
    scan(n_lat, ql_ref, kl_ref, vl_ref, gl_ref, accl_ref)
    finish(accl_ref, ggl_ref, ol_ref)
    if ctx_out:
        finish(accc_ref, ggc_ref, oc_ref)


def _gla(zc3, cc, lrc3, zl3, cl, lrl3, w2, b2, gain3, l, ctx_out):
    nb, lc, _ = zc3.shape
    ll = zl3.shape[1]
    assert lc % (2 * GLA_CHUNK) == 0 and ll % (2 * GLA_CHUNK) == 0
    tabs, masks = _gla_tables()

    def seq_specs(z3, cols, lr3, slen):
        qo, ko, vo, go = cols
        return ([pl.BlockSpec((None, slen, GLA_DK), lambda b, h: (b, 0, qo // GLA_DK + h)),
                 pl.BlockSpec((None, slen, GLA_DK), lambda b, h: (b, 0, ko // GLA_DK + h)),
                 pl.BlockSpec((None, slen, GLA_DV), lambda b, h: (b, 0, vo // GLA_DV + h)),
                 pl.BlockSpec((None, slen, GLA_DV), lambda b, h: (b, 0, (0 if go is None else go) // GLA_DV + h)),
                 pl.BlockSpec((None, slen, LANES), lambda b, h: (b, 0, 0))],
                [z3, z3, z3, z3, lr3])

    sc, ac = seq_specs(zc3, cc, lrc3, lc)
    sl, al = seq_specs(zl3, cl, lrl3, ll)
    in_specs = sc + sl + [pl.BlockSpec((None, None, LANES, 2 * GLA_DK), lambda b, h: (l, h, 0, 0)),
                          pl.BlockSpec((None, None, 1, 2 * GLA_DK), lambda b, h: (l, h, 0, 0)),
                          pl.BlockSpec((None, 1, GLA_DV), lambda b, h: (l, 0, 0)),
                          pl.BlockSpec(tabs.shape, lambda b, h: (0, 0, 0)),
                          pl.BlockSpec(masks.shape, lambda b, h: (0, 0, 0, 0))]
    args = ac + al + [w2, b2, gain3, tabs, masks]
    out_specs = [pl.BlockSpec((None, ll, GLA_DV), lambda b, h: (b, 0, h))]
    out_shape = [jax.ShapeDtypeStruct((nb, ll, GLA_HEADS * GLA_DV), BF16)]
    scratch = [pltpu.VMEM((ll, 4 * GLA_DK), BF16), pltpu.VMEM((lc, 4 * GLA_DK), BF16),
               pltpu.VMEM((ll, GLA_DV), F32)]
    if ctx_out:
        out_specs.append(pl.BlockSpec((None, lc, GLA_DV), lambda b, h: (b, 0, h)))
        out_shape.append(jax.ShapeDtypeStruct((nb, lc, GLA_HEADS * GLA_DV), BF16))
        scratch.append(pltpu.VMEM((lc, GLA_DV), F32))
    scratch += [pltpu.VMEM((GLA_DV, GLA_DK), F32), pltpu.VMEM((GLA_DV, GLA_DK), F32)]
    kern = functools.partial(_gla_kernel, n_ctx=lc // GLA_CHUNK, n_lat=ll // GLA_CHUNK, ctx_out=ctx_out)
    outs = pl.pallas_call(
        kern,
        grid=(nb, GLA_HEADS),
        in_specs=in_specs,
        out_specs=out_specs,
        out_shape=out_shape,
        scratch_shapes=scratch,
        compiler_params=_cparams("arbitrary", "arbitrary"),
        name="gla",
    )(*args)
    return (outs[0], outs[1]) if ctx_out else (outs[0], None)


def _merge_kernel(p_ref, d_ref, g_ref, wp_ref, wd_ref, wg_ref, mp_ref, md_ref, mg_ref, y_ref):
    y = (_sigmoid(mp_ref[...].astype(F32)) * _dot(p_ref[...], wp_ref[...])
         + _sigmoid(md_ref[...].astype(F32)) * _dot(d_ref[...], wd_ref[...])
         + _sigmoid(mg_ref[...].astype(F32)) * _dot(g_ref[...], wg_ref[...]))
    y_ref[...] = y.astype(y_ref.dtype)


def _merge(pool_o, diff_o, gla_o, z2d, col_mg, wbp, wbd, wbg, l, tm):
    m, kw = pool_o.shape
    d = wbp.shape[2]
    act = pl.BlockSpec((tm, kw), lambda i: (i, 0))
    wsp = pl.BlockSpec((None, kw, d), lambda i: (l, 0, 0), pipeline_mode=pl.Buffered(1))
    gate = lambda k: pl.BlockSpec((tm, d), lambda i: (i, col_mg // d + k))
    return pl.pallas_call(
        _merge_kernel,
        grid=(m // tm,),
        in_specs=[act, act, act, wsp, wsp, wsp, gate(0), gate(1), gate(2)],
        out_specs=pl.BlockSpec((tm, d), lambda i: (i, 0)),
        out_shape=jax.ShapeDtypeStruct((m, d), BF16),
        compiler_params=_cparams("arbitrary"),
        name="merge",
    )(pool_o, diff_o, gla_o, wbp, wbd, wbg, z2d, z2d, z2d)


def _outproj_kernel(y_ref, w_ref, x_ref, mod_ref, o_ref):
    o_ref[...] = x_ref[...] + mod_ref[2:3, :] * _dot(y_ref[...], w_ref[...])


def _outproj(y, w_out, x2d, mod3, l, row_of_tile, tm):
    m, d = x2d.shape
    return pl.pallas_call(
        _outproj_kernel,
        grid=(m // tm,),
        in_specs=[pl.BlockSpec((tm, d), lambda i: (i, 0)),
                  pl.BlockSpec((None, d, d), lambda i: (l, 0, 0), pipeline_mode=pl.Buffered(1)),
                  pl.BlockSpec((tm, d), lambda i: (i, 0)),
                  pl.BlockSpec((None, 3, d), lambda i: (row_of_tile(i), 0, 0))],
        out_specs=pl.BlockSpec((tm, d), lambda i: (i, 0)),
        out_shape=jax.ShapeDtypeStruct((m, d), F32),
        compiler_params=_cparams("arbitrary"),
        name="outproj",
    )(y, w_out, x2d, mod3)


def _rope_tables(seq_len):
    n_freq = DIFF_HEAD_DIM // 4
    t = jnp.arange(seq_len)
    inv = ROPE_THETA ** (-jnp.arange(n_freq, dtype=F32) / n_freq)
    ang = jnp.concatenate([(t // GRID_W).astype(F32)[:, None] * inv,
                           (t % GRID_W).astype(F32)[:, None] * inv], axis=-1)
    cos, sin = jnp.cos(ang), jnp.sin(ang)
    return jnp.tile(cos, (1, 4)), jnp.tile(jnp.concatenate([-sin, sin], axis=-1), (1, 2))


def _largest_tile(n, cap, mult):
    t = min(n, cap)
    while n % t or t % mult:
        t -= mult
    return t


def kernel(x, c, ctx, c_ctx, norm_g, w_ada, b_ada, w_in, pool_w, pool_scale, diff_q_norm, diff_k_norm, diff_lam_q1, diff_lam_k1, diff_lam_q2, diff_lam_k2, diff_subln, gla_w_gate_f, gla_b_gate_f, gla_w_gate_b, gla_b_gate_b, gla_norm, w_branch_pool, w_branch_diff, w_branch_gla, w_out):
    nb, seq, d = x.shape
    lc = ctx.shape[1]
    depth = w_in.shape[0]
    pw = pool_scale.shape[1]
    dw = DIFF_HEADS * 2 * DIFF_HEAD_DIM
    gkw, gvw = GLA_HEADS * GLA_DK, GLA_HEADS * GLA_DV

    sizes = dict(pu=pw, pg=pw, dq=dw, dk=dw, dv=dw, dg=dw, gq=gkw, gk=gkw, gv=gvw, gg=gvw)
    wcol, off = {}, 0
    for name, size in sizes.items():
        wcol[name] = off
        off += size
    lr_col = off
    zcol = {name: 3 * d + o for name, o in wcol.items()}
    zcol["mg"] = 0

    rope_tabs = _rope_tables(seq)
    cc = jnp.zeros((8, d), F32).at[:nb].set(c).at[nb].set(c_ctx)
    x2d = x.reshape(nb * seq, d)
    ctx2d = ctx.reshape(nb * lc, d)

    lr_end = lr_col + 2 * GLA_RANK
    w_in_b = jnp.concatenate([w_in[:, :, lr_end:], w_in[:, :, :lr_col], w_in[:, :, lr_col:lr_col + LANES]],
                             axis=2).astype(BF16)
    lr_tile = 3 * d + lr_col
    wbp, wbd, wbg, wo = (w.astype(BF16) for w in (w_branch_pool, w_branch_diff, w_branch_gla, w_out))
    norm_g3 = norm_g.reshape(depth, 1, d)
    b_ada3 = b_ada.reshape(depth, 1, 3 * d)
    pool_scale3 = pool_scale.reshape(depth, 1, pw)
    gains3 = jnp.stack([jnp.tile(diff_q_norm, (1, 2)), jnp.tile(diff_k_norm, (1, 2)), diff_subln], axis=1)
    lam4 = jnp.stack([diff_lam_q1, diff_lam_k1, diff_lam_q2, diff_lam_k2], axis=1)
    gla_gain3 = gla_norm.reshape(depth, 1, GLA_DV)
    per_head = lambda w: w.reshape(depth, GLA_RANK, GLA_HEADS, GLA_DK).transpose(0, 2, 1, 3)
    w2 = jnp.zeros((depth, GLA_HEADS, LANES, 2 * GLA_DK), F32)
    w2 = w2.at[:, :, :GLA_RANK, :GLA_DK].set(per_head(gla_w_gate_f))
    w2 = w2.at[:, :, GLA_RANK:2 * GLA_RANK, GLA_DK:].set(per_head(gla_w_gate_b)).astype(BF16)
    b2 = jnp.concatenate([gla_b_gate_f.reshape(depth, GLA_HEADS, 1, GLA_DK),
                          gla_b_gate_b.reshape(depth, GLA_HEADS, 1, GLA_DK)], axis=-1)

    tm_x = _largest_tile(seq, 1024, 16)
    tm_c = _largest_tile(nb * lc, 1024, 16)
    tn_in = _largest_tile(math.gcd(lr_col, 3 * d), 1536, 256)
    row_x = lambda tm: (lambda i: (i * tm) // seq)
    row_c = lambda i: nb

    for l in range(depth):
        last = l == depth - 1
        lambda_init = 0.8 - 0.6 * math.exp(-0.3 * l)
        mod3 = _ada(cc, w_ada, b_ada3, l).reshape(8, 3, d)
        every = lambda j: j
        z, lr = _inproj(x2d, mod3, norm_g3, l, w_in_b, lr_tile, row_x(tm_x), tm_x, tn_in, every, lr_tile // tn_in)
        if last:
            tn_c = math.gcd(math.gcd(zcol["dk"], zcol["dg"]), math.gcd(zcol["gk"], zcol["gg"]))
            n_kv = (zcol["dg"] - zcol["dk"]) // tn_c
            n_all = n_kv + (zcol["gg"] - zcol["gk"]) // tn_c
            first_kv, skip = zcol["dk"] // tn_c, (zcol["gk"] - zcol["dg"]) // tn_c
            kv_only = lambda j: jnp.where(j < n_kv, first_kv + j, first_kv + skip + j)
            zc, lrc = _inproj(ctx2d, mod3, norm_g3, l, w_in_b, lr_tile, row_c, tm_c, tn_c, kv_only, n_all)
            ccol = dict(dk=0, dv=dw, gk=2 * dw, gv=2 * dw + gkw, gq=2 * dw, gg=None)
        else:
            zc, lrc = _inproj(ctx2d, mod3, norm_g3, l, w_in_b, lr_tile, row_c, tm_c, tn_in, every, lr_tile // tn_in)
            ccol = zcol
        z3, zc3 = z.reshape(nb, seq, -1), zc.reshape(nb, lc, -1)
        lr3, lrc3 = lr.reshape(nb, seq, LANES), lrc.reshape(nb, lc, LANES)

        pool_l = _pool(z3, zcol["pu"], zcol["pg"], pool_w, pool_scale3, l, _largest_tile(seq, 512, 16))
        diff_l = _attn(z3, zcol["dq"], zcol["dg"], zc3, ccol["dk"], ccol["dv"], z3, zcol["dk"], zcol["dv"],
                       rope_tabs, gains3, lam4, l, lambda_init, _largest_tile(seq, 512, 16))
        gla_l, gla_c = _gla(zc3, (ccol["gq"], ccol["gk"], ccol["gv"], ccol["gg"]), lrc3,
                            z3, (zcol["gq"], zcol["gk"], zcol["gv"], zcol["gg"]), lr3,
                            w2, b2, gla_gain3, l, ctx_out=not last)

        tm_m = _largest_tile(seq, 512, 16)
        y = _merge(pool_l.reshape(nb * seq, pw), diff_l.reshape(nb * seq, dw), gla_l.reshape(nb * seq, gvw),
                   z, zcol["mg"], wbp, wbd, wbg, l, tm_m)
        x2d_new = _outproj(y, wo, x2d, mod3, l, row_x(tm_m), tm_m)

        if not last:
            pool_c = _pool(zc3, zcol["pu"], zcol["pg"], pool_w, pool_scale3, l, _largest_tile(lc, 512, 16))
            diff_c = _attn(zc3, zcol["dq"], zcol["dg"], zc3, zcol["dk"], zcol["dv"], None, None, None,
                           None, gains3, lam4, l, lambda_init, _largest_tile(lc, 256, 16))
            tm_mc = _largest_tile(nb * lc, 512, 16)
            y_c = _merge(pool_c.reshape(nb * lc, pw), diff_c.reshape(nb * lc, dw), gla_c.reshape(nb * lc, gvw),
                         zc, zcol["mg"], wbp, wbd, wbg, l, tm_mc)
            ctx2d = _outproj(y_c, wo, ctx2d, mod3, l, row_c, tm_mc)
        x2d = x2d_new

    return x2d.reshape(nb, seq, d)
```

```python
import functools
import math

import jax
import jax.numpy as jnp
import numpy as np
from jax import lax
from jax.experimental import pallas as pl
from jax.experimental.pallas import tpu as pltpu

F32 = jnp.float32
BF16 = jnp.bfloat16

EPS = 1e-6
GRID_W = 64
ROPE_THETA = 10000.0

POOL_WINDOWS = (2, 4, 8, 16)
POOL_HALO = 16
DIFF_HEADS = 8
DIFF_HEAD_DIM = 64
GLA_HEADS = 4
GLA_DK = 128
GLA_DV = 256
GLA_RANK = 16
GLA_NORMALIZER = 16.0
GLA_CHUNK = 256
LANES = 128
LOG2E = math.log2(math.e)

VMEM_LIMIT = 48 * 1024 * 1024


def _cparams(*sem):
    return pltpu.CompilerParams(dimension_semantics=sem, vmem_limit_bytes=VMEM_LIMIT)


def _sigmoid(x):
    return 1.0 / (1.0 + jnp.exp(-x))


def _silu(x):
    return x * _sigmoid(x)


def _dot(a, b):
    return jnp.dot(a, b, preferred_element_type=F32)


def _dot_nt(a, b):
    return lax.dot_general(a, b, (((1,), (1,)), ((), ())), preferred_element_type=F32)


def _dot_tn(a, b):
    return lax.dot_general(a, b, (((0,), (0,)), ((), ())), preferred_element_type=F32)


def _ada_kernel(cc_ref, w_ref, b_ref, o_ref):
    @pl.when(pl.program_id(0) == 0)
    def _():
        o_ref[...] = jnp.broadcast_to(b_ref[...], o_ref.shape)

    a = _silu(cc_ref[...]).astype(BF16)
    o_ref[...] += _dot(a, w_ref[...].astype(BF16))


def _ada(cc, w_ada, b_ada3, l, tk=256):
    rows, d = cc.shape
    n = w_ada.shape[2]
    return pl.pallas_call(
        _ada_kernel,
        grid=(d // tk,),
        in_specs=[pl.BlockSpec((rows, tk), lambda k: (0, k)),
                  pl.BlockSpec((None, tk, n), lambda k: (l, k, 0)),
                  pl.BlockSpec((None, 1, n), lambda k: (l, 0, 0))],
        out_specs=pl.BlockSpec((rows, n), lambda k: (0, 0)),
        out_shape=jax.ShapeDtypeStruct((rows, n), F32),
        compiler_params=_cparams("arbitrary"),
        name="ada",
    )(cc, w_ada, b_ada3)


def _inproj_kernel(x_ref, mod_ref, g_ref, w_ref, wlr_ref, z_ref, lr_ref, h_ref):
    @pl.when(pl.program_id(1) == 0)
    def _():
        x = x_ref[...]
        ms = jnp.mean(x * x, axis=-1, keepdims=True)
        y = x * lax.rsqrt(ms + EPS) * g_ref[...]
        h = (y * (1.0 + mod_ref[1:2, :]) + mod_ref[0:1, :]).astype(BF16)
        h_ref[...] = h
        lr_ref[...] = _dot(h, wlr_ref[...]).astype(lr_ref.dtype)

    z_ref[...] = _dot(h_ref[...], w_ref[...]).astype(z_ref.dtype)


def _inproj(x2d, mod3, norm_g3, l, w_all, lr_col, row_of_tile, tm, tn, tile_of, n_tiles):
    m, d = x2d.shape
    return pl.pallas_call(
        _inproj_kernel,
        grid=(m // tm, n_tiles),
        in_specs=[pl.BlockSpec((tm, d), lambda i, j: (i, 0)),
                  pl.BlockSpec((None, 3, d), lambda i, j: (row_of_tile(i), 0, 0)),
                  pl.BlockSpec((None, 1, d), lambda i, j: (l, 0, 0)),
                  pl.BlockSpec((None, d, tn), lambda i, j: (l, 0, tile_of(j))),
                  pl.BlockSpec((None, d, LANES), lambda i, j: (l, 0, lr_col // LANES))],
        out_specs=[pl.BlockSpec((tm, tn), lambda i, j: (i, j)),
                   pl.BlockSpec((tm, LANES), lambda i, j: (i, 0))],
        out_shape=[jax.ShapeDtypeStruct((m, n_tiles * tn), BF16),
                   jax.ShapeDtypeStruct((m, LANES), BF16)],
        scratch_shapes=[pltpu.VMEM((tm, d), BF16)],
        compiler_params=_cparams("arbitrary", "arbitrary"),
        name="inproj",
    )(x2d, mod3, norm_g3, w_all, w_all)


def _pool_kernel(up_ref, uc_ref, un_ref, pg_ref, w_ref, sc_ref, o_ref, *, tile, seq_len):
    r = pl.program_id(1)
    half = jnp.left_shift(1, pl.program_id(2))
    base = r * tile
    u_all = jnp.concatenate([up_ref[...], uc_ref[...], un_ref[...]], axis=0)
    width = tile + 2 * POOL_HALO
    t = base + lax.broadcasted_iota(jnp.int32, (tile, width), 0)
    s = base - POOL_HALO + lax.broadcasted_iota(jnp.int32, (tile, width), 1)
    inside = (s >= jnp.maximum(t - half, 0)) & (s < jnp.minimum(t + half, seq_len))
    band = jnp.where(inside, 1.0, 0.0).astype(BF16)
    wsum = _dot(band, u_all)
    tc = base + lax.broadcasted_iota(jnp.int32, (tile, 1), 0)
    cnt = (jnp.minimum(tc + half, seq_len) - jnp.maximum(tc - half, 0)).astype(F32)
    dcen = wsum / cnt - uc_ref[...].astype(F32)
    y = _dot(dcen.astype(BF16), w_ref[...].astype(BF16)) * sc_ref[...]
    o_ref[...] = (y * _silu(pg_ref[...].astype(F32))).astype(o_ref.dtype)


def _pool(z3, col_u, col_g, pool_w, pool_scale3, l, tile):
    nseq, seq_len, _ = z3.shape
    ngrp = len(POOL_WINDOWS)
    gw = pool_w.shape[-1]
    cu, cg = col_u // gw, col_g // gw
    nhalo = seq_len // POOL_HALO
    per = tile // POOL_HALO
    kern = functools.partial(_pool_kernel, tile=tile, seq_len=seq_len)
    return pl.pallas_call(
        kern,
        grid=(nseq, seq_len // tile, ngrp),
        in_specs=[
            pl.BlockSpec((None, POOL_HALO, gw), lambda s, r, g: (s, jnp.maximum(r * per - 1, 0), cu + g)),
            pl.BlockSpec((None, tile, gw), lambda s, r, g: (s, r, cu + g)),
            pl.BlockSpec((None, POOL_HALO, gw), lambda s, r, g: (s, jnp.minimum((r + 1) * per, nhalo - 1), cu + g)),
            pl.BlockSpec((None, tile, gw), lambda s, r, g: (s, r, cg + g)),
            pl.BlockSpec((None, None, gw, gw), lambda s, r, g: (l, g, 0, 0)),
            pl.BlockSpec((None, 1, gw), lambda s, r, g: (l, 0, g)),
        ],
        out_specs=pl.BlockSpec((None, tile, gw), lambda s, r, g: (s, r, g)),
        out_shape=jax.ShapeDtypeStruct((nseq, seq_len, ngrp * gw), BF16),
        compiler_params=_cparams("arbitrary", "arbitrary", "arbitrary"),
        name="pool",
    )(z3, z3, z3, z3, pool_w, pool_scale3)


def _head_norm(t, gain, gmat):
    sq = t * t
    hi = sq.astype(BF16)
    lo = (sq - hi.astype(F32)).astype(BF16)
    ssq = _dot(hi, gmat) + _dot(lo, gmat)
    return t * lax.rsqrt(ssq * (1.0 / DIFF_HEAD_DIM) + EPS) * gain


def _rope(t, cos_t, sin_s, first_half):
    half = DIFF_HEAD_DIM // 2
    lower = pltpu.roll(t, half, 1)
    upper = pltpu.roll(t, LANES - half, 1)
    return t * cos_t + jnp.where(first_half, upper, lower) * sin_s


def _attn_kernel(*refs, tq, row_group, rope_q, has_latent, lambda_init):
    if has_latent:
        (q_ref, dg_ref, kc_ref, vc_ref, kl_ref, vl_ref, cosq_ref, sinq_ref, cosk_ref, sink_ref,
         gains_ref, lam_ref, o_ref, kcs_ref, vcs_ref, kls_ref, vls_ref) = refs
    else:
        (q_ref, dg_ref, kc_ref, vc_ref, gains_ref, lam_ref, o_ref, kcs_ref, vcs_ref) = refs
    hw = 2 * DIFF_HEAD_DIM
    q_gain, k_gain, sub_gain = gains_ref[0:1, :], gains_ref[1:2, :], gains_ref[2:3, :]

    lane = lax.broadcasted_iota(jnp.int32, (1, LANES), 1)
    first_half = (lane & (DIFF_HEAD_DIM - 1)) < (DIFF_HEAD_DIM // 2)
    sub1 = lane < DIFF_HEAD_DIM
    gi = jnp.where(lax.broadcasted_iota(jnp.int32, (LANES, LANES), 0) < DIFF_HEAD_DIM, 1.0, 0.0)
    gj = jnp.where(lax.broadcasted_iota(jnp.int32, (LANES, LANES), 1) < DIFF_HEAD_DIM, 1.0, 0.0)
    gmat = (gi * gj + (1.0 - gi) * (1.0 - gj)).astype(BF16)

    @pl.when(pl.program_id(2) == 0)
    def _():
        kcs_ref[...] = _head_norm(kc_ref[...].astype(F32), k_gain, gmat).astype(BF16)
        vcs_ref[:, :hw] = vc_ref[...]
        vcs_ref[:, hw:] = jnp.ones((vc_ref.shape[0], hw), BF16)
        if has_latent:
            kl = _head_norm(kl_ref[...].astype(F32), k_gain, gmat)
            kls_ref[...] = _rope(kl, cosk_ref[...], sink_ref[...], first_half).astype(BF16)
            vls_ref[:, :hw] = vl_ref[...]
            vls_ref[:, hw:] = jnp.ones((vl_ref.shape[0], hw), BF16)

    q = _head_norm(q_ref[...].astype(F32), q_gain, gmat)
    if rope_q:
        q = _rope(q, cosq_ref[...], sinq_ref[...], first_half)
    q = q * (DIFF_HEAD_DIM ** -0.5 * LOG2E)
    qq = jnp.concatenate([jnp.where(sub1, q, 0.0), jnp.where(sub1, 0.0, q)], axis=0).astype(BF16)

    segs = [(kcs_ref, vcs_ref)] + ([(kls_ref, vls_ref)] if has_latent else [])
    pvs = []
    for r0 in range(0, 2 * tq, row_group):
        qg = qq[r0:r0 + row_group]
        scores = [_dot_nt(qg, k_ref[...]) for k_ref, _ in segs]
        mx = functools.reduce(jnp.maximum, [jnp.max(s, axis=-1, keepdims=True) for s in scores])
        acc = None
        for s, (_, v_ref) in zip(scores, segs):
            part = _dot(jnp.exp2(s - mx).astype(BF16), v_ref[...])
            acc = part if acc is None else acc + part
        pvs.append(acc[:, :hw] / acc[:, hw:hw + 1])
    pv = jnp.concatenate(pvs, axis=0)

    lam_p = lam_ref[...]
    lam = (jnp.exp(jnp.sum(lam_p[0:1] * lam_p[1:2], axis=-1, keepdims=True))
           - jnp.exp(jnp.sum(lam_p[2:3] * lam_p[3:4], axis=-1, keepdims=True)) + lambda_init)
    o = pv[:tq] - lam * pv[tq:]

    ms = jnp.mean(o * o, axis=-1, keepdims=True)
    y = o * lax.rsqrt(ms + EPS) * sub_gain * (1.0 - lambda_init)
    o_ref[...] = (y * _silu(dg_ref[...].astype(F32))).astype(o_ref.dtype)


def _attn(zq3, cq, cdg, zc3, ckc, cvc, zl3, ckl, cvl, rope_tabs, gains3, lam4, l, lambda_init, tq):
    nb, lq, _ = zq3.shape
    lc = zc3.shape[1]
    has_latent = zl3 is not None
    hw = 2 * DIFF_HEAD_DIM
    b_of = lambda c: c // hw
    qmap = lambda off: (lambda b, h, i: (b, i, b_of(off) + h))
    kmap = lambda off: (lambda b, h, i: (b, 0, b_of(off) + h))
    const = lambda b, h, i: (0, 0)
    in_specs = [pl.BlockSpec((None, tq, hw), qmap(cq)),
                pl.BlockSpec((None, tq, hw), qmap(cdg)),
                pl.BlockSpec((None, lc, hw), kmap(ckc)),
                pl.BlockSpec((None, lc, hw), kmap(cvc))]
    args = [zq3, zq3, zc3, zc3]
    scratch = [pltpu.VMEM((lc, hw), BF16), pltpu.VMEM((lc, 2 * hw), BF16)]
    if has_latent:
        ll = zl3.shape[1]
        cos_t, sin_s = rope_tabs
        in_specs += [pl.BlockSpec((None, ll, hw), kmap(ckl)),
                     pl.BlockSpec((None, ll, hw), kmap(cvl)),
                     pl.BlockSpec((tq, hw), lambda b, h, i: (i, 0)),
                     pl.BlockSpec((tq, hw), lambda b, h, i: (i, 0)),
                     pl.BlockSpec((ll, hw), const),
                     pl.BlockSpec((ll, hw), const)]
        args += [zl3, zl3, cos_t, sin_s, cos_t, sin_s]
        scratch += [pltpu.VMEM((ll, hw), BF16), pltpu.VMEM((ll, 2 * hw), BF16)]
    in_specs += [pl.BlockSpec((None, 3, hw), lambda b, h, i: (l, 0, 0)),
                 pl.BlockSpec((None, 4, DIFF_HEAD_DIM), lambda b, h, i: (l, 0, 0))]
    args += [gains3, lam4]
    kern = functools.partial(_attn_kernel, tq=tq, row_group=min(2 * tq, LANES), rope_q=has_latent, has_latent=has_latent,
                             lambda_init=lambda_init)
    return pl.pallas_call(
        kern,
        grid=(nb, DIFF_HEADS, lq // tq),
        in_specs=in_specs,
        out_specs=pl.BlockSpec((None, tq, hw), lambda b, h, i: (b, i, h)),
        out_shape=jax.ShapeDtypeStruct((nb, lq, DIFF_HEADS * hw), BF16),
        scratch_shapes=scratch,
        compiler_params=_cparams("arbitrary", "arbitrary", "arbitrary"),
        name="diff_attn",
    )(*args)


def _log_sigmoid(x):
    return jnp.minimum(x, 0.0) - jnp.log1p(jnp.exp(-jnp.abs(x)))


GLA_LEVELS = tuple(1 << i for i in range(GLA_CHUNK.bit_length() - 1))
GLA_TAB_LEVELS = tuple(m for m in GLA_LEVELS if m < 8)
GLA_ROW_LEVELS = tuple(m for m in GLA_LEVELS if m >= 8)
GLA_SLOT_CUM = len(GLA_TAB_LEVELS)
GLA_UNROLL = 2
GLA_TOT_ROWS = 16


def _gla_boundary(m, rev):
    return m if rev else m - 1


def _gla_tables():
    c = GLA_CHUNK
    t = np.arange(c)[:, None]
    j = np.arange(c)[None, :]
    sums, masks = [], []
    for rev in (False, True):
        rows, mk = [], []
        for m in GLA_LEVELS:
            base = t & ~(2 * m - 1)
            bd = base + _gla_boundary(m, rev)
            is_q = ((t & m) == 0) if rev else ((t & m) != 0)
            if rev:
                between = np.where(is_q, (j >= t) & (j < bd), (j >= bd) & (j < t))
            else:
                between = np.where(is_q, (j > bd) & (j <= t), (j > t) & (j <= bd))
            if m in GLA_TAB_LEVELS:
                rows.append(between)
            mk.append((base == (j & ~(2 * m - 1))) & is_q & (((j & m) != 0) if rev else ((j & m) == 0)))
        rows.append(j >= t if rev else j <= t)
        rows.append(np.ones((GLA_TOT_ROWS, c), bool))
        mk.append(t == j)
        sums.append(np.concatenate(rows, axis=0))
        masks.append(np.stack(mk))
    return (jnp.asarray(np.stack(sums), BF16), jnp.asarray(np.stack(masks), F32))


def _gla_chunk(q, k, v, ghl, st, tab_ref, mask_ref, rev, need_out):
    c = GLA_CHUNK
    first = 0 if need_out else GLA_SLOT_CUM
    x2 = _dot(tab_ref[first * c:, :], ghl)
    x = x2[:, :GLA_DK] + x2[:, GLA_DK:]
    cum = x[(GLA_SLOT_CUM - first) * c:(GLA_SLOT_CUM - first + 1) * c]
    tot = x[-GLA_TOT_ROWS:1 - GLA_TOT_ROWS]
    kd = (k * jnp.exp2(tot - cum)).astype(BF16)
    st_new = st * jnp.exp2(tot) + _dot_tn(v, kd)
    if not need_out:
        return None, st_new

    out = _dot_nt((q * jnp.exp2(cum)).astype(BF16), st.astype(BF16))
    att = _dot_nt(q.astype(BF16), k.astype(BF16)) * mask_ref[len(GLA_LEVELS)]
    for lvl, m in enumerate(GLA_LEVELS):
        if m in GLA_TAB_LEVELS:
            xl = x[lvl * c:(lvl + 1) * c]
        else:
            anchor = jnp.concatenate(
                [jnp.broadcast_to(cum[b0 + _gla_boundary(m, rev):b0 + _gla_boundary(m, rev) + 1], (2 * m, GLA_DK))
                 for b0 in range(0, c, 2 * m)], axis=0)
            xl = -jnp.abs(cum - anchor)
        el = jnp.exp2(xl)
        att = att + _dot_nt((q * el).astype(BF16), (k * el).astype(BF16)) * mask_ref[lvl]
    return out + _dot(att.astype(BF16), v), st_new


def _gla_kernel(*refs, n_ctx, n_lat, ctx_out):
    (qc_ref, kc_ref, vc_ref, ggc_ref, lrc_ref, ql_ref, kl_ref, vl_ref, ggl_ref, lrl_ref,
     w2_ref, b2_ref, gain_ref, tab_ref, mask_ref) = refs[:15]
    rest = refs[15:]
    if ctx_out:
        ol_ref, oc_ref, gl_ref, gc_ref, accl_ref, accc_ref, stf_ref, stb_ref = rest
    else:
        ol_ref, gl_ref, gc_ref, accl_ref, stf_ref, stb_ref = rest
        oc_ref = accc_ref = None

    def decays(lr_ref, g_ref):
        g = _log_sigmoid(_dot(lr_ref[...], w2_ref[...]) + b2_ref[...]) * (LOG2E / GLA_NORMALIZER)
        hi = g.astype(BF16)
        lo = (g - hi.astype(F32)).astype(BF16)
        for d in range(2):
            g_ref[:, (2 * d) * GLA_DK:(2 * d + 1) * GLA_DK] = hi[:, d * GLA_DK:(d + 1) * GLA_DK]
            g_ref[:, (2 * d + 1) * GLA_DK:(2 * d + 2) * GLA_DK] = lo[:, d * GLA_DK:(d + 1) * GLA_DK]

    decays(lrl_ref, gl_ref)
    decays(lrc_ref, gc_ref)
    stf_ref[...] = jnp.zeros_like(stf_ref)
    stb_ref[...] = jnp.zeros_like(stb_ref)
    scale = GLA_DK ** -0.5

    def scan(n, q_ref, k_ref, v_ref, g_ref, acc_ref):
        need_out = acc_ref is not None

        def step(i, assign_up, assign_down):
            for c_idx, d, st_ref, first_visit in ((i, 0, stf_ref, assign_up), (n - 1 - i, 1, stb_ref, assign_down)):
                rows = pl.ds(pl.multiple_of(c_idx * GLA_CHUNK, GLA_CHUNK), GLA_CHUNK)
                q = q_ref[rows, :].astype(F32) * scale
                k = k_ref[rows, :].astype(F32)
                ghl = g_ref[rows, 2 * d * GLA_DK:(2 * d + 2) * GLA_DK]
                out, st_new = _gla_chunk(q, k, v_ref[rows, :], ghl, st_ref[...], tab_ref.at[d], mask_ref.at[d],
                                         d == 1, need_out)
                st_ref[...] = st_new
                if need_out:
                    if first_visit:
                        acc_ref[rows, :] = out
                    else:
                        acc_ref[rows, :] += out

        def first(i, carry):
            step(i, True, True)
            return carry

        def second(i, carry):
            step(i, False, False)
            return carry

        if n // 2:
            lax.fori_loop(0, n // 2, first, 0, unroll=min(GLA_UNROLL, n // 2))
        if n % 2:
            step(n // 2, True, False)
        if n // 2:
            lax.fori_loop((n + 1) // 2, n, second, 0, unroll=min(GLA_UNROLL, n // 2))

    def finish(acc_ref, gg_ref, o_ref):
        o = acc_ref[...]
        ms = jnp.mean(o * o, axis=-1, keepdims=True)
        y = o * lax.rsqrt(ms + EPS) * gain_ref[...]
        o_ref[...] = (y * _silu(gg_ref[...].astype(F32))).astype(o_ref.dtype)

    scan(n_ctx, qc_ref, kc_ref, vc_ref, gc_ref, accc_ref)
    scan(n_lat, ql_ref, kl_ref, vl_ref, gl_ref, accl_ref)
    finish(accl_ref, ggl_ref, ol_ref)
    if ctx_out:
        finish(accc_ref, ggc_ref, oc_ref)


def _gla(zc3, cc, lrc3, zl3, cl, lrl3, w2, b2, gain3, l, ctx_out):
    nb, lc, _ = zc3.shape
    ll = zl3.shape[1]
    assert lc % GLA_CHUNK == 0 and ll % GLA_CHUNK == 0
    tabs, masks = _gla_tables()

    def seq_specs(z3, cols, lr3, slen):
        qo, ko, vo, go = cols
        return ([pl.BlockSpec((None, slen, GLA_DK), lambda b, h: (b, 0, qo // GLA_DK + h)),
                 pl.BlockSpec((None, slen, GLA_DK), lambda b, h: (b, 0, ko // GLA_DK + h)),
                 pl.BlockSpec((None, slen, GLA_DV), lambda b, h: (b, 0, vo // GLA_DV + h)),
                 pl.BlockSpec((None, slen, GLA_DV), lambda b, h: (b, 0, (0 if go is None else go) // GLA_DV + h)),
                 pl.BlockSpec((None, slen, LANES), lambda b, h: (b, 0, 0))],
                [z3, z3, z3, z3, lr3])

    sc, ac = seq_specs(zc3, cc, lrc3, lc)
    sl, al = seq_specs(zl3, cl, lrl3, ll)
    in_specs = sc + sl + [pl.BlockSpec((None, None, LANES, 2 * GLA_DK), lambda b, h: (l, h, 0, 0)),
                          pl.BlockSpec((None, None, 1, 2 * GLA_DK), lambda b, h: (l, h, 0, 0)),
                          pl.BlockSpec((None, 1, GLA_DV), lambda b, h: (l, 0, 0)),
                          pl.BlockSpec(tabs.shape, lambda b, h: (0, 0, 0)),
                          pl.BlockSpec(masks.shape, lambda b, h: (0, 0, 0, 0))]
    args = ac + al + [w2, b2, gain3, tabs, masks]
    out_specs = [pl.BlockSpec((None, ll, GLA_DV), lambda b, h: (b, 0, h))]
    out_shape = [jax.ShapeDtypeStruct((nb, ll, GLA_HEADS * GLA_DV), BF16)]
    scratch = [pltpu.VMEM((ll, 4 * GLA_DK), BF16), pltpu.VMEM((lc, 4 * GLA_DK), BF16),
               pltpu.VMEM((ll, GLA_DV), F32)]
    if ctx_out:
        out_specs.append(pl.BlockSpec((None, lc, GLA_DV), lambda b, h: (b, 0, h)))
        out_shape.append(jax.ShapeDtypeStruct((nb, lc, GLA_HEADS * GLA_DV), BF16))
        scratch.append(pltpu.VMEM((lc, GLA_DV), F32))
    scratch += [pltpu.VMEM((GLA_DV, GLA_DK), F32), pltpu.VMEM((GLA_DV, GLA_DK), F32)]
    kern = functools.partial(_gla_kernel, n_ctx=lc // GLA_CHUNK, n_lat=ll // GLA_CHUNK, ctx_out=ctx_out)
    outs = pl.pallas_call(
        kern,
        grid=(nb, GLA_HEADS),
        in_specs=in_specs,
        out_specs=out_specs,
        out_shape=out_shape,
        scratch_shapes=scratch,
        compiler_params=_cparams("arbitrary", "arbitrary"),
        name="gla",
    )(*args)
    return (outs[0], outs[1]) if ctx_out else (outs[0], None)


def _merge_kernel(p_ref, d_ref, g_ref, wp_ref, wd_ref, wg_ref, mp_ref, md_ref, mg_ref, y_ref):
    y = (_sigmoid(mp_ref[...].astype(F32)) * _dot(p_ref[...], wp_ref[...])
         + _sigmoid(md_ref[...].astype(F32)) * _dot(d_ref[...], wd_ref[...])
         + _sigmoid(mg_ref[...].astype(F32)) * _dot(g_ref[...], wg_ref[...]))
    y_ref[...] = y.astype(y_ref.dtype)


def _merge(pool_o, diff_o, gla_o, z2d, col_mg, wbp, wbd, wbg, l, tm):
    m, kw = pool_o.shape
    d = wbp.shape[2]
    act = pl.BlockSpec((tm, kw), lambda i: (i, 0))
    wsp = pl.BlockSpec((None, kw, d), lambda i: (l, 0, 0), pipeline_mode=pl.Buffered(1))
    gate = lambda k: pl.BlockSpec((tm, d), lambda i: (i, col_mg // d + k))
    return pl.pallas_call(
        _merge_kernel,
        grid=(m // tm,),
        in_specs=[act, act, act, wsp, wsp, wsp, gate(0), gate(1), gate(2)],
        out_specs=pl.BlockSpec((tm, d), lambda i: (i, 0)),
        out_shape=jax.ShapeDtypeStruct((m, d), BF16),
        compiler_params=_cparams("arbitrary"),
        name="merge",
    )(pool_o, diff_o, gla_o, wbp, wbd, wbg, z2d, z2d, z2d)


def _outproj_kernel(y_ref, w_ref, x_ref, mod_ref, o_ref):
    o_ref[...] = x_ref[...] + mod_ref[2:3, :] * _dot(y_ref[...], w_ref[...])


def _outproj(y, w_out, x2d, mod3, l, row_of_tile, tm):
    m, d = x2d.shape
    return pl.pallas_call(
        _outproj_kernel,
        grid=(m // tm,),
        in_specs=[pl.BlockSpec((tm, d), lambda i: (i, 0)),
                  pl.BlockSpec((None, d, d), lambda i: (l, 0, 0), pipeline_mode=pl.Buffered(1)),
                  pl.BlockSpec((tm, d), lambda i: (i, 0)),
                  pl.BlockSpec((None, 3, d), lambda i: (row_of_tile(i), 0, 0))],
        out_specs=pl.BlockSpec((tm, d), lambda i: (i, 0)),
        out_shape=jax.ShapeDtypeStruct((m, d), F32),
        compiler_params=_cparams("arbitrary"),
        name="outproj",
    )(y, w_out, x2d, mod3)


def _rope_tables(seq_len):
    n_freq = DIFF_HEAD_DIM // 4
    t = jnp.arange(seq_len)
    inv = ROPE_THETA ** (-jnp.arange(n_freq, dtype=F32) / n_freq)
    ang = jnp.concatenate([(t // GRID_W).astype(F32)[:, None] * inv,
                           (t % GRID_W).astype(F32)[:, None] * inv], axis=-1)
    cos, sin = jnp.cos(ang), jnp.sin(ang)
    return jnp.tile(cos, (1, 4)), jnp.tile(jnp.concatenate([-sin, sin], axis=-1), (1, 2))


def _largest_tile(n, cap, mult):
    t = min(n, cap)
    while n % t or t % mult:
        t -= mult
    return t


def kernel(x, c, ctx, c_ctx, norm_g, w_ada, b_ada, w_in, pool_w, pool_scale, diff_q_norm, diff_k_norm, diff_lam_q1, diff_lam_k1, diff_lam_q2, diff_lam_k2, diff_subln, gla_w_gate_f, gla_b_gate_f, gla_w_gate_b, gla_b_gate_b, gla_norm, w_branch_pool, w_branch_diff, w_branch_gla, w_out):
    nb, seq, d = x.shape
    lc = ctx.shape[1]
    depth = w_in.shape[0]
    pw = pool_scale.shape[1]
    dw = DIFF_HEADS * 2 * DIFF_HEAD_DIM
    gkw, gvw = GLA_HEADS * GLA_DK, GLA_HEADS * GLA_DV

    sizes = dict(pu=pw, pg=pw, dq=dw, dk=dw, dv=dw, dg=dw, gq=gkw, gk=gkw, gv=gvw, gg=gvw)
    wcol, off = {}, 0
    for name, size in sizes.items():
        wcol[name] = off
        off += size
    lr_col = off
    zcol = {name: 3 * d + o for name, o in wcol.items()}
    zcol["mg"] = 0

    rope_tabs = _rope_tables(seq)
    cc = jnp.zeros((8, d), F32).at[:nb].set(c).at[nb].set(c_ctx)
    x2d = x.reshape(nb * seq, d)
    ctx2d = ctx.reshape(nb * lc, d)

    lr_end = lr_col + 2 * GLA_RANK
    w_in_b = jnp.concatenate([w_in[:, :, lr_end:], w_in[:, :, :lr_col], w_in[:, :, lr_col:lr_col + LANES]],
                             axis=2).astype(BF16)
    lr_tile = 3 * d + lr_col
    wbp, wbd, wbg, wo = (w.astype(BF16) for w in (w_branch_pool, w_branch_diff, w_branch_gla, w_out))
    norm_g3 = norm_g.reshape(depth, 1, d)
    b_ada3 = b_ada.reshape(depth, 1, 3 * d)
    pool_scale3 = pool_scale.reshape(depth, 1, pw)
    gains3 = jnp.stack([jnp.tile(diff_q_norm, (1, 2)), jnp.tile(diff_k_norm, (1, 2)), diff_subln], axis=1)
    lam4 = jnp.stack([diff_lam_q1, diff_lam_k1, diff_lam_q2, diff_lam_k2], axis=1)
    gla_gain3 = gla_norm.reshape(depth, 1, GLA_DV)
    per_head = lambda w: w.reshape(depth, GLA_RANK, GLA_HEADS, GLA_DK).transpose(0, 2, 1, 3)
    w2 = jnp.zeros((depth, GLA_HEADS, LANES, 2 * GLA_DK), F32)
    w2 = w2.at[:, :, :GLA_RANK, :GLA_DK].set(per_head(gla_w_gate_f))
    w2 = w2.at[:, :, GLA_RANK:2 * GLA_RANK, GLA_DK:].set(per_head(gla_w_gate_b)).astype(BF16)
    b2 = jnp.concatenate([gla_b_gate_f.reshape(depth, GLA_HEADS, 1, GLA_DK),
                          gla_b_gate_b.reshape(depth, GLA_HEADS, 1, GLA_DK)], axis=-1)

    tm_x = _largest_tile(seq, 1024, 16)
    tm_c = _largest_tile(nb * lc, 1024, 16)
    tn_in = _largest_tile(math.gcd(lr_col, 3 * d), 1536, 256)
    row_x = lambda tm: (lambda i: (i * tm) // seq)
    row_c = lambda i: nb

    for l in range(depth):
        last = l == depth - 1
        lambda_init = 0.8 - 0.6 * math.exp(-0.3 * l)
        mod3 = _ada(cc, w_ada, b_ada3, l).reshape(8, 3, d)
        every = lambda j: j
        z, lr = _inproj(x2d, mod3, norm_g3, l, w_in_b, lr_tile, row_x(tm_x), tm_x, tn_in, every, lr_tile // tn_in)
        if last:
            tn_c = math.gcd(math.gcd(zcol["dk"], zcol["dg"]), math.gcd(zcol["gk"], zcol["gg"]))
            n_kv = (zcol["dg"] - zcol["dk"]) // tn_c
            n_all = n_kv + (zcol["gg"] - zcol["gk"]) // tn_c
            first_kv, skip = zcol["dk"] // tn_c, (zcol["gk"] - zcol["dg"]) // tn_c
            kv_only = lambda j: jnp.where(j < n_kv, first_kv + j, first_kv + skip + j)
            zc, lrc = _inproj(ctx2d, mod3, norm_g3, l, w_in_b, lr_tile, row_c, tm_c, tn_c, kv_only, n_all)
            ccol = dict(dk=0, dv=dw, gk=2 * dw, gv=2 * dw + gkw, gq=2 * dw, gg=None)
        else:
            zc, lrc = _inproj(ctx2d, mod3, norm_g3, l, w_in_b, lr_tile, row_c, tm_c, tn_in, every, lr_tile // tn_in)
            ccol = zcol
        z3, zc3 = z.reshape(nb, seq, -1), zc.reshape(nb, lc, -1)
        lr3, lrc3 = lr.reshape(nb, seq, LANES), lrc.reshape(nb, lc, LANES)

        pool_l = _pool(z3, zcol["pu"], zcol["pg"], pool_w, pool_scale3, l, _largest_tile(seq, 512, 16))
        diff_l = _attn(z3, zcol["dq"], zcol["dg"], zc3, ccol["dk"], ccol["dv"], z3, zcol["dk"], zcol["dv"],
                       rope_tabs, gains3, lam4, l, lambda_init, _largest_tile(seq, 1024, 16))
        gla_l, gla_c = _gla(zc3, (ccol["gq"], ccol["gk"], ccol["gv"], ccol["gg"]), lrc3,
                            z3, (zcol["gq"], zcol["gk"], zcol["gv"], zcol["gg"]), lr3,
                            w2, b2, gla_gain3, l, ctx_out=not last)

        tm_m = _largest_tile(seq, 512, 16)
        y = _merge(pool_l.reshape(nb * seq, pw), diff_l.reshape(nb * seq, dw), gla_l.reshape(nb * seq, gvw),
                   z, zcol["mg"], wbp, wbd, wbg, l, tm_m)
        x2d_new = _outproj(y, wo, x2d, mod3, l, row_x(tm_m), tm_m)

        if not last:
            pool_c = _pool(zc3, zcol["pu"], zcol["pg"], pool_w, pool_scale3, l, _largest_tile(lc, 512, 16))
            diff_c = _attn(zc3, zcol["dq"], zcol["dg"], zc3, zcol["dk"], zcol["dv"], None, None, None,
                           None, gains3, lam4, l, lambda_init, _largest_tile(lc, 256, 16))
            tm_mc = _largest_tile(nb * lc, 512, 16)
            y_c = _merge(pool_c.reshape(nb * lc, pw), diff_c.reshape(nb * lc, dw), gla_c.reshape(nb * lc, gvw),
                         zc, zcol["mg"], wbp, wbd, wbg, l, tm_mc)
            ctx2d = _outproj(y_c, wo, ctx2d, mod3, l, row_c, tm_mc)
        x2d = x2d_new

    return x2d.reshape(nb, seq, d)
```

```python
import functools
import math

import jax
import jax.numpy as jnp
import numpy as np
from jax import lax
from jax.experimental import pallas as pl
from jax.experimental.pallas import tpu as pltpu

F32 = jnp.float32
BF16 = jnp.bfloat16

EPS = 1e-6
GRID_W = 64
ROPE_THETA = 10000.0

POOL_WINDOWS = (2, 4, 8, 16)
POOL_HALO = 16
DIFF_HEADS = 8
DIFF_HEAD_DIM = 64
GLA_HEADS = 4
GLA_DK = 128
GLA_DV = 256
GLA_RANK = 16
GLA_NORMALIZER = 16.0
GLA_CHUNK = 256
LANES = 128
LOG2E = math.log2(math.e)

VMEM_LIMIT = 48 * 1024 * 1024


def _cparams(*sem):
    return pltpu.CompilerParams(dimension_semantics=sem, vmem_limit_bytes=VMEM_LIMIT)


def _sigmoid(x):
    return 1.0 / (1.0 + jnp.exp(-x))


def _silu(x):
    return x * _sigmoid(x)


def _dot(a, b):
    return jnp.dot(a, b, preferred_element_type=F32)


def _dot_nt(a, b):
    return lax.dot_general(a, b, (((1,), (1,)), ((), ())), preferred_element_type=F32)


def _dot_tn(a, b):
    return lax.dot_general(a, b, (((0,), (0,)), ((), ())), preferred_element_type=F32)


def _ada_kernel(cc_ref, w_ref, b_ref, o_ref):
    @pl.when(pl.program_id(0) == 0)
    def _():
        o_ref[...] = jnp.broadcast_to(b_ref[...], o_ref.shape)

    a = _silu(cc_ref[...]).astype(BF16)
    o_ref[...] += _dot(a, w_ref[...].astype(BF16))


def _ada(cc, w_ada, b_ada3, l, tk=256):
    rows, d = cc.shape
    n = w_ada.shape[2]
    return pl.pallas_call(
        _ada_kernel,
        grid=(d // tk,),
        in_specs=[pl.BlockSpec((rows, tk), lambda k: (0, k)),
                  pl.BlockSpec((None, tk, n), lambda k: (l, k, 0)),
                  pl.BlockSpec((None, 1, n), lambda k: (l, 0, 0))],
        out_specs=pl.BlockSpec((rows, n), lambda k: (0, 0)),
        out_shape=jax.ShapeDtypeStruct((rows, n), F32),
        compiler_params=_cparams("arbitrary"),
        name="ada",
    )(cc, w_ada, b_ada3)


def _prenorm_kernel(x_ref, c_ref, mod_ref, g_ref, wlr_ref, h_ref, lrx_ref, lrc_ref, *, n_x):
    def emit(src_ref, lr_ref):
        x = src_ref[...]
        ms = jnp.mean(x * x, axis=-1, keepdims=True)
        y = x * lax.rsqrt(ms + EPS) * g_ref[...]
        h = (y * (1.0 + mod_ref[1:2, :]) + mod_ref[0:1, :]).astype(BF16)
        h_ref[...] = h
        lr_ref[...] = _dot(h, wlr_ref[...].astype(BF16)).astype(lr_ref.dtype)

    @pl.when(pl.program_id(0) < n_x)
    def _():
        emit(x_ref, lrx_ref)

    @pl.when(pl.program_id(0) >= n_x)
    def _():
        emit(c_ref, lrc_ref)


def _prenorm(x2d, ctx2d, mod3, norm_g3, w_in, l, lr_col, seq, nb, tm):
    mx, d = x2d.shape
    mc = ctx2d.shape[0]
    n_x, n_c = mx // tm, mc // tm
    return pl.pallas_call(
        functools.partial(_prenorm_kernel, n_x=n_x),
        grid=(n_x + n_c,),
        in_specs=[pl.BlockSpec((tm, d), lambda i: (jnp.minimum(i, n_x - 1), 0)),
                  pl.BlockSpec((tm, d), lambda i: (jnp.maximum(i - n_x, 0), 0)),
                  pl.BlockSpec((None, 3, d), lambda i: (jnp.where(i < n_x, (i * tm) // seq, nb), 0, 0)),
                  pl.BlockSpec((None, 1, d), lambda i: (l, 0, 0)),
                  pl.BlockSpec((None, d, LANES), lambda i: (l, 0, lr_col // LANES))],
        out_specs=[pl.BlockSpec((tm, d), lambda i: (i, 0)),
                   pl.BlockSpec((tm, LANES), lambda i: (jnp.minimum(i, n_x - 1), 0)),
                   pl.BlockSpec((tm, LANES), lambda i: (jnp.maximum(i - n_x, 0), 0))],
        out_shape=[jax.ShapeDtypeStruct((mx + mc, d), BF16),
                   jax.ShapeDtypeStruct((mx, LANES), BF16),
                   jax.ShapeDtypeStruct((mc, LANES), BF16)],
        compiler_params=_cparams("arbitrary"),
        name="prenorm",
    )(x2d, ctx2d, mod3, norm_g3, w_in)


def _inproj_kernel(h_ref, wa_ref, wb_ref, zx_ref, zc_ref, wbf_ref, *, n_x, n_shift, shift, ctx_tiles):
    j, i = pl.program_id(0), pl.program_id(1)
    tn = wbf_ref.shape[1]

    @pl.when(i == 0)
    def _():
        @pl.when(j < n_shift)
        def _():
            window = jnp.concatenate([wa_ref[...], wb_ref[...]], axis=1)
            wbf_ref[...] = pltpu.roll(window, tn + LANES - shift, 1)[:, :tn].astype(BF16)

        @pl.when(j >= n_shift)
        def _():
            wbf_ref[...] = wa_ref[...].astype(BF16)

    @pl.when(i < n_x)
    def _():
        zx_ref[...] = _dot(h_ref[...], wbf_ref[...]).astype(zx_ref.dtype)

    @pl.when(i >= n_x)
    def _():
        if ctx_tiles is None:
            zc_ref[...] = _dot(h_ref[...], wbf_ref[...]).astype(zc_ref.dtype)
        else:
            needed = functools.reduce(jnp.logical_or, [j == t for t in ctx_tiles])

            @pl.when(needed)
            def _():
                zc_ref[...] = _dot(h_ref[...], wbf_ref[...]).astype(zc_ref.dtype)

            @pl.when(jnp.logical_not(needed))
            def _():
                zc_ref[...] = jnp.zeros_like(zc_ref)


def _inproj(h, w_in, l, lr_col, n_gate, mx, tm, tn, ctx_tiles):
    m, d = h.shape
    n_x = mx // tm
    n_shift = n_gate // tn
    shift = w_in.shape[2] - lr_col - n_gate
    first_gate = lr_col // tn
    return pl.pallas_call(
        functools.partial(_inproj_kernel, n_x=n_x, n_shift=n_shift, shift=shift, ctx_tiles=ctx_tiles),
        grid=(n_shift + first_gate, m // tm),
        in_specs=[pl.BlockSpec((tm, d), lambda j, i: (i, 0)),
                  pl.BlockSpec((None, d, tn), lambda j, i: (l, 0, jnp.where(j < n_shift, first_gate + j, j - n_shift))),
                  pl.BlockSpec((None, d, LANES),
                               lambda j, i: (l, 0, jnp.where(j < n_shift, (lr_col + (j + 1) * tn) // LANES, 0)))],
        out_specs=[pl.BlockSpec((tm, tn), lambda j, i: (jnp.minimum(i, n_x - 1), j)),
                   pl.BlockSpec((tm, tn), lambda j, i: (jnp.maximum(i - n_x, 0), j))],
        out_shape=[jax.ShapeDtypeStruct((mx, n_gate + lr_col), BF16),
                   jax.ShapeDtypeStruct((m - mx, n_gate + lr_col), BF16)],
        scratch_shapes=[pltpu.VMEM((d, tn), BF16)],
        compiler_params=_cparams("arbitrary", "arbitrary"),
        name="inproj",
    )(h, w_in, w_in)


def _pool_kernel(up_ref, uc_ref, un_ref, pg_ref, w_ref, sc_ref, o_ref, *, tile, seq_len):
    r = pl.program_id(1)
    half = jnp.left_shift(1, pl.program_id(2))
    base = r * tile
    u_all = jnp.concatenate([up_ref[...], uc_ref[...], un_ref[...]], axis=0)
    width = tile + 2 * POOL_HALO
    t = base + lax.broadcasted_iota(jnp.int32, (tile, width), 0)
    s = base - POOL_HALO + lax.broadcasted_iota(jnp.int32, (tile, width), 1)
    inside = (s >= jnp.maximum(t - half, 0)) & (s < jnp.minimum(t + half, seq_len))
    band = jnp.where(inside, 1.0, 0.0).astype(BF16)
    wsum = _dot(band, u_all)
    tc = base + lax.broadcasted_iota(jnp.int32, (tile, 1), 0)
    cnt = (jnp.minimum(tc + half, seq_len) - jnp.maximum(tc - half, 0)).astype(F32)
    dcen = wsum / cnt - uc_ref[...].astype(F32)
    y = _dot(dcen.astype(BF16), w_ref[...].astype(BF16)) * sc_ref[...]
    o_ref[...] = (y * _silu(pg_ref[...].astype(F32))).astype(o_ref.dtype)


def _pool(z3, col_u, col_g, pool_w, pool_scale3, l, tile):
    nseq, seq_len, _ = z3.shape
    ngrp = len(POOL_WINDOWS)
    gw = pool_w.shape[-1]
    cu, cg = col_u // gw, col_g // gw
    nhalo = seq_len // POOL_HALO
    per = tile // POOL_HALO
    kern = functools.partial(_pool_kernel, tile=tile, seq_len=seq_len)
    return pl.pallas_call(
        kern,
        grid=(nseq, seq_len // tile, ngrp),
        in_specs=[
            pl.BlockSpec((None, POOL_HALO, gw), lambda s, r, g: (s, jnp.maximum(r * per - 1, 0), cu + g)),
            pl.BlockSpec((None, tile, gw), lambda s, r, g: (s, r, cu + g)),
            pl.BlockSpec((None, POOL_HALO, gw), lambda s, r, g: (s, jnp.minimum((r + 1) * per, nhalo - 1), cu + g)),
            pl.BlockSpec((None, tile, gw), lambda s, r, g: (s, r, cg + g)),
            pl.BlockSpec((None, None, gw, gw), lambda s, r, g: (l, g, 0, 0)),
            pl.BlockSpec((None, 1, gw), lambda s, r, g: (l, 0, g)),
        ],
        out_specs=pl.BlockSpec((None, tile, gw), lambda s, r, g: (s, r, g)),
        out_shape=jax.ShapeDtypeStruct((nseq, seq_len, ngrp * gw), BF16),
        compiler_params=_cparams("arbitrary", "arbitrary", "arbitrary"),
        name="pool",
    )(z3, z3, z3, z3, pool_w, pool_scale3)


def _head_norm(t, gain, gmat):
    sq = t * t
    hi = sq.astype(BF16)
    lo = (sq - hi.astype(F32)).astype(BF16)
    ssq = _dot(hi, gmat) + _dot(lo, gmat)
    return t * lax.rsqrt(ssq * (1.0 / DIFF_HEAD_DIM) + EPS) * gain


def _rope(t, cos_t, sin_s, first_half):
    half = DIFF_HEAD_DIM // 2
    lower = pltpu.roll(t, half, 1)
    upper = pltpu.roll(t, LANES - half, 1)
    return t * cos_t + jnp.where(first_half, upper, lower) * sin_s


def _attn_kernel(*refs, tq, row_group, rope_q, has_latent, lambda_init):
    if has_latent:
        (q_ref, dg_ref, kc_ref, vc_ref, kl_ref, vl_ref, cosq_ref, sinq_ref, cosk_ref, sink_ref,
         gains_ref, lam_ref, o_ref, kcs_ref, vcs_ref, kls_ref, vls_ref) = refs
    else:
        (q_ref, dg_ref, kc_ref, vc_ref, gains_ref, lam_ref, o_ref, kcs_ref, vcs_ref) = refs
    hw = 2 * DIFF_HEAD_DIM
    q_gain, k_gain, sub_gain = gains_ref[0:1, :], gains_ref[1:2, :], gains_ref[2:3, :]

    lane = lax.broadcasted_iota(jnp.int32, (1, LANES), 1)
    first_half = (lane & (DIFF_HEAD_DIM - 1)) < (DIFF_HEAD_DIM // 2)
    sub1 = lane < DIFF_HEAD_DIM
    gi = jnp.where(lax.broadcasted_iota(jnp.int32, (LANES, LANES), 0) < DIFF_HEAD_DIM, 1.0, 0.0)
    gj = jnp.where(lax.broadcasted_iota(jnp.int32, (LANES, LANES), 1) < DIFF_HEAD_DIM, 1.0, 0.0)
    gmat = (gi * gj + (1.0 - gi) * (1.0 - gj)).astype(BF16)

    @pl.when(pl.program_id(2) == 0)
    def _():
        kcs_ref[...] = _head_norm(kc_ref[...].astype(F32), k_gain, gmat).astype(BF16)
        vcs_ref[:, :hw] = vc_ref[...]
        vcs_ref[:, hw:] = jnp.ones((vc_ref.shape[0], hw), BF16)
        if has_latent:
            kl = _head_norm(kl_ref[...].astype(F32), k_gain, gmat)
            kls_ref[...] = _rope(kl, cosk_ref[...], sink_ref[...], first_half).astype(BF16)
            vls_ref[:, :hw] = vl_ref[...]
            vls_ref[:, hw:] = jnp.ones((vl_ref.shape[0], hw), BF16)

    q = _head_norm(q_ref[...].astype(F32), q_gain, gmat)
    if rope_q:
        q = _rope(q, cosq_ref[...], sinq_ref[...], first_half)
    q = q * (DIFF_HEAD_DIM ** -0.5 * LOG2E)
    qq = jnp.concatenate([jnp.where(sub1, q, 0.0), jnp.where(sub1, 0.0, q)], axis=0).astype(BF16)

    segs = [(kcs_ref, vcs_ref)] + ([(kls_ref, vls_ref)] if has_latent else [])
    pvs = []
    for r0 in range(0, 2 * tq, row_group):
        qg = qq[r0:r0 + row_group]
        scores = [_dot_nt(qg, k_ref[...]) for k_ref, _ in segs]
        mx = functools.reduce(jnp.maximum, [jnp.max(s, axis=-1, keepdims=True) for s in scores])
        acc = None
        for s, (_, v_ref) in zip(scores, segs):
            part = _dot(jnp.exp2(s - mx).astype(BF16), v_ref[...])
            acc = part if acc is None else acc + part
        pvs.append(acc[:, :hw] / acc[:, hw:hw + 1])
    pv = jnp.concatenate(pvs, axis=0)

    lam_p = lam_ref[...]
    lam = (jnp.exp(jnp.sum(lam_p[0:1] * lam_p[1:2], axis=-1, keepdims=True))
           - jnp.exp(jnp.sum(lam_p[2:3] * lam_p[3:4], axis=-1, keepdims=True)) + lambda_init)
    o = pv[:tq] - lam * pv[tq:]

    ms = jnp.mean(o * o, axis=-1, keepdims=True)
    y = o * lax.rsqrt(ms + EPS) * sub_gain * (1.0 - lambda_init)
    o_ref[...] = (y * _silu(dg_ref[...].astype(F32))).astype(o_ref.dtype)


def _attn(zq3, cq, cdg, zc3, ckc, cvc, zl3, ckl, cvl, rope_tabs, gains3, lam4, l, lambda_init, tq):
    nb, lq, _ = zq3.shape
    lc = zc3.shape[1]
    has_latent = zl3 is not None
    hw = 2 * DIFF_HEAD_DIM
    b_of = lambda c: c // hw
    qmap = lambda off: (lambda b, h, i: (b, i, b_of(off) + h))
    kmap = lambda off: (lambda b, h, i: (b, 0, b_of(off) + h))
    const = lambda b, h, i: (0, 0)
    in_specs = [pl.BlockSpec((None, tq, hw), qmap(cq)),
                pl.BlockSpec((None, tq, hw), qmap(cdg)),
                pl.BlockSpec((None, lc, hw), kmap(ckc)),
                pl.BlockSpec((None, lc, hw), kmap(cvc))]
    args = [zq3, zq3, zc3, zc3]
    scratch = [pltpu.VMEM((lc, hw), BF16), pltpu.VMEM((lc, 2 * hw), BF16)]
    if has_latent:
        ll = zl3.shape[1]
        cos_t, sin_s = rope_tabs
        in_specs += [pl.BlockSpec((None, ll, hw), kmap(ckl)),
                     pl.BlockSpec((None, ll, hw), kmap(cvl)),
                     pl.BlockSpec((tq, hw), lambda b, h, i: (i, 0)),
                     pl.BlockSpec((tq, hw), lambda b, h, i: (i, 0)),
                     pl.BlockSpec((ll, hw), const),
                     pl.BlockSpec((ll, hw), const)]
        args += [zl3, zl3, cos_t, sin_s, cos_t, sin_s]
        scratch += [pltpu.VMEM((ll, hw), BF16), pltpu.VMEM((ll, 2 * hw), BF16)]
    in_specs += [pl.BlockSpec((None, 3, hw), lambda b, h, i: (l, 0, 0)),
                 pl.BlockSpec((None, 4, DIFF_HEAD_DIM), lambda b, h, i: (l, 0, 0))]
    args += [gains3, lam4]
    kern = functools.partial(_attn_kernel, tq=tq, row_group=min(2 * tq, LANES), rope_q=has_latent, has_latent=has_latent,
                             lambda_init=lambda_init)
    return pl.pallas_call(
        kern,
        grid=(nb, DIFF_HEADS, lq // tq),
        in_specs=in_specs,
        out_specs=pl.BlockSpec((None, tq, hw), lambda b, h, i: (b, i, h)),
        out_shape=jax.ShapeDtypeStruct((nb, lq, DIFF_HEADS * hw), BF16),
        scratch_shapes=scratch,
        compiler_params=_cparams("arbitrary", "arbitrary", "arbitrary"),
        name="diff_attn",
    )(*args)


def _log_sigmoid(x):
    return jnp.minimum(x, 0.0) - jnp.log1p(jnp.exp(-jnp.abs(x)))


GLA_LEVELS = tuple(1 << i for i in range(GLA_CHUNK.bit_length() - 1))
GLA_TAB_LEVELS = tuple(m for m in GLA_LEVELS if m < 8)
GLA_ROW_LEVELS = tuple(m for m in GLA_LEVELS if m >= 8)
GLA_SLOT_CUM = len(GLA_TAB_LEVELS)
GLA_UNROLL = 2
GLA_TOT_ROWS = 16


def _gla_boundary(m, rev):
    return m if rev else m - 1


def _gla_tables():
    c = GLA_CHUNK
    t = np.arange(c)[:, None]
    j = np.arange(c)[None, :]
    sums, masks = [], []
    for rev in (False, True):
        rows, mk = [], []
        for m in GLA_LEVELS:
            base = t & ~(2 * m - 1)
            bd = base + _gla_boundary(m, rev)
            is_q = ((t & m) == 0) if rev else ((t & m) != 0)
            if rev:
                between = np.where(is_q, (j >= t) & (j < bd), (j >= bd) & (j < t))
            else:
                between = np.where(is_q, (j > bd) & (j <= t), (j > t) & (j <= bd))
            if m in GLA_TAB_LEVELS:
                rows.append(between)
            mk.append((base == (j & ~(2 * m - 1))) & is_q & (((j & m) != 0) if rev else ((j & m) == 0)))
        rows.append(j >= t if rev else j <= t)
        rows.append(np.ones((GLA_TOT_ROWS, c), bool))
        mk.append(t == j)
        sums.append(np.concatenate(rows, axis=0))
        masks.append(np.stack(mk))
    return (jnp.asarray(np.stack(sums), BF16), jnp.asarray(np.stack(masks), F32))


def _gla_chunk(q, k, v, ghl, st, tab_ref, mask_ref, rev, need_out):
    c = GLA_CHUNK
    first = 0 if need_out else GLA_SLOT_CUM
    x2 = _dot(tab_ref[first * c:, :], ghl)
    x = x2[:, :GLA_DK] + x2[:, GLA_DK:]
    cum = x[(GLA_SLOT_CUM - first) * c:(GLA_SLOT_CUM - first + 1) * c]
    tot = x[-GLA_TOT_ROWS:1 - GLA_TOT_ROWS]
    kd = (k * jnp.exp2(tot - cum)).astype(BF16)
    st_new = st * jnp.exp2(tot) + _dot_tn(v, kd)
    if not need_out:
        return None, st_new

    out = _dot_nt((q * jnp.exp2(cum)).astype(BF16), st.astype(BF16))
    att = _dot_nt(q.astype(BF16), k.astype(BF16)) * mask_ref[len(GLA_LEVELS)]
    for lvl, m in enumerate(GLA_LEVELS):
        if m in GLA_TAB_LEVELS:
            xl = x[lvl * c:(lvl + 1) * c]
        else:
            anchor = jnp.concatenate(
                [jnp.broadcast_to(cum[b0 + _gla_boundary(m, rev):b0 + _gla_boundary(m, rev) + 1], (2 * m, GLA_DK))
                 for b0 in range(0, c, 2 * m)], axis=0)
            xl = -jnp.abs(cum - anchor)
        el = jnp.exp2(xl)
        att = att + _dot_nt((q * el).astype(BF16), (k * el).astype(BF16)) * mask_ref[lvl]
    return out + _dot(att.astype(BF16), v), st_new


def _gla_kernel(*refs, n_ctx, n_lat, ctx_out):
    (qc_ref, kc_ref, vc_ref, ggc_ref, lrc_ref, ql_ref, kl_ref, vl_ref, ggl_ref, lrl_ref,
     w2_ref, b2_ref, gain_ref, tab_ref, mask_ref) = refs[:15]
    rest = refs[15:]
    if ctx_out:
        ol_ref, oc_ref, gl_ref, gc_ref, accl_ref, accc_ref, stf_ref, stb_ref = rest
    else:
        ol_ref, gl_ref, gc_ref, accl_ref, stf_ref, stb_ref = rest
        oc_ref = accc_ref = None

    def decays(lr_ref, g_ref):
        g = _log_sigmoid(_dot(lr_ref[...], w2_ref[...]) + b2_ref[...]) * (LOG2E / GLA_NORMALIZER)
        hi = g.astype(BF16)
        lo = (g - hi.astype(F32)).astype(BF16)
        for d in range(2):
            g_ref[:, (2 * d) * GLA_DK:(2 * d + 1) * GLA_DK] = hi[:, d * GLA_DK:(d + 1) * GLA_DK]
            g_ref[:, (2 * d + 1) * GLA_DK:(2 * d + 2) * GLA_DK] = lo[:, d * GLA_DK:(d + 1) * GLA_DK]

    decays(lrl_ref, gl_ref)
    decays(lrc_ref, gc_ref)
    stf_ref[...] = jnp.zeros_like(stf_ref)
    stb_ref[...] = jnp.zeros_like(stb_ref)
    scale = GLA_DK ** -0.5

    def scan(n, q_ref, k_ref, v_ref, g_ref, acc_ref):
        need_out = acc_ref is not None

        def step(i, assign_up, assign_down):
            for c_idx, d, st_ref, first_visit in ((i, 0, stf_ref, assign_up), (n - 1 - i, 1, stb_ref, assign_down)):
                rows = pl.ds(pl.multiple_of(c_idx * GLA_CHUNK, GLA_CHUNK), GLA_CHUNK)
                q = q_ref[rows, :].astype(F32) * scale
                k = k_ref[rows, :].astype(F32)
                ghl = g_ref[rows, 2 * d * GLA_DK:(2 * d + 2) * GLA_DK]
                out, st_new = _gla_chunk(q, k, v_ref[rows, :], ghl, st_ref[...], tab_ref.at[d], mask_ref.at[d],
                                         d == 1, need_out)
                st_ref[...] = st_new
                if need_out:
                    if first_visit:
                        acc_ref[rows, :] = out
                    else:
                        acc_ref[rows, :] += out

        def first(i, carry):
            step(i, True, True)
            return carry

        def second(i, carry):
            step(i, False, False)
            return carry

        if n // 2:
            lax.fori_loop(0, n // 2, first, 0, unroll=min(GLA_UNROLL, n // 2))
        if n % 2:
            step(n // 2, True, False)
        if n // 2:
            lax.fori_loop((n + 1) // 2, n, second, 0, unroll=min(GLA_UNROLL, n // 2))

    def finish(acc_ref, gg_ref, o_ref):
        o = acc_ref[...]
        ms = jnp.mean(o * o, axis=-1, keepdims=True)
        y = o * lax.rsqrt(ms + EPS) * gain_ref[...]
        o_ref[...] = (y * _silu(gg_ref[...].astype(F32))).astype(o_ref.dtype)

    scan(n_ctx, qc_ref, kc_ref, vc_ref, gc_ref, accc_ref)
    scan(n_lat, ql_ref, kl_ref, vl_ref, gl_ref, accl_ref)
    finish(accl_ref, ggl_ref, ol_ref)
    if ctx_out:
        finish(accc_ref, ggc_ref, oc_ref)


def _gla(zc3, cc, lrc3, zl3, cl, lrl3, w2, b2, gain3, l, ctx_out):
    nb, lc, _ = zc3.shape
    ll = zl3.shape[1]
    assert lc % GLA_CHUNK == 0 and ll % GLA_CHUNK == 0
    tabs, masks = _gla_tables()

    def seq_specs(z3, cols, lr3, slen):
        qo, ko, vo, go = cols
        return ([pl.BlockSpec((None, slen, GLA_DK), lambda b, h: (b, 0, qo // GLA_DK + h)),
                 pl.BlockSpec((None, slen, GLA_DK), lambda b, h: (b, 0, ko // GLA_DK + h)),
                 pl.BlockSpec((None, slen, GLA_DV), lambda b, h: (b, 0, vo // GLA_DV + h)),
                 pl.BlockSpec((None, slen, GLA_DV), lambda b, h: (b, 0, (0 if go is None else go) // GLA_DV + h)),
                 pl.BlockSpec((None, slen, LANES), lambda b, h: (b, 0, 0))],
                [z3, z3, z3, z3, lr3])

    sc, ac = seq_specs(zc3, cc, lrc3, lc)
    sl, al = seq_specs(zl3, cl, lrl3, ll)
    in_specs = sc + sl + [pl.BlockSpec((None, None, LANES, 2 * GLA_DK), lambda b, h: (l, h, 0, 0)),
                          pl.BlockSpec((None, None, 1, 2 * GLA_DK), lambda b, h: (l, h, 0, 0)),
                          pl.BlockSpec((None, 1, GLA_DV), lambda b, h: (l, 0, 0)),
                          pl.BlockSpec(tabs.shape, lambda b, h: (0, 0, 0)),
                          pl.BlockSpec(masks.shape, lambda b, h: (0, 0, 0, 0))]
    args = ac + al + [w2, b2, gain3, tabs, masks]
    out_specs = [pl.BlockSpec((None, ll, GLA_DV), lambda b, h: (b, 0, h))]
    out_shape = [jax.ShapeDtypeStruct((nb, ll, GLA_HEADS * GLA_DV), BF16)]
    scratch = [pltpu.VMEM((ll, 4 * GLA_DK), BF16), pltpu.VMEM((lc, 4 * GLA_DK), BF16),
               pltpu.VMEM((ll, GLA_DV), F32)]
    if ctx_out:
        out_specs.append(pl.BlockSpec((None, lc, GLA_DV), lambda b, h: (b, 0, h)))
        out_shape.append(jax.ShapeDtypeStruct((nb, lc, GLA_HEADS * GLA_DV), BF16))
        scratch.append(pltpu.VMEM((lc, GLA_DV), F32))
    scratch += [pltpu.VMEM((GLA_DV, GLA_DK), F32), pltpu.VMEM((GLA_DV, GLA_DK), F32)]
    kern = functools.partial(_gla_kernel, n_ctx=lc // GLA_CHUNK, n_lat=ll // GLA_CHUNK, ctx_out=ctx_out)
    outs = pl.pallas_call(
        kern,
        grid=(nb, GLA_HEADS),
        in_specs=in_specs,
        out_specs=out_specs,
        out_shape=out_shape,
        scratch_shapes=scratch,
        compiler_params=_cparams("arbitrary", "arbitrary"),
        name="gla",
    )(*args)
    return (outs[0], outs[1]) if ctx_out else (outs[0], None)


def _merge_kernel(p_ref, d_ref, g_ref, wp_ref, wd_ref, wg_ref, mp_ref, md_ref, mg_ref, y_ref):
    y = (_sigmoid(mp_ref[...].astype(F32)) * _dot(p_ref[...], wp_ref[...])
         + _sigmoid(md_ref[...].astype(F32)) * _dot(d_ref[...], wd_ref[...])
         + _sigmoid(mg_ref[...].astype(F32)) * _dot(g_ref[...], wg_ref[...]))
    y_ref[...] = y.astype(y_ref.dtype)


def _merge(pool_o, diff_o, gla_o, z2d, col_mg, wbp, wbd, wbg, l, tm):
    m, kw = pool_o.shape
    d = wbp.shape[2]
    act = pl.BlockSpec((tm, kw), lambda i: (i, 0))
    wsp = pl.BlockSpec((None, kw, d), lambda i: (l, 0, 0), pipeline_mode=pl.Buffered(1))
    gate = lambda k: pl.BlockSpec((tm, d), lambda i: (i, col_mg // d + k))
    return pl.pallas_call(
        _merge_kernel,
        grid=(m // tm,),
        in_specs=[act, act, act, wsp, wsp, wsp, gate(0), gate(1), gate(2)],
        out_specs=pl.BlockSpec((tm, d), lambda i: (i, 0)),
        out_shape=jax.ShapeDtypeStruct((m, d), BF16),
        compiler_params=_cparams("arbitrary"),
        name="merge",
    )(pool_o, diff_o, gla_o, wbp, wbd, wbg, z2d, z2d, z2d)


def _outproj_kernel(y_ref, w_ref, x_ref, mod_ref, o_ref):
    o_ref[...] = x_ref[...] + mod_ref[2:3, :] * _dot(y_ref[...], w_ref[...])


def _outproj(y, w_out, x2d, mod3, l, row_of_tile, tm):
    m, d = x2d.shape
    return pl.pallas_call(
        _outproj_kernel,
        grid=(m // tm,),
        in_specs=[pl.BlockSpec((tm, d), lambda i: (i, 0)),
                  pl.BlockSpec((None, d, d), lambda i: (l, 0, 0), pipeline_mode=pl.Buffered(1)),
                  pl.BlockSpec((tm, d), lambda i: (i, 0)),
                  pl.BlockSpec((None, 3, d), lambda i: (row_of_tile(i), 0, 0))],
        out_specs=pl.BlockSpec((tm, d), lambda i: (i, 0)),
        out_shape=jax.ShapeDtypeStruct((m, d), F32),
        compiler_params=_cparams("arbitrary"),
        name="outproj",
    )(y, w_out, x2d, mod3)


def _rope_tables(seq_len):
    n_freq = DIFF_HEAD_DIM // 4
    t = jnp.arange(seq_len)
    inv = ROPE_THETA ** (-jnp.arange(n_freq, dtype=F32) / n_freq)
    ang = jnp.concatenate([(t // GRID_W).astype(F32)[:, None] * inv,
                           (t % GRID_W).astype(F32)[:, None] * inv], axis=-1)
    cos, sin = jnp.cos(ang), jnp.sin(ang)
    return jnp.tile(cos, (1, 4)), jnp.tile(jnp.concatenate([-sin, sin], axis=-1), (1, 2))


def _largest_tile(n, cap, mult):
    t = min(n, cap)
    while n % t or t % mult:
        t -= mult
    return t


def kernel(x, c, ctx, c_ctx, norm_g, w_ada, b_ada, w_in, pool_w, pool_scale, diff_q_norm, diff_k_norm, diff_lam_q1, diff_lam_k1, diff_lam_q2, diff_lam_k2, diff_subln, gla_w_gate_f, gla_b_gate_f, gla_w_gate_b, gla_b_gate_b, gla_norm, w_branch_pool, w_branch_diff, w_branch_gla, w_out):
    nb, seq, d = x.shape
    lc = ctx.shape[1]
    depth = w_in.shape[0]
    pw = pool_scale.shape[1]
    dw = DIFF_HEADS * 2 * DIFF_HEAD_DIM
    gkw, gvw = GLA_HEADS * GLA_DK, GLA_HEADS * GLA_DV

    sizes = dict(pu=pw, pg=pw, dq=dw, dk=dw, dv=dw, dg=dw, gq=gkw, gk=gkw, gv=gvw, gg=gvw)
    wcol, off = {}, 0
    for name, size in sizes.items():
        wcol[name] = off
        off += size
    lr_col = off
    zcol = {name: 3 * d + o for name, o in wcol.items()}
    zcol["mg"] = 0

    rope_tabs = _rope_tables(seq)
    cc = jnp.zeros((8, d), F32).at[:nb].set(c).at[nb].set(c_ctx)
    x2d = x.reshape(nb * seq, d)
    ctx2d = ctx.reshape(nb * lc, d)

    wbp, wbd, wbg, wo = (w.astype(BF16) for w in (w_branch_pool, w_branch_diff, w_branch_gla, w_out))
    norm_g3 = norm_g.reshape(depth, 1, d)
    b_ada3 = b_ada.reshape(depth, 1, 3 * d)
    pool_scale3 = pool_scale.reshape(depth, 1, pw)
    gains3 = jnp.stack([jnp.tile(diff_q_norm, (1, 2)), jnp.tile(diff_k_norm, (1, 2)), diff_subln], axis=1)
    lam4 = jnp.stack([diff_lam_q1, diff_lam_k1, diff_lam_q2, diff_lam_k2], axis=1)
    gla_gain3 = gla_norm.reshape(depth, 1, GLA_DV)
    per_head = lambda w: w.reshape(depth, GLA_RANK, GLA_HEADS, GLA_DK).transpose(0, 2, 1, 3)
    w2 = jnp.zeros((depth, GLA_HEADS, LANES, 2 * GLA_DK), F32)
    w2 = w2.at[:, :, :GLA_RANK, :GLA_DK].set(per_head(gla_w_gate_f))
    w2 = w2.at[:, :, GLA_RANK:2 * GLA_RANK, GLA_DK:].set(per_head(gla_w_gate_b)).astype(BF16)
    b2 = jnp.concatenate([gla_b_gate_f.reshape(depth, GLA_HEADS, 1, GLA_DK),
                          gla_b_gate_b.reshape(depth, GLA_HEADS, 1, GLA_DK)], axis=-1)

    rows_gcd = math.gcd(seq, nb * lc)
    tm_pre = _largest_tile(rows_gcd, 512, 16)
    tm_in = _largest_tile(rows_gcd, 1024, 16)
    tn_in = _largest_tile(math.gcd(lr_col, 3 * d), 1024, 256)
    row_x = lambda tm: (lambda i: (i * tm) // seq)
    row_c = lambda i: nb
    kv_cols = ((zcol["dk"], zcol["dg"]), (zcol["gk"], zcol["gg"]))
    kv_tiles = tuple(t for t in range((3 * d + lr_col) // tn_in)
                     if any(lo < (t + 1) * tn_in and t * tn_in < hi for lo, hi in kv_cols))
    ccol = zcol

    for l in range(depth):
        last = l == depth - 1
        lambda_init = 0.8 - 0.6 * math.exp(-0.3 * l)
        mod3 = _ada(cc, w_ada, b_ada3, l).reshape(8, 3, d)
        h, lr, lrc = _prenorm(x2d, ctx2d, mod3, norm_g3, w_in, l, lr_col, seq, nb, tm_pre)
        z, zc = _inproj(h, w_in, l, lr_col, 3 * d, nb * seq, tm_in, tn_in, kv_tiles if last else None)
        z3, zc3 = z.reshape(nb, seq, -1), zc.reshape(nb, lc, -1)
        lr3, lrc3 = lr.reshape(nb, seq, LANES), lrc.reshape(nb, lc, LANES)

        pool_l = _pool(z3, zcol["pu"], zcol["pg"], pool_w, pool_scale3, l, _largest_tile(seq, 512, 16))
        diff_l = _attn(z3, zcol["dq"], zcol["dg"], zc3, ccol["dk"], ccol["dv"], z3, zcol["dk"], zcol["dv"],
                       rope_tabs, gains3, lam4, l, lambda_init, _largest_tile(seq, 1024, 16))
        gla_l, gla_c = _gla(zc3, (ccol["gq"], ccol["gk"], ccol["gv"], ccol["gg"]), lrc3,
                            z3, (zcol["gq"], zcol["gk"], zcol["gv"], zcol["gg"]), lr3,
                            w2, b2, gla_gain3, l, ctx_out=not last)

        tm_m = _largest_tile(seq, 512, 16)
        y = _merge(pool_l.reshape(nb * seq, pw), diff_l.reshape(nb * seq, dw), gla_l.reshape(nb * seq, gvw),
                   z, zcol["mg"], wbp, wbd, wbg, l, tm_m)
        x2d_new = _outproj(y, wo, x2d, mod3, l, row_x(tm_m), tm_m)

        if not last:
            pool_c = _pool(zc3, zcol["pu"], zcol["pg"], pool_w, pool_scale3, l, _largest_tile(lc, 512, 16))
            diff_c = _attn(zc3, zcol["dq"], zcol["dg"], zc3, zcol["dk"], zcol["dv"], None, None, None,
                           None, gains3, lam4, l, lambda_init, _largest_tile(lc, 256, 16))
            tm_mc = _largest_tile(nb * lc, 512, 16)
            y_c = _merge(pool_c.reshape(nb * lc, pw), diff_c.reshape(nb * lc, dw), gla_c.reshape(nb * lc, gvw),
                         zc, zcol["mg"], wbp, wbd, wbg, l, tm_mc)
            ctx2d = _outproj(y_c, wo, ctx2d, mod3, l, row_c, tm_mc)
        x2d = x2d_new

    return x2d.reshape(nb, seq, d)
```

```python
import functools
import math

import jax
import jax.numpy as jnp
import numpy as np
from jax import lax
from jax.experimental import pallas as pl
from jax.experimental.pallas import tpu as pltpu

F32 = jnp.float32
BF16 = jnp.bfloat16

EPS = 1e-6
GRID_W = 64
ROPE_THETA = 10000.0

POOL_WINDOWS = (2, 4, 8, 16)
POOL_HALO = 16
DIFF_HEADS = 8
DIFF_HEAD_DIM = 64
GLA_HEADS = 4
GLA_DK = 128
GLA_DV = 256
GLA_RANK = 16
GLA_NORMALIZER = 16.0
GLA_CHUNK = 256
LANES = 128
LOG2E = math.log2(math.e)

VMEM_LIMIT = 48 * 1024 * 1024


def _cparams(*sem):
    return pltpu.CompilerParams(dimension_semantics=sem, vmem_limit_bytes=VMEM_LIMIT)


def _sigmoid(x):
    return 1.0 / (1.0 + jnp.exp(-x))


def _silu(x):
    return x * _sigmoid(x)


def _dot(a, b):
    return jnp.dot(a, b, preferred_element_type=F32)


def _dot_nt(a, b):
    return lax.dot_general(a, b, (((1,), (1,)), ((), ())), preferred_element_type=F32)


def _dot_tn(a, b):
    return lax.dot_general(a, b, (((0,), (0,)), ((), ())), preferred_element_type=F32)


def _ada_kernel(cc_ref, w_ref, b_ref, o_ref):
    @pl.when(pl.program_id(0) == 0)
    def _():
        o_ref[...] = jnp.broadcast_to(b_ref[...], o_ref.shape)

    a = _silu(cc_ref[...]).astype(BF16)
    o_ref[...] += _dot(a, w_ref[...].astype(BF16))


def _ada(cc, w_ada, b_ada3, l, tk=256):
    rows, d = cc.shape
    n = w_ada.shape[2]
    return pl.pallas_call(
        _ada_kernel,
        grid=(d // tk,),
        in_specs=[pl.BlockSpec((rows, tk), lambda k: (0, k)),
                  pl.BlockSpec((None, tk, n), lambda k: (l, k, 0)),
                  pl.BlockSpec((None, 1, n), lambda k: (l, 0, 0))],
        out_specs=pl.BlockSpec((rows, n), lambda k: (0, 0)),
        out_shape=jax.ShapeDtypeStruct((rows, n), F32),
        compiler_params=_cparams("arbitrary"),
        name="ada",
    )(cc, w_ada, b_ada3)


def _prenorm_kernel(x_ref, c_ref, mod_ref, g_ref, wlr_ref, h_ref, lrx_ref, lrc_ref, *, n_x):
    def emit(src_ref, lr_ref):
        x = src_ref[...]
        ms = jnp.mean(x * x, axis=-1, keepdims=True)
        y = x * lax.rsqrt(ms + EPS) * g_ref[...]
        h = (y * (1.0 + mod_ref[1:2, :]) + mod_ref[0:1, :]).astype(BF16)
        h_ref[...] = h
        lr_ref[...] = _dot_nt(h, wlr_ref[...].astype(BF16)).astype(lr_ref.dtype)

    @pl.when(pl.program_id(0) < n_x)
    def _():
        emit(x_ref, lrx_ref)

    @pl.when(pl.program_id(0) >= n_x)
    def _():
        emit(c_ref, lrc_ref)


def _prenorm(x2d, ctx2d, mod3, norm_g3, w_in_t, l, lr_col, seq, nb, tm):
    mx, d = x2d.shape
    mc = ctx2d.shape[0]
    n_x, n_c = mx // tm, mc // tm
    return pl.pallas_call(
        functools.partial(_prenorm_kernel, n_x=n_x),
        grid=(n_x + n_c,),
        in_specs=[pl.BlockSpec((tm, d), lambda i: (jnp.minimum(i, n_x - 1), 0)),
                  pl.BlockSpec((tm, d), lambda i: (jnp.maximum(i - n_x, 0), 0)),
                  pl.BlockSpec((None, 3, d), lambda i: (jnp.where(i < n_x, (i * tm) // seq, nb), 0, 0)),
                  pl.BlockSpec((None, 1, d), lambda i: (l, 0, 0)),
                  pl.BlockSpec((None, LANES, d), lambda i: (l, lr_col // LANES, 0))],
        out_specs=[pl.BlockSpec((tm, d), lambda i: (i, 0)),
                   pl.BlockSpec((tm, LANES), lambda i: (jnp.minimum(i, n_x - 1), 0)),
                   pl.BlockSpec((tm, LANES), lambda i: (jnp.maximum(i - n_x, 0), 0))],
        out_shape=[jax.ShapeDtypeStruct((mx + mc, d), BF16),
                   jax.ShapeDtypeStruct((mx, LANES), BF16),
                   jax.ShapeDtypeStruct((mc, LANES), BF16)],
        compiler_params=_cparams("arbitrary"),
        name="prenorm",
    )(x2d, ctx2d, mod3, norm_g3, w_in_t)


def _inproj_kernel(h_ref, wa_ref, wb_ref, zx_ref, zc_ref, wbf_ref, *, n_x, n_shift, shift, ctx_tiles):
    j, i = pl.program_id(0), pl.program_id(1)

    @pl.when(i == 0)
    def _():
        @pl.when(j < n_shift)
        def _():
            wbf_ref[...] = jnp.concatenate([wa_ref[shift:, :], wb_ref[...]], axis=0).astype(BF16)

        @pl.when(j >= n_shift)
        def _():
            wbf_ref[...] = wa_ref[...].astype(BF16)

    @pl.when(i < n_x)
    def _():
        zx_ref[...] = _dot_nt(h_ref[...], wbf_ref[...]).astype(zx_ref.dtype)

    @pl.when(i >= n_x)
    def _():
        if ctx_tiles is None:
            zc_ref[...] = _dot_nt(h_ref[...], wbf_ref[...]).astype(zc_ref.dtype)
        else:
            needed = functools.reduce(jnp.logical_or, [j == t for t in ctx_tiles])

            @pl.when(needed)
            def _():
                zc_ref[...] = _dot_nt(h_ref[...], wbf_ref[...]).astype(zc_ref.dtype)

            @pl.when(jnp.logical_not(needed))
            def _():
                zc_ref[...] = jnp.zeros_like(zc_ref)


def _inproj(h, w_in_t, l, lr_col, n_gate, mx, tm, tn, ctx_tiles):
    m, d = h.shape
    n_x = mx // tm
    n_shift = n_gate // tn
    shift = w_in_t.shape[1] - lr_col - n_gate
    assert shift % 8 == 0 and lr_col % tn == 0 and tn % shift == 0
    first_gate = lr_col // tn
    return pl.pallas_call(
        functools.partial(_inproj_kernel, n_x=n_x, n_shift=n_shift, shift=shift, ctx_tiles=ctx_tiles),
        grid=(n_shift + first_gate, m // tm),
        in_specs=[pl.BlockSpec((tm, d), lambda j, i: (i, 0)),
                  pl.BlockSpec((None, tn, d), lambda j, i: (l, jnp.where(j < n_shift, first_gate + j, j - n_shift), 0)),
                  pl.BlockSpec((None, shift, d),
                               lambda j, i: (l, jnp.where(j < n_shift, (lr_col + (j + 1) * tn) // shift, 0), 0))],
        out_specs=[pl.BlockSpec((tm, tn), lambda j, i: (jnp.minimum(i, n_x - 1), j)),
                   pl.BlockSpec((tm, tn), lambda j, i: (jnp.maximum(i - n_x, 0), j))],
        out_shape=[jax.ShapeDtypeStruct((mx, n_gate + lr_col), BF16),
                   jax.ShapeDtypeStruct((m - mx, n_gate + lr_col), BF16)],
        scratch_shapes=[pltpu.VMEM((tn, d), BF16)],
        compiler_params=_cparams("arbitrary", "arbitrary"),
        name="inproj",
    )(h, w_in_t, w_in_t)


def _pool_kernel(up_ref, uc_ref, un_ref, pg_ref, w_ref, sc_ref, o_ref, *, tile, seq_len):
    r = pl.program_id(1)
    half = jnp.left_shift(1, pl.program_id(2))
    base = r * tile
    u_all = jnp.concatenate([up_ref[...], uc_ref[...], un_ref[...]], axis=0)
    width = tile + 2 * POOL_HALO
    t = base + lax.broadcasted_iota(jnp.int32, (tile, width), 0)
    s = base - POOL_HALO + lax.broadcasted_iota(jnp.int32, (tile, width), 1)
    inside = (s >= jnp.maximum(t - half, 0)) & (s < jnp.minimum(t + half, seq_len))
    band = jnp.where(inside, 1.0, 0.0).astype(BF16)
    wsum = _dot(band, u_all)
    tc = base + lax.broadcasted_iota(jnp.int32, (tile, 1), 0)
    cnt = (jnp.minimum(tc + half, seq_len) - jnp.maximum(tc - half, 0)).astype(F32)
    dcen = wsum / cnt - uc_ref[...].astype(F32)
    y = _dot(dcen.astype(BF16), w_ref[...].astype(BF16)) * sc_ref[...]
    o_ref[...] = (y * _silu(pg_ref[...].astype(F32))).astype(o_ref.dtype)


def _pool(z3, col_u, col_g, pool_w, pool_scale3, l, tile):
    nseq, seq_len, _ = z3.shape
    ngrp = len(POOL_WINDOWS)
    gw = pool_w.shape[-1]
    cu, cg = col_u // gw, col_g // gw
    nhalo = seq_len // POOL_HALO
    per = tile // POOL_HALO
    kern = functools.partial(_pool_kernel, tile=tile, seq_len=seq_len)
    return pl.pallas_call(
        kern,
        grid=(nseq, seq_len // tile, ngrp),
        in_specs=[
            pl.BlockSpec((None, POOL_HALO, gw), lambda s, r, g: (s, jnp.maximum(r * per - 1, 0), cu + g)),
            pl.BlockSpec((None, tile, gw), lambda s, r, g: (s, r, cu + g)),
            pl.BlockSpec((None, POOL_HALO, gw), lambda s, r, g: (s, jnp.minimum((r + 1) * per, nhalo - 1), cu + g)),
            pl.BlockSpec((None, tile, gw), lambda s, r, g: (s, r, cg + g)),
            pl.BlockSpec((None, None, gw, gw), lambda s, r, g: (l, g, 0, 0)),
            pl.BlockSpec((None, 1, gw), lambda s, r, g: (l, 0, g)),
        ],
        out_specs=pl.BlockSpec((None, tile, gw), lambda s, r, g: (s, r, g)),
        out_shape=jax.ShapeDtypeStruct((nseq, seq_len, ngrp * gw), BF16),
        compiler_params=_cparams("arbitrary", "arbitrary", "arbitrary"),
        name="pool",
    )(z3, z3, z3, z3, pool_w, pool_scale3)


def _head_norm(t, gain, gmat):
    sq = t * t
    hi = sq.astype(BF16)
    lo = (sq - hi.astype(F32)).astype(BF16)
    ssq = _dot(hi, gmat) + _dot(lo, gmat)
    return t * lax.rsqrt(ssq * (1.0 / DIFF_HEAD_DIM) + EPS) * gain


def _rope(t, cos_t, sin_s, first_half):
    half = DIFF_HEAD_DIM // 2
    lower = pltpu.roll(t, half, 1)
    upper = pltpu.roll(t, LANES - half, 1)
    return t * cos_t + jnp.where(first_half, upper, lower) * sin_s


def _attn_kernel(*refs, tq, row_group, rope_q, has_latent, lambda_init):
    if has_latent:
        (q_ref, dg_ref, kc_ref, vc_ref, kl_ref, vl_ref, cosq_ref, sinq_ref, cosk_ref, sink_ref,
         gains_ref, lam_ref, o_ref, kcs_ref, vcs_ref, kls_ref, vls_ref) = refs
    else:
        (q_ref, dg_ref, kc_ref, vc_ref, gains_ref, lam_ref, o_ref, kcs_ref, vcs_ref) = refs
    hw = 2 * DIFF_HEAD_DIM
    q_gain, k_gain, sub_gain = gains_ref[0:1, :], gains_ref[1:2, :], gains_ref[2:3, :]

    lane = lax.broadcasted_iota(jnp.int32, (1, LANES), 1)
    first_half = (lane & (DIFF_HEAD_DIM - 1)) < (DIFF_HEAD_DIM // 2)
    sub1 = lane < DIFF_HEAD_DIM
    gi = jnp.where(lax.broadcasted_iota(jnp.int32, (LANES, LANES), 0) < DIFF_HEAD_DIM, 1.0, 0.0)
    gj = jnp.where(lax.broadcasted_iota(jnp.int32, (LANES, LANES), 1) < DIFF_HEAD_DIM, 1.0, 0.0)
    gmat = (gi * gj + (1.0 - gi) * (1.0 - gj)).astype(BF16)

    @pl.when(pl.program_id(2) == 0)
    def _():
        kcs_ref[...] = _head_norm(kc_ref[...].astype(F32), k_gain, gmat).astype(BF16)
        vcs_ref[:, :hw] = vc_ref[...]
        vcs_ref[:, hw:] = jnp.ones((vc_ref.shape[0], hw), BF16)
        if has_latent:
            kl = _head_norm(kl_ref[...].astype(F32), k_gain, gmat)
            kls_ref[...] = _rope(kl, cosk_ref[...], sink_ref[...], first_half).astype(BF16)
            vls_ref[:, :hw] = vl_ref[...]
            vls_ref[:, hw:] = jnp.ones((vl_ref.shape[0], hw), BF16)

    q = _head_norm(q_ref[...].astype(F32), q_gain, gmat)
    if rope_q:
        q = _rope(q, cosq_ref[...], sinq_ref[...], first_half)
    q = q * (DIFF_HEAD_DIM ** -0.5 * LOG2E)
    qq = jnp.concatenate([jnp.where(sub1, q, 0.0), jnp.where(sub1, 0.0, q)], axis=0).astype(BF16)

    segs = [(kcs_ref, vcs_ref)] + ([(kls_ref, vls_ref)] if has_latent else [])
    pvs = []
    for r0 in range(0, 2 * tq, row_group):
        qg = qq[r0:r0 + row_group]
        scores = [_dot_nt(qg, k_ref[...]) for k_ref, _ in segs]
        mx = functools.reduce(jnp.maximum, [jnp.max(s, axis=-1, keepdims=True) for s in scores])
        acc = None
        for s, (_, v_ref) in zip(scores, segs):
            part = _dot(jnp.exp2(s - mx).astype(BF16), v_ref[...])
            acc = part if acc is None else acc + part
        pvs.append(acc[:, :hw] / acc[:, hw:hw + 1])
    pv = jnp.concatenate(pvs, axis=0)

    lam_p = lam_ref[...]
    lam = (jnp.exp(jnp.sum(lam_p[0:1] * lam_p[1:2], axis=-1, keepdims=True))
           - jnp.exp(jnp.sum(lam_p[2:3] * lam_p[3:4], axis=-1, keepdims=True)) + lambda_init)
    o = pv[:tq] - lam * pv[tq:]

    ms = jnp.mean(o * o, axis=-1, keepdims=True)
    y = o * lax.rsqrt(ms + EPS) * sub_gain * (1.0 - lambda_init)
    o_ref[...] = (y * _silu(dg_ref[...].astype(F32))).astype(o_ref.dtype)


def _attn(zq3, cq, cdg, zc3, ckc, cvc, zl3, ckl, cvl, rope_tabs, gains3, lam4, l, lambda_init, tq):
    nb, lq, _ = zq3.shape
    lc = zc3.shape[1]
    has_latent = zl3 is not None
    hw = 2 * DIFF_HEAD_DIM
    b_of = lambda c: c // hw
    qmap = lambda off: (lambda b, h, i: (b, i, b_of(off) + h))
    kmap = lambda off: (lambda b, h, i: (b, 0, b_of(off) + h))
    const = lambda b, h, i: (0, 0)
    in_specs = [pl.BlockSpec((None, tq, hw), qmap(cq)),
                pl.BlockSpec((None, tq, hw), qmap(cdg)),
                pl.BlockSpec((None, lc, hw), kmap(ckc)),
                pl.BlockSpec((None, lc, hw), kmap(cvc))]
    args = [zq3, zq3, zc3, zc3]
    scratch = [pltpu.VMEM((lc, hw), BF16), pltpu.VMEM((lc, 2 * hw), BF16)]
    if has_latent:
        ll = zl3.shape[1]
        cos_t, sin_s = rope_tabs
        in_specs += [pl.BlockSpec((None, ll, hw), kmap(ckl)),
                     pl.BlockSpec((None, ll, hw), kmap(cvl)),
                     pl.BlockSpec((tq, hw), lambda b, h, i: (i, 0)),
                     pl.BlockSpec((tq, hw), lambda b, h, i: (i, 0)),
                     pl.BlockSpec((ll, hw), const),
                     pl.BlockSpec((ll, hw), const)]
        args += [zl3, zl3, cos_t, sin_s, cos_t, sin_s]
        scratch += [pltpu.VMEM((ll, hw), BF16), pltpu.VMEM((ll, 2 * hw), BF16)]
    in_specs += [pl.BlockSpec((None, 3, hw), lambda b, h, i: (l, 0, 0)),
                 pl.BlockSpec((None, 4, DIFF_HEAD_DIM), lambda b, h, i: (l, 0, 0))]
    args += [gains3, lam4]
    kern = functools.partial(_attn_kernel, tq=tq, row_group=min(2 * tq, LANES), rope_q=has_latent, has_latent=has_latent,
                             lambda_init=lambda_init)
    return pl.pallas_call(
        kern,
        grid=(nb, DIFF_HEADS, lq // tq),
        in_specs=in_specs,
        out_specs=pl.BlockSpec((None, tq, hw), lambda b, h, i: (b, i, h)),
        out_shape=jax.ShapeDtypeStruct((nb, lq, DIFF_HEADS * hw), BF16),
        scratch_shapes=scratch,
        compiler_params=_cparams("arbitrary", "arbitrary", "arbitrary"),
        name="diff_attn",
    )(*args)


def _log_sigmoid(x):
    return jnp.minimum(x, 0.0) - jnp.log1p(jnp.exp(-jnp.abs(x)))


GLA_LEVELS = tuple(1 << i for i in range(GLA_CHUNK.bit_length() - 1))
GLA_TAB_LEVELS = tuple(m for m in GLA_LEVELS if m < 8)
GLA_ROW_LEVELS = tuple(m for m in GLA_LEVELS if m >= 8)
GLA_SLOT_CUM = len(GLA_TAB_LEVELS)
GLA_UNROLL = 2
GLA_TOT_ROWS = 16


def _gla_boundary(m, rev):
    return m if rev else m - 1


def _gla_tables():
    c = GLA_CHUNK
    t = np.arange(c)[:, None]
    j = np.arange(c)[None, :]
    sums, masks = [], []
    for rev in (False, True):
        rows, mk = [], []
        for m in GLA_LEVELS:
            base = t & ~(2 * m - 1)
            bd = base + _gla_boundary(m, rev)
            is_q = ((t & m) == 0) if rev else ((t & m) != 0)
            if rev:
                between = np.where(is_q, (j >= t) & (j < bd), (j >= bd) & (j < t))
            else:
                between = np.where(is_q, (j > bd) & (j <= t), (j > t) & (j <= bd))
            if m in GLA_TAB_LEVELS:
                rows.append(between)
            mk.append((base == (j & ~(2 * m - 1))) & is_q & (((j & m) != 0) if rev else ((j & m) == 0)))
        rows.append(j >= t if rev else j <= t)
        rows.append(np.ones((GLA_TOT_ROWS, c), bool))
        mk.append(t == j)
        sums.append(np.concatenate(rows, axis=0))
        masks.append(np.stack(mk))
    return (jnp.asarray(np.stack(sums), BF16), jnp.asarray(np.stack(masks), F32))


def _gla_chunk(q, k, v, ghl, st, tab_ref, mask_ref, rev, need_out):
    c = GLA_CHUNK
    first = 0 if need_out else GLA_SLOT_CUM
    x2 = _dot(tab_ref[first * c:, :], ghl)
    x = x2[:, :GLA_DK] + x2[:, GLA_DK:]
    cum = x[(GLA_SLOT_CUM - first) * c:(GLA_SLOT_CUM - first + 1) * c]
    tot = x[-GLA_TOT_ROWS:1 - GLA_TOT_ROWS]
    kd = (k * jnp.exp2(tot - cum)).astype(BF16)
    st_new = st * jnp.exp2(tot) + _dot_tn(v, kd)
    if not need_out:
        return None, st_new

    out = _dot_nt((q * jnp.exp2(cum)).astype(BF16), st.astype(BF16))
    att = _dot_nt(q.astype(BF16), k.astype(BF16)) * mask_ref[len(GLA_LEVELS)]
    for lvl, m in enumerate(GLA_LEVELS):
        if m in GLA_TAB_LEVELS:
            xl = x[lvl * c:(lvl + 1) * c]
        else:
            anchor = jnp.concatenate(
                [jnp.broadcast_to(cum[b0 + _gla_boundary(m, rev):b0 + _gla_boundary(m, rev) + 1], (2 * m, GLA_DK))
                 for b0 in range(0, c, 2 * m)], axis=0)
            xl = -jnp.abs(cum - anchor)
        el = jnp.exp2(xl)
        att = att + _dot_nt((q * el).astype(BF16), (k * el).astype(BF16)) * mask_ref[lvl]
    return out + _dot(att.astype(BF16), v), st_new


def _gla_kernel(*refs, n_ctx, n_lat, ctx_out):
    (qc_ref, kc_ref, vc_ref, ggc_ref, lrc_ref, ql_ref, kl_ref, vl_ref, ggl_ref, lrl_ref,
     w2_ref, b2_ref, gain_ref, tab_ref, mask_ref) = refs[:15]
    rest = refs[15:]
    if ctx_out:
        ol_ref, oc_ref, gl_ref, gc_ref, accl_ref, accc_ref, stf_ref, stb_ref = rest
    else:
        ol_ref, gl_ref, gc_ref, accl_ref, stf_ref, stb_ref = rest
        oc_ref = accc_ref = None

    def decays(lr_ref, g_ref):
        g = _log_sigmoid(_dot(lr_ref[...], w2_ref[...]) + b2_ref[...]) * (LOG2E / GLA_NORMALIZER)
        hi = g.astype(BF16)
        lo = (g - hi.astype(F32)).astype(BF16)
        for d in range(2):
            g_ref[:, (2 * d) * GLA_DK:(2 * d + 1) * GLA_DK] = hi[:, d * GLA_DK:(d + 1) * GLA_DK]
            g_ref[:, (2 * d + 1) * GLA_DK:(2 * d + 2) * GLA_DK] = lo[:, d * GLA_DK:(d + 1) * GLA_DK]

    decays(lrl_ref, gl_ref)
    decays(lrc_ref, gc_ref)
    stf_ref[...] = jnp.zeros_like(stf_ref)
    stb_ref[...] = jnp.zeros_like(stb_ref)
    scale = GLA_DK ** -0.5

    def scan(n, q_ref, k_ref, v_ref, g_ref, acc_ref):
        need_out = acc_ref is not None

        def step(i, assign_up, assign_down):
            for c_idx, d, st_ref, first_visit in ((i, 0, stf_ref, assign_up), (n - 1 - i, 1, stb_ref, assign_down)):
                rows = pl.ds(pl.multiple_of(c_idx * GLA_CHUNK, GLA_CHUNK), GLA_CHUNK)
                q = q_ref[rows, :].astype(F32) * scale
                k = k_ref[rows, :].astype(F32)
                ghl = g_ref[rows, 2 * d * GLA_DK:(2 * d + 2) * GLA_DK]
                out, st_new = _gla_chunk(q, k, v_ref[rows, :], ghl, st_ref[...], tab_ref.at[d], mask_ref.at[d],
                                         d == 1, need_out)
                st_ref[...] = st_new
                if need_out:
                    if first_visit:
                        acc_ref[rows, :] = out
                    else:
                        acc_ref[rows, :] += out

        def first(i, carry):
            step(i, True, True)
            return carry

        def second(i, carry):
            step(i, False, False)
            return carry

        if n // 2:
            lax.fori_loop(0, n // 2, first, 0, unroll=min(GLA_UNROLL, n // 2))
        if n % 2:
            step(n // 2, True, False)
        if n // 2:
            lax.fori_loop((n + 1) // 2, n, second, 0, unroll=min(GLA_UNROLL, n // 2))

    def finish(acc_ref, gg_ref, o_ref):
        o = acc_ref[...]
        ms = jnp.mean(o * o, axis=-1, keepdims=True)
        y = o * lax.rsqrt(ms + EPS) * gain_ref[...]
        o_ref[...] = (y * _silu(gg_ref[...].astype(F32))).astype(o_ref.dtype)

    scan(n_ctx, qc_ref, kc_ref, vc_ref, gc_ref, accc_ref)
    scan(n_lat, ql_ref, kl_ref, vl_ref, gl_ref, accl_ref)
    finish(accl_ref, ggl_ref, ol_ref)
    if ctx_out:
        finish(accc_ref, ggc_ref, oc_ref)


def _gla(zc3, cc, lrc3, zl3, cl, lrl3, w2, b2, gain3, l, ctx_out):
    nb, lc, _ = zc3.shape
    ll = zl3.shape[1]
    assert lc % GLA_CHUNK == 0 and ll % GLA_CHUNK == 0
    tabs, masks = _gla_tables()

    def seq_specs(z3, cols, lr3, slen):
        qo, ko, vo, go = cols
        return ([pl.BlockSpec((None, slen, GLA_DK), lambda b, h: (b, 0, qo // GLA_DK + h)),
                 pl.BlockSpec((None, slen, GLA_DK), lambda b, h: (b, 0, ko // GLA_DK + h)),
                 pl.BlockSpec((None, slen, GLA_DV), lambda b, h: (b, 0, vo // GLA_DV + h)),
                 pl.BlockSpec((None, slen, GLA_DV), lambda b, h: (b, 0, (0 if go is None else go) // GLA_DV + h)),
                 pl.BlockSpec((None, slen, LANES), lambda b, h: (b, 0, 0))],
                [z3, z3, z3, z3, lr3])

    sc, ac = seq_specs(zc3, cc, lrc3, lc)
    sl, al = seq_specs(zl3, cl, lrl3, ll)
    in_specs = sc + sl + [pl.BlockSpec((None, None, LANES, 2 * GLA_DK), lambda b, h: (l, h, 0, 0)),
                          pl.BlockSpec((None, None, 1, 2 * GLA_DK), lambda b, h: (l, h, 0, 0)),
                          pl.BlockSpec((None, 1, GLA_DV), lambda b, h: (l, 0, 0)),
                          pl.BlockSpec(tabs.shape, lambda b, h: (0, 0, 0)),
                          pl.BlockSpec(masks.shape, lambda b, h: (0, 0, 0, 0))]
    args = ac + al + [w2, b2, gain3, tabs, masks]
    out_specs = [pl.BlockSpec((None, ll, GLA_DV), lambda b, h: (b, 0, h))]
    out_shape = [jax.ShapeDtypeStruct((nb, ll, GLA_HEADS * GLA_DV), BF16)]
    scratch = [pltpu.VMEM((ll, 4 * GLA_DK), BF16), pltpu.VMEM((lc, 4 * GLA_DK), BF16),
               pltpu.VMEM((ll, GLA_DV), F32)]
    if ctx_out:
        out_specs.append(pl.BlockSpec((None, lc, GLA_DV), lambda b, h: (b, 0, h)))
        out_shape.append(jax.ShapeDtypeStruct((nb, lc, GLA_HEADS * GLA_DV), BF16))
        scratch.append(pltpu.VMEM((lc, GLA_DV), F32))
    scratch += [pltpu.VMEM((GLA_DV, GLA_DK), F32), pltpu.VMEM((GLA_DV, GLA_DK), F32)]
    kern = functools.partial(_gla_kernel, n_ctx=lc // GLA_CHUNK, n_lat=ll // GLA_CHUNK, ctx_out=ctx_out)
    outs = pl.pallas_call(
        kern,
        grid=(nb, GLA_HEADS),
        in_specs=in_specs,
        out_specs=out_specs,
        out_shape=out_shape,
        scratch_shapes=scratch,
        compiler_params=_cparams("arbitrary", "arbitrary"),
        name="gla",
    )(*args)
    return (outs[0], outs[1]) if ctx_out else (outs[0], None)


def _merge_kernel(p_ref, d_ref, g_ref, wp_ref, wd_ref, wg_ref, mp_ref, md_ref, mg_ref, y_ref):
    y = (_sigmoid(mp_ref[...].astype(F32)) * _dot(p_ref[...], wp_ref[...])
         + _sigmoid(md_ref[...].astype(F32)) * _dot(d_ref[...], wd_ref[...])
         + _sigmoid(mg_ref[...].astype(F32)) * _dot(g_ref[...], wg_ref[...]))
    y_ref[...] = y.astype(y_ref.dtype)


def _merge(pool_o, diff_o, gla_o, z2d, col_mg, wbp, wbd, wbg, l, tm):
    m, kw = pool_o.shape
    d = wbp.shape[2]
    act = pl.BlockSpec((tm, kw), lambda i: (i, 0))
    wsp = pl.BlockSpec((None, kw, d), lambda i: (l, 0, 0), pipeline_mode=pl.Buffered(1))
    gate = lambda k: pl.BlockSpec((tm, d), lambda i: (i, col_mg // d + k))
    return pl.pallas_call(
        _merge_kernel,
        grid=(m // tm,),
        in_specs=[act, act, act, wsp, wsp, wsp, gate(0), gate(1), gate(2)],
        out_specs=pl.BlockSpec((tm, d), lambda i: (i, 0)),
        out_shape=jax.ShapeDtypeStruct((m, d), BF16),
        compiler_params=_cparams("arbitrary"),
        name="merge",
    )(pool_o, diff_o, gla_o, wbp, wbd, wbg, z2d, z2d, z2d)


def _outproj_kernel(y_ref, w_ref, x_ref, mod_ref, o_ref):
    o_ref[...] = x_ref[...] + mod_ref[2:3, :] * _dot(y_ref[...], w_ref[...])


def _outproj(y, w_out, x2d, mod3, l, row_of_tile, tm):
    m, d = x2d.shape
    return pl.pallas_call(
        _outproj_kernel,
        grid=(m // tm,),
        in_specs=[pl.BlockSpec((tm, d), lambda i: (i, 0)),
                  pl.BlockSpec((None, d, d), lambda i: (l, 0, 0), pipeline_mode=pl.Buffered(1)),
                  pl.BlockSpec((tm, d), lambda i: (i, 0)),
                  pl.BlockSpec((None, 3, d), lambda i: (row_of_tile(i), 0, 0))],
        out_specs=pl.BlockSpec((tm, d), lambda i: (i, 0)),
        out_shape=jax.ShapeDtypeStruct((m, d), F32),
        compiler_params=_cparams("arbitrary"),
        name="outproj",
    )(y, w_out, x2d, mod3)


def _rope_tables(seq_len):
    n_freq = DIFF_HEAD_DIM // 4
    t = jnp.arange(seq_len)
    inv = ROPE_THETA ** (-jnp.arange(n_freq, dtype=F32) / n_freq)
    ang = jnp.concatenate([(t // GRID_W).astype(F32)[:, None] * inv,
                           (t % GRID_W).astype(F32)[:, None] * inv], axis=-1)
    cos, sin = jnp.cos(ang), jnp.sin(ang)
    return jnp.tile(cos, (1, 4)), jnp.tile(jnp.concatenate([-sin, sin], axis=-1), (1, 2))


def _largest_tile(n, cap, mult):
    t = min(n, cap)
    while n % t or t % mult:
        t -= mult
    return t


def kernel(x, c, ctx, c_ctx, norm_g, w_ada, b_ada, w_in, pool_w, pool_scale, diff_q_norm, diff_k_norm, diff_lam_q1, diff_lam_k1, diff_lam_q2, diff_lam_k2, diff_subln, gla_w_gate_f, gla_b_gate_f, gla_w_gate_b, gla_b_gate_b, gla_norm, w_branch_pool, w_branch_diff, w_branch_gla, w_out):
    nb, seq, d = x.shape
    lc = ctx.shape[1]
    depth = w_in.shape[0]
    pw = pool_scale.shape[1]
    dw = DIFF_HEADS * 2 * DIFF_HEAD_DIM
    gkw, gvw = GLA_HEADS * GLA_DK, GLA_HEADS * GLA_DV

    sizes = dict(pu=pw, pg=pw, dq=dw, dk=dw, dv=dw, dg=dw, gq=gkw, gk=gkw, gv=gvw, gg=gvw)
    wcol, off = {}, 0
    for name, size in sizes.items():
        wcol[name] = off
        off += size
    lr_col = off
    zcol = {name: 3 * d + o for name, o in wcol.items()}
    zcol["mg"] = 0

    rope_tabs = _rope_tables(seq)
    cc = jnp.zeros((8, d), F32).at[:nb].set(c).at[nb].set(c_ctx)
    x2d = x.reshape(nb * seq, d)
    ctx2d = ctx.reshape(nb * lc, d)

    w_in_t = jnp.swapaxes(w_in, 1, 2)
    wbp, wbd, wbg, wo = (w.astype(BF16) for w in (w_branch_pool, w_branch_diff, w_branch_gla, w_out))
    norm_g3 = norm_g.reshape(depth, 1, d)
    b_ada3 = b_ada.reshape(depth, 1, 3 * d)
    pool_scale3 = pool_scale.reshape(depth, 1, pw)
    gains3 = jnp.stack([jnp.tile(diff_q_norm, (1, 2)), jnp.tile(diff_k_norm, (1, 2)), diff_subln], axis=1)
    lam4 = jnp.stack([diff_lam_q1, diff_lam_k1, diff_lam_q2, diff_lam_k2], axis=1)
    gla_gain3 = gla_norm.reshape(depth, 1, GLA_DV)
    per_head = lambda w: w.reshape(depth, GLA_RANK, GLA_HEADS, GLA_DK).transpose(0, 2, 1, 3)
    w2 = jnp.zeros((depth, GLA_HEADS, LANES, 2 * GLA_DK), F32)
    w2 = w2.at[:, :, :GLA_RANK, :GLA_DK].set(per_head(gla_w_gate_f))
    w2 = w2.at[:, :, GLA_RANK:2 * GLA_RANK, GLA_DK:].set(per_head(gla_w_gate_b)).astype(BF16)
    b2 = jnp.concatenate([gla_b_gate_f.reshape(depth, GLA_HEADS, 1, GLA_DK),
                          gla_b_gate_b.reshape(depth, GLA_HEADS, 1, GLA_DK)], axis=-1)

    rows_gcd = math.gcd(seq, nb * lc)
    tm_pre = _largest_tile(rows_gcd, 512, 16)
    tm_in = _largest_tile(rows_gcd, 1024, 16)
    tn_in = _largest_tile(math.gcd(lr_col, 3 * d), 1024, 256)
    row_x = lambda tm: (lambda i: (i * tm) // seq)
    row_c = lambda i: nb
    kv_cols = ((zcol["dk"], zcol["dg"]), (zcol["gk"], zcol["gg"]))
    kv_tiles = tuple(t for t in range((3 * d + lr_col) // tn_in)
                     if any(lo < (t + 1) * tn_in and t * tn_in < hi for lo, hi in kv_cols))
    ccol = zcol

    for l in range(depth):
        last = l == depth - 1
        lambda_init = 0.8 - 0.6 * math.exp(-0.3 * l)
        mod3 = _ada(cc, w_ada, b_ada3, l).reshape(8, 3, d)
        h, lr, lrc = _prenorm(x2d, ctx2d, mod3, norm_g3, w_in_t, l, lr_col, seq, nb, tm_pre)
        z, zc = _inproj(h, w_in_t, l, lr_col, 3 * d, nb * seq, tm_in, tn_in, kv_tiles if last else None)
        z3, zc3 = z.reshape(nb, seq, -1), zc.reshape(nb, lc, -1)
        lr3, lrc3 = lr.reshape(nb, seq, LANES), lrc.reshape(nb, lc, LANES)

        pool_l = _pool(z3, zcol["pu"], zcol["pg"], pool_w, pool_scale3, l, _largest_tile(seq, 512, 16))
        diff_l = _attn(z3, zcol["dq"], zcol["dg"], zc3, ccol["dk"], ccol["dv"], z3, zcol["dk"], zcol["dv"],
                       rope_tabs, gains3, lam4, l, lambda_init, _largest_tile(seq, 1024, 16))
        gla_l, gla_c = _gla(zc3, (ccol["gq"], ccol["gk"], ccol["gv"], ccol["gg"]), lrc3,
                            z3, (zcol["gq"], zcol["gk"], zcol["gv"], zcol["gg"]), lr3,
                            w2, b2, gla_gain3, l, ctx_out=not last)

        tm_m = _largest_tile(seq, 512, 16)
        y = _merge(pool_l.reshape(nb * seq, pw), diff_l.reshape(nb * seq, dw), gla_l.reshape(nb * seq, gvw),
                   z, zcol["mg"], wbp, wbd, wbg, l, tm_m)
        x2d_new = _outproj(y, wo, x2d, mod3, l, row_x(tm_m), tm_m)

        if not last:
            pool_c = _pool(zc3, zcol["pu"], zcol["pg"], pool_w, pool_scale3, l, _largest_tile(lc, 512, 16))
            diff_c = _attn(zc3, zcol["dq"], zcol["dg"], zc3, zcol["dk"], zcol["dv"], None, None, None,
                           None, gains3, lam4, l, lambda_init, _largest_tile(lc, 256, 16))
            tm_mc = _largest_tile(nb * lc, 512, 16)
            y_c = _merge(pool_c.reshape(nb * lc, pw), diff_c.reshape(nb * lc, dw), gla_c.reshape(nb * lc, gvw),
                         zc, zcol["mg"], wbp, wbd, wbg, l, tm_mc)
            ctx2d = _outproj(y_c, wo, ctx2d, mod3, l, row_c, tm_mc)
        x2d = x2d_new

    return x2d.reshape(nb, seq, d)
```

```python
import functools
import math

import jax
import jax.numpy as jnp
import numpy as np
from jax import lax
from jax.experimental import pallas as pl
from jax.experimental.pallas import tpu as pltpu

F32 = jnp.float32
BF16 = jnp.bfloat16

EPS = 1e-6
GRID_W = 64
ROPE_THETA = 10000.0

POOL_WINDOWS = (2, 4, 8, 16)
POOL_HALO = 16
DIFF_HEADS = 8
DIFF_HEAD_DIM = 64
GLA_HEADS = 4
GLA_DK = 128
GLA_DV = 256
GLA_RANK = 16
GLA_NORMALIZER = 16.0
GLA_CHUNK = 256
LANES = 128
LOG2E = math.log2(math.e)

VMEM_LIMIT = 48 * 1024 * 1024


def _cparams(*sem):
    return pltpu.CompilerParams(dimension_semantics=sem, vmem_limit_bytes=VMEM_LIMIT)


def _sigmoid(x):
    return 1.0 / (1.0 + jnp.exp(-x))


def _silu(x):
    return x * _sigmoid(x)


def _dot(a, b):
    return jnp.dot(a, b, preferred_element_type=F32)


def _dot_nt(a, b):
    return lax.dot_general(a, b, (((1,), (1,)), ((), ())), preferred_element_type=F32)


def _dot_tn(a, b):
    return lax.dot_general(a, b, (((0,), (0,)), ((), ())), preferred_element_type=F32)


def _ada_kernel(cc_ref, w_ref, b_ref, o_ref):
    @pl.when(pl.program_id(0) == 0)
    def _():
        o_ref[...] = jnp.broadcast_to(b_ref[...], o_ref.shape)

    a = _silu(cc_ref[...]).astype(BF16)
    o_ref[...] += _dot(a, w_ref[...].astype(BF16))


def _ada(cc, w_ada, b_ada3, l, tk=256):
    rows, d = cc.shape
    n = w_ada.shape[2]
    return pl.pallas_call(
        _ada_kernel,
        grid=(d // tk,),
        in_specs=[pl.BlockSpec((rows, tk), lambda k: (0, k)),
                  pl.BlockSpec((None, tk, n), lambda k: (l, k, 0)),
                  pl.BlockSpec((None, 1, n), lambda k: (l, 0, 0))],
        out_specs=pl.BlockSpec((rows, n), lambda k: (0, 0)),
        out_shape=jax.ShapeDtypeStruct((rows, n), F32),
        compiler_params=_cparams("arbitrary"),
        name="ada",
    )(cc, w_ada, b_ada3)


def _prenorm_kernel(x_ref, c_ref, mod_ref, g_ref, wlr_ref, h_ref, lrx_ref, lrc_ref, *, n_x):
    def emit(src_ref, lr_ref):
        x = src_ref[...]
        ms = jnp.mean(x * x, axis=-1, keepdims=True)
        y = x * lax.rsqrt(ms + EPS) * g_ref[...]
        h = (y * (1.0 + mod_ref[1:2, :]) + mod_ref[0:1, :]).astype(BF16)
        h_ref[...] = h
        lr_ref[...] = _dot_nt(h, wlr_ref[...].astype(BF16)).astype(lr_ref.dtype)

    @pl.when(pl.program_id(0) < n_x)
    def _():
        emit(x_ref, lrx_ref)

    @pl.when(pl.program_id(0) >= n_x)
    def _():
        emit(c_ref, lrc_ref)


def _prenorm(x2d, ctx2d, mod3, norm_g3, w_in_t, l, lr_col, seq, nb, tm):
    mx, d = x2d.shape
    mc = ctx2d.shape[0]
    n_x, n_c = mx // tm, mc // tm
    return pl.pallas_call(
        functools.partial(_prenorm_kernel, n_x=n_x),
        grid=(n_x + n_c,),
        in_specs=[pl.BlockSpec((tm, d), lambda i: (jnp.minimum(i, n_x - 1), 0)),
                  pl.BlockSpec((tm, d), lambda i: (jnp.maximum(i - n_x, 0), 0)),
                  pl.BlockSpec((None, 3, d), lambda i: (jnp.where(i < n_x, (i * tm) // seq, nb), 0, 0)),
                  pl.BlockSpec((None, 1, d), lambda i: (l, 0, 0)),
                  pl.BlockSpec((None, LANES, d), lambda i: (l, lr_col // LANES, 0))],
        out_specs=[pl.BlockSpec((tm, d), lambda i: (i, 0)),
                   pl.BlockSpec((tm, LANES), lambda i: (jnp.minimum(i, n_x - 1), 0)),
                   pl.BlockSpec((tm, LANES), lambda i: (jnp.maximum(i - n_x, 0), 0))],
        out_shape=[jax.ShapeDtypeStruct((mx + mc, d), BF16),
                   jax.ShapeDtypeStruct((mx, LANES), BF16),
                   jax.ShapeDtypeStruct((mc, LANES), BF16)],
        compiler_params=_cparams("arbitrary"),
        name="prenorm",
    )(x2d, ctx2d, mod3, norm_g3, w_in_t)


def _inproj_kernel(h_ref, wa_ref, wb_ref, zx_ref, zc_ref, wbf_ref, *, n_x, n_shift, shift, ctx_tiles):
    j, i = pl.program_id(0), pl.program_id(1)

    @pl.when(i == 0)
    def _():
        @pl.when(j < n_shift)
        def _():
            wbf_ref[...] = jnp.concatenate([wa_ref[shift:, :], wb_ref[...]], axis=0).astype(BF16)

        @pl.when(j >= n_shift)
        def _():
            wbf_ref[...] = wa_ref[...].astype(BF16)

    @pl.when(i < n_x)
    def _():
        zx_ref[...] = _dot_nt(h_ref[...], wbf_ref[...]).astype(zx_ref.dtype)

    @pl.when(i >= n_x)
    def _():
        if ctx_tiles is None:
            zc_ref[...] = _dot_nt(h_ref[...], wbf_ref[...]).astype(zc_ref.dtype)
        else:
            needed = functools.reduce(jnp.logical_or, [j == t for t in ctx_tiles])

            @pl.when(needed)
            def _():
                zc_ref[...] = _dot_nt(h_ref[...], wbf_ref[...]).astype(zc_ref.dtype)

            @pl.when(jnp.logical_not(needed))
            def _():
                zc_ref[...] = jnp.zeros_like(zc_ref)


def _inproj(h, w_in_t, l, lr_col, n_gate, mx, tm, tn, ctx_tiles):
    m, d = h.shape
    n_x = mx // tm
    n_shift = n_gate // tn
    shift = w_in_t.shape[1] - lr_col - n_gate
    assert shift % 8 == 0 and lr_col % tn == 0 and tn % shift == 0
    first_gate = lr_col // tn
    return pl.pallas_call(
        functools.partial(_inproj_kernel, n_x=n_x, n_shift=n_shift, shift=shift, ctx_tiles=ctx_tiles),
        grid=(n_shift + first_gate, m // tm),
        in_specs=[pl.BlockSpec((tm, d), lambda j, i: (i, 0)),
                  pl.BlockSpec((None, tn, d), lambda j, i: (l, jnp.where(j < n_shift, first_gate + j, j - n_shift), 0)),
                  pl.BlockSpec((None, shift, d),
                               lambda j, i: (l, jnp.where(j < n_shift, (lr_col + (j + 1) * tn) // shift, 0), 0))],
        out_specs=[pl.BlockSpec((tm, tn), lambda j, i: (jnp.minimum(i, n_x - 1), j)),
                   pl.BlockSpec((tm, tn), lambda j, i: (jnp.maximum(i - n_x, 0), j))],
        out_shape=[jax.ShapeDtypeStruct((mx, n_gate + lr_col), BF16),
                   jax.ShapeDtypeStruct((m - mx, n_gate + lr_col), BF16)],
        scratch_shapes=[pltpu.VMEM((tn, d), BF16)],
        compiler_params=_cparams("arbitrary", "arbitrary"),
        name="inproj",
    )(h, w_in_t, w_in_t)


def _pool_kernel(up_ref, uc_ref, un_ref, pg_ref, w_ref, sc_ref, o_ref, *, tile, seq_len):
    base = pl.program_id(1) * tile
    gw = w_ref.shape[-1]
    width = tile + 2 * POOL_HALO
    t = base + lax.broadcasted_iota(jnp.int32, (tile, width), 0)
    s = base - POOL_HALO + lax.broadcasted_iota(jnp.int32, (tile, width), 1)
    tc = base + lax.broadcasted_iota(jnp.int32, (tile, 1), 0)
    for g, window in enumerate(POOL_WINDOWS):
        half = window // 2
        cols = slice(g * gw, (g + 1) * gw)
        u_all = jnp.concatenate([up_ref[:, cols], uc_ref[:, cols], un_ref[:, cols]], axis=0)
        inside = (s >= jnp.maximum(t - half, 0)) & (s < jnp.minimum(t + half, seq_len))
        band = jnp.where(inside, 1.0, 0.0).astype(BF16)
        wsum = _dot(band, u_all)
        cnt = (jnp.minimum(tc + half, seq_len) - jnp.maximum(tc - half, 0)).astype(F32)
        dcen = wsum / cnt - uc_ref[:, cols].astype(F32)
        y = _dot(dcen.astype(BF16), w_ref[g].astype(BF16)) * sc_ref[:, cols]
        o_ref[:, cols] = (y * _silu(pg_ref[:, cols].astype(F32))).astype(o_ref.dtype)


def _pool(z3, col_u, col_g, pool_w, pool_scale3, l, tile):
    nseq, seq_len, _ = z3.shape
    ngrp = len(POOL_WINDOWS)
    gw = pool_w.shape[-1]
    pw = ngrp * gw
    cu, cg = col_u // pw, col_g // pw
    nhalo = seq_len // POOL_HALO
    per = tile // POOL_HALO
    kern = functools.partial(_pool_kernel, tile=tile, seq_len=seq_len)
    return pl.pallas_call(
        kern,
        grid=(nseq, seq_len // tile),
        in_specs=[
            pl.BlockSpec((None, POOL_HALO, pw), lambda s, r: (s, jnp.maximum(r * per - 1, 0), cu)),
            pl.BlockSpec((None, tile, pw), lambda s, r: (s, r, cu)),
            pl.BlockSpec((None, POOL_HALO, pw), lambda s, r: (s, jnp.minimum((r + 1) * per, nhalo - 1), cu)),
            pl.BlockSpec((None, tile, pw), lambda s, r: (s, r, cg)),
            pl.BlockSpec((None, ngrp, gw, gw), lambda s, r: (l, 0, 0, 0)),
            pl.BlockSpec((None, 1, pw), lambda s, r: (l, 0, 0)),
        ],
        out_specs=pl.BlockSpec((None, tile, pw), lambda s, r: (s, r, 0)),
        out_shape=jax.ShapeDtypeStruct((nseq, seq_len, pw), BF16),
        compiler_params=_cparams("arbitrary", "arbitrary"),
        name="pool",
    )(z3, z3, z3, z3, pool_w, pool_scale3)


def _head_norm(t, gain, gmat):
    sq = t * t
    hi = sq.astype(BF16)
    lo = (sq - hi.astype(F32)).astype(BF16)
    ssq = _dot(hi, gmat) + _dot(lo, gmat)
    return t * lax.rsqrt(ssq * (1.0 / DIFF_HEAD_DIM) + EPS) * gain


def _rope(t, cos_t, sin_s, first_half):
    half = DIFF_HEAD_DIM // 2
    lower = pltpu.roll(t, half, 1)
    upper = pltpu.roll(t, LANES - half, 1)
    return t * cos_t + jnp.where(first_half, upper, lower) * sin_s


def _attn_kernel(*refs, tq, row_group, rope_q, has_latent, lambda_init):
    if has_latent:
        (q_ref, dg_ref, kc_ref, vc_ref, kl_ref, vl_ref, cosq_ref, sinq_ref, cosk_ref, sink_ref,
         gains_ref, lam_ref, o_ref, kcs_ref, vcs_ref, kls_ref, vls_ref) = refs
    else:
        (q_ref, dg_ref, kc_ref, vc_ref, gains_ref, lam_ref, o_ref, kcs_ref, vcs_ref) = refs
    hw = 2 * DIFF_HEAD_DIM
    q_gain, k_gain, sub_gain = gains_ref[0:1, :], gains_ref[1:2, :], gains_ref[2:3, :]

    lane = lax.broadcasted_iota(jnp.int32, (1, LANES), 1)
    first_half = (lane & (DIFF_HEAD_DIM - 1)) < (DIFF_HEAD_DIM // 2)
    sub1 = lane < DIFF_HEAD_DIM
    gi = jnp.where(lax.broadcasted_iota(jnp.int32, (LANES, LANES), 0) < DIFF_HEAD_DIM, 1.0, 0.0)
    gj = jnp.where(lax.broadcasted_iota(jnp.int32, (LANES, LANES), 1) < DIFF_HEAD_DIM, 1.0, 0.0)
    gmat = (gi * gj + (1.0 - gi) * (1.0 - gj)).astype(BF16)

    @pl.when(pl.program_id(2) == 0)
    def _():
        kcs_ref[...] = _head_norm(kc_ref[...].astype(F32), k_gain, gmat).astype(BF16)
        vcs_ref[:, :hw] = vc_ref[...]
        vcs_ref[:, hw:] = jnp.ones((vc_ref.shape[0], hw), BF16)
        if has_latent:
            kl = _head_norm(kl_ref[...].astype(F32), k_gain, gmat)
            kls_ref[...] = _rope(kl, cosk_ref[...], sink_ref[...], first_half).astype(BF16)
            vls_ref[:, :hw] = vl_ref[...]
            vls_ref[:, hw:] = jnp.ones((vl_ref.shape[0], hw), BF16)

    q = _head_norm(q_ref[...].astype(F32), q_gain, gmat)
    if rope_q:
        q = _rope(q, cosq_ref[...], sinq_ref[...], first_half)
    q = q * (DIFF_HEAD_DIM ** -0.5 * LOG2E)
    qq = jnp.concatenate([jnp.where(sub1, q, 0.0), jnp.where(sub1, 0.0, q)], axis=0).astype(BF16)

    segs = [(kcs_ref, vcs_ref)] + ([(kls_ref, vls_ref)] if has_latent else [])
    pvs = []
    for r0 in range(0, 2 * tq, row_group):
        qg = qq[r0:r0 + row_group]
        scores = [_dot_nt(qg, k_ref[...]) for k_ref, _ in segs]
        mx = functools.reduce(jnp.maximum, [jnp.max(s, axis=-1, keepdims=True) for s in scores])
        acc = None
        for s, (_, v_ref) in zip(scores, segs):
            part = _dot(jnp.exp2(s - mx).astype(BF16), v_ref[...])
            acc = part if acc is None else acc + part
        pvs.append(acc[:, :hw] / acc[:, hw:hw + 1])
    pv = jnp.concatenate(pvs, axis=0)

    lam_p = lam_ref[...]
    lam = (jnp.exp(jnp.sum(lam_p[0:1] * lam_p[1:2], axis=-1, keepdims=True))
           - jnp.exp(jnp.sum(lam_p[2:3] * lam_p[3:4], axis=-1, keepdims=True)) + lambda_init)
    o = pv[:tq] - lam * pv[tq:]

    ms = jnp.mean(o * o, axis=-1, keepdims=True)
    y = o * lax.rsqrt(ms + EPS) * sub_gain * (1.0 - lambda_init)
    o_ref[...] = (y * _silu(dg_ref[...].astype(F32))).astype(o_ref.dtype)


def _attn(zq3, cq, cdg, zc3, ckc, cvc, zl3, ckl, cvl, rope_tabs, gains3, lam4, l, lambda_init, tq):
    nb, lq, _ = zq3.shape
    lc = zc3.shape[1]
    has_latent = zl3 is not None
    hw = 2 * DIFF_HEAD_DIM
    b_of = lambda c: c // hw
    qmap = lambda off: (lambda b, h, i: (b, i, b_of(off) + h))
    kmap = lambda off: (lambda b, h, i: (b, 0, b_of(off) + h))
    const = lambda b, h, i: (0, 0)
    in_specs = [pl.BlockSpec((None, tq, hw), qmap(cq)),
                pl.BlockSpec((None, tq, hw), qmap(cdg)),
                pl.BlockSpec((None, lc, hw), kmap(ckc)),
                pl.BlockSpec((None, lc, hw), kmap(cvc))]
    args = [zq3, zq3, zc3, zc3]
    scratch = [pltpu.VMEM((lc, hw), BF16), pltpu.VMEM((lc, 2 * hw), BF16)]
    if has_latent:
        ll = zl3.shape[1]
        cos_t, sin_s = rope_tabs
        in_specs += [pl.BlockSpec((None, ll, hw), kmap(ckl)),
                     pl.BlockSpec((None, ll, hw), kmap(cvl)),
                     pl.BlockSpec((tq, hw), lambda b, h, i: (i, 0)),
                     pl.BlockSpec((tq, hw), lambda b, h, i: (i, 0)),
                     pl.BlockSpec((ll, hw), const),
                     pl.BlockSpec((ll, hw), const)]
        args += [zl3, zl3, cos_t, sin_s, cos_t, sin_s]
        scratch += [pltpu.VMEM((ll, hw), BF16), pltpu.VMEM((ll, 2 * hw), BF16)]
    in_specs += [pl.BlockSpec((None, 3, hw), lambda b, h, i: (l, 0, 0)),
                 pl.BlockSpec((None, 4, DIFF_HEAD_DIM), lambda b, h, i: (l, 0, 0))]
    args += [gains3, lam4]
    kern = functools.partial(_attn_kernel, tq=tq, row_group=min(2 * tq, LANES), rope_q=has_latent, has_latent=has_latent,
                             lambda_init=lambda_init)
    return pl.pallas_call(
        kern,
        grid=(nb, DIFF_HEADS, lq // tq),
        in_specs=in_specs,
        out_specs=pl.BlockSpec((None, tq, hw), lambda b, h, i: (b, i, h)),
        out_shape=jax.ShapeDtypeStruct((nb, lq, DIFF_HEADS * hw), BF16),
        scratch_shapes=scratch,
        compiler_params=_cparams("arbitrary", "arbitrary", "arbitrary"),
        name="diff_attn",
    )(*args)


def _log_sigmoid(x):
    return jnp.minimum(x, 0.0) - jnp.log1p(jnp.exp(-jnp.abs(x)))


GLA_LEVELS = tuple(1 << i for i in range(GLA_CHUNK.bit_length() - 1))
GLA_TAB_LEVELS = tuple(m for m in GLA_LEVELS if m < 8)
GLA_ROW_LEVELS = tuple(m for m in GLA_LEVELS if m >= 8)
GLA_SLOT_CUM = len(GLA_TAB_LEVELS)
GLA_UNROLL = 4
GLA_TOT_ROWS = 16


def _gla_boundary(m, rev):
    return m if rev else m - 1


def _gla_tables():
    c = GLA_CHUNK
    t = np.arange(c)[:, None]
    j = np.arange(c)[None, :]
    sums, masks = [], []
    for rev in (False, True):
        rows, mk = [], []
        for m in GLA_LEVELS:
            base = t & ~(2 * m - 1)
            bd = base + _gla_boundary(m, rev)
            is_q = ((t & m) == 0) if rev else ((t & m) != 0)
            if rev:
                between = np.where(is_q, (j >= t) & (j < bd), (j >= bd) & (j < t))
            else:
                between = np.where(is_q, (j > bd) & (j <= t), (j > t) & (j <= bd))
            if m in GLA_TAB_LEVELS:
                rows.append(between)
            mk.append((base == (j & ~(2 * m - 1))) & is_q & (((j & m) != 0) if rev else ((j & m) == 0)))
        rows.append(j >= t if rev else j <= t)
        rows.append(np.ones((GLA_TOT_ROWS, c), bool))
        mk.append(t == j)
        sums.append(np.concatenate(rows, axis=0))
        masks.append(np.stack(mk))
    return (jnp.asarray(np.stack(sums), BF16), jnp.asarray(np.stack(masks), F32))


def _gla_chunk(q, k, v, ghl, st, tab_ref, mask_ref, rev, need_out):
    c = GLA_CHUNK
    first = 0 if need_out else GLA_SLOT_CUM
    x2 = _dot(tab_ref[first * c:, :], ghl)
    x = x2[:, :GLA_DK] + x2[:, GLA_DK:]
    cum = x[(GLA_SLOT_CUM - first) * c:(GLA_SLOT_CUM - first + 1) * c]
    tot = x[-GLA_TOT_ROWS:1 - GLA_TOT_ROWS]
    kd = (k * jnp.exp2(tot - cum)).astype(BF16)
    st_new = st * jnp.exp2(tot) + _dot_tn(v, kd)
    if not need_out:
        return None, st_new

    out = _dot_nt((q * jnp.exp2(cum)).astype(BF16), st.astype(BF16))
    att = _dot_nt(q.astype(BF16), k.astype(BF16)) * mask_ref[len(GLA_LEVELS)]
    for lvl, m in enumerate(GLA_LEVELS):
        if m in GLA_TAB_LEVELS:
            xl = x[lvl * c:(lvl + 1) * c]
        else:
            anchor = jnp.concatenate(
                [jnp.broadcast_to(cum[b0 + _gla_boundary(m, rev):b0 + _gla_boundary(m, rev) + 1], (2 * m, GLA_DK))
                 for b0 in range(0, c, 2 * m)], axis=0)
            xl = -jnp.abs(cum - anchor)
        el = jnp.exp2(xl)
        att = att + _dot_nt((q * el).astype(BF16), (k * el).astype(BF16)) * mask_ref[lvl]
    return out + _dot(att.astype(BF16), v), st_new


def _gla_kernel(*refs, n_ctx, n_lat, ctx_out):
    (qc_ref, kc_ref, vc_ref, ggc_ref, lrc_ref, ql_ref, kl_ref, vl_ref, ggl_ref, lrl_ref,
     w2_ref, b2_ref, gain_ref, tab_ref, mask_ref) = refs[:15]
    rest = refs[15:]
    if ctx_out:
        ol_ref, oc_ref, gl_ref, gc_ref, accl_ref, accc_ref, stf_ref, stb_ref = rest
    else:
        ol_ref, gl_ref, gc_ref, accl_ref, stf_ref, stb_ref = rest
        oc_ref = accc_ref = None

    def decays(lr_ref, g_ref):
        g = _log_sigmoid(_dot(lr_ref[...], w2_ref[...]) + b2_ref[...]) * (LOG2E / GLA_NORMALIZER)
        hi = g.astype(BF16)
        lo = (g - hi.astype(F32)).astype(BF16)
        for d in range(2):
            g_ref[:, (2 * d) * GLA_DK:(2 * d + 1) * GLA_DK] = hi[:, d * GLA_DK:(d + 1) * GLA_DK]
            g_ref[:, (2 * d + 1) * GLA_DK:(2 * d + 2) * GLA_DK] = lo[:, d * GLA_DK:(d + 1) * GLA_DK]

    decays(lrl_ref, gl_ref)
    decays(lrc_ref, gc_ref)
    stf_ref[...] = jnp.zeros_like(stf_ref)
    stb_ref[...] = jnp.zeros_like(stb_ref)
    scale = GLA_DK ** -0.5

    def scan(n, q_ref, k_ref, v_ref, g_ref, acc_ref):
        need_out = acc_ref is not None

        def step(i, assign_up, assign_down):
            for c_idx, d, st_ref, first_visit in ((i, 0, stf_ref, assign_up), (n - 1 - i, 1, stb_ref, assign_down)):
                rows = pl.ds(pl.multiple_of(c_idx * GLA_CHUNK, GLA_CHUNK), GLA_CHUNK)
                q = q_ref[rows, :].astype(F32) * scale
                k = k_ref[rows, :].astype(F32)
                ghl = g_ref[rows, 2 * d * GLA_DK:(2 * d + 2) * GLA_DK]
                out, st_new = _gla_chunk(q, k, v_ref[rows, :], ghl, st_ref[...], tab_ref.at[d], mask_ref.at[d],
                                         d == 1, need_out)
                st_ref[...] = st_new
                if need_out:
                    if first_visit:
                        acc_ref[rows, :] = out
                    else:
                        acc_ref[rows, :] += out

        def first(i, carry):
            step(i, True, True)
            return carry

        def second(i, carry):
            step(i, False, False)
            return carry

        if n // 2:
            lax.fori_loop(0, n // 2, first, 0, unroll=min(GLA_UNROLL, n // 2))
        if n % 2:
            step(n // 2, True, False)
        if n // 2:
            lax.fori_loop((n + 1) // 2, n, second, 0, unroll=min(GLA_UNROLL, n // 2))

    def finish(acc_ref, gg_ref, o_ref):
        o = acc_ref[...]
        ms = jnp.mean(o * o, axis=-1, keepdims=True)
        y = o * lax.rsqrt(ms + EPS) * gain_ref[...]
        o_ref[...] = (y * _silu(gg_ref[...].astype(F32))).astype(o_ref.dtype)

    scan(n_ctx, qc_ref, kc_ref, vc_ref, gc_ref, accc_ref)
    scan(n_lat, ql_ref, kl_ref, vl_ref, gl_ref, accl_ref)
    finish(accl_ref, ggl_ref, ol_ref)
    if ctx_out:
        finish(accc_ref, ggc_ref, oc_ref)


def _gla(zc3, cc, lrc3, zl3, cl, lrl3, w2, b2, gain3, l, ctx_out):
    nb, lc, _ = zc3.shape
    ll = zl3.shape[1]
    assert lc % GLA_CHUNK == 0 and ll % GLA_CHUNK == 0
    tabs, masks = _gla_tables()

    def seq_specs(z3, cols, lr3, slen):
        qo, ko, vo, go = cols
        return ([pl.BlockSpec((None, slen, GLA_DK), lambda b, h: (b, 0, qo // GLA_DK + h)),
                 pl.BlockSpec((None, slen, GLA_DK), lambda b, h: (b, 0, ko // GLA_DK + h)),
                 pl.BlockSpec((None, slen, GLA_DV), lambda b, h: (b, 0, vo // GLA_DV + h)),
                 pl.BlockSpec((None, slen, GLA_DV), lambda b, h: (b, 0, (0 if go is None else go) // GLA_DV + h)),
                 pl.BlockSpec((None, slen, LANES), lambda b, h: (b, 0, 0))],
                [z3, z3, z3, z3, lr3])

    sc, ac = seq_specs(zc3, cc, lrc3, lc)
    sl, al = seq_specs(zl3, cl, lrl3, ll)
    in_specs = sc + sl + [pl.BlockSpec((None, None, LANES, 2 * GLA_DK), lambda b, h: (l, h, 0, 0)),
                          pl.BlockSpec((None, None, 1, 2 * GLA_DK), lambda b, h: (l, h, 0, 0)),
                          pl.BlockSpec((None, 1, GLA_DV), lambda b, h: (l, 0, 0)),
                          pl.BlockSpec(tabs.shape, lambda b, h: (0, 0, 0)),
                          pl.BlockSpec(masks.shape, lambda b, h: (0, 0, 0, 0))]
    args = ac + al + [w2, b2, gain3, tabs, masks]
    out_specs = [pl.BlockSpec((None, ll, GLA_DV), lambda b, h: (b, 0, h))]
    out_shape = [jax.ShapeDtypeStruct((nb, ll, GLA_HEADS * GLA_DV), BF16)]
    scratch = [pltpu.VMEM((ll, 4 * GLA_DK), BF16), pltpu.VMEM((lc, 4 * GLA_DK), BF16),
               pltpu.VMEM((ll, GLA_DV), F32)]
    if ctx_out:
        out_specs.append(pl.BlockSpec((None, lc, GLA_DV), lambda b, h: (b, 0, h)))
        out_shape.append(jax.ShapeDtypeStruct((nb, lc, GLA_HEADS * GLA_DV), BF16))
        scratch.append(pltpu.VMEM((lc, GLA_DV), F32))
    scratch += [pltpu.VMEM((GLA_DV, GLA_DK), F32), pltpu.VMEM((GLA_DV, GLA_DK), F32)]
    kern = functools.partial(_gla_kernel, n_ctx=lc // GLA_CHUNK, n_lat=ll // GLA_CHUNK, ctx_out=ctx_out)
    outs = pl.pallas_call(
        kern,
        grid=(nb, GLA_HEADS),
        in_specs=in_specs,
        out_specs=out_specs,
        out_shape=out_shape,
        scratch_shapes=scratch,
        compiler_params=_cparams("arbitrary", "arbitrary"),
        name="gla",
    )(*args)
    return (outs[0], outs[1]) if ctx_out else (outs[0], None)


def _merge_kernel(p_ref, d_ref, g_ref, wp_ref, wd_ref, wg_ref, mp_ref, md_ref, mg_ref, y_ref):
    y = (_sigmoid(mp_ref[...].astype(F32)) * _dot(p_ref[...], wp_ref[...])
         + _sigmoid(md_ref[...].astype(F32)) * _dot(d_ref[...], wd_ref[...])
         + _sigmoid(mg_ref[...].astype(F32)) * _dot(g_ref[...], wg_ref[...]))
    y_ref[...] = y.astype(y_ref.dtype)


def _merge(pool_o, diff_o, gla_o, z2d, col_mg, wbp, wbd, wbg, l, tm):
    m, kw = pool_o.shape
    d = wbp.shape[2]
    act = pl.BlockSpec((tm, kw), lambda i: (i, 0))
    wsp = pl.BlockSpec((None, kw, d), lambda i: (l, 0, 0), pipeline_mode=pl.Buffered(1))
    gate = lambda k: pl.BlockSpec((tm, d), lambda i: (i, col_mg // d + k))
    return pl.pallas_call(
        _merge_kernel,
        grid=(m // tm,),
        in_specs=[act, act, act, wsp, wsp, wsp, gate(0), gate(1), gate(2)],
        out_specs=pl.BlockSpec((tm, d), lambda i: (i, 0)),
        out_shape=jax.ShapeDtypeStruct((m, d), BF16),
        compiler_params=_cparams("arbitrary"),
        name="merge",
    )(pool_o, diff_o, gla_o, wbp, wbd, wbg, z2d, z2d, z2d)


def _outproj_kernel(y_ref, w_ref, x_ref, mod_ref, o_ref):
    o_ref[...] = x_ref[...] + mod_ref[2:3, :] * _dot(y_ref[...], w_ref[...])


def _outproj(y, w_out, x2d, mod3, l, row_of_tile, tm):
    m, d = x2d.shape
    return pl.pallas_call(
        _outproj_kernel,
        grid=(m // tm,),
        in_specs=[pl.BlockSpec((tm, d), lambda i: (i, 0)),
                  pl.BlockSpec((None, d, d), lambda i: (l, 0, 0), pipeline_mode=pl.Buffered(1)),
                  pl.BlockSpec((tm, d), lambda i: (i, 0)),
                  pl.BlockSpec((None, 3, d), lambda i: (row_of_tile(i), 0, 0))],
        out_specs=pl.BlockSpec((tm, d), lambda i: (i, 0)),
        out_shape=jax.ShapeDtypeStruct((m, d), F32),
        compiler_params=_cparams("arbitrary"),
        name="outproj",
    )(y, w_out, x2d, mod3)


def _rope_tables(seq_len):
    n_freq = DIFF_HEAD_DIM // 4
    t = jnp.arange(seq_len)
    inv = ROPE_THETA ** (-jnp.arange(n_freq, dtype=F32) / n_freq)
    ang = jnp.concatenate([(t // GRID_W).astype(F32)[:, None] * inv,
                           (t % GRID_W).astype(F32)[:, None] * inv], axis=-1)
    cos, sin = jnp.cos(ang), jnp.sin(ang)
    return jnp.tile(cos, (1, 4)), jnp.tile(jnp.concatenate([-sin, sin], axis=-1), (1, 2))


def _largest_tile(n, cap, mult):
    t = min(n, cap)
    while n % t or t % mult:
        t -= mult
    return t


def kernel(x, c, ctx, c_ctx, norm_g, w_ada, b_ada, w_in, pool_w, pool_scale, diff_q_norm, diff_k_norm, diff_lam_q1, diff_lam_k1, diff_lam_q2, diff_lam_k2, diff_subln, gla_w_gate_f, gla_b_gate_f, gla_w_gate_b, gla_b_gate_b, gla_norm, w_branch_pool, w_branch_diff, w_branch_gla, w_out):
    nb, seq, d = x.shape
    lc = ctx.shape[1]
    depth = w_in.shape[0]
    pw = pool_scale.shape[1]
    dw = DIFF_HEADS * 2 * DIFF_HEAD_DIM
    gkw, gvw = GLA_HEADS * GLA_DK, GLA_HEADS * GLA_DV

    sizes = dict(pu=pw, pg=pw, dq=dw, dk=dw, dv=dw, dg=dw, gq=gkw, gk=gkw, gv=gvw, gg=gvw)
    wcol, off = {}, 0
    for name, size in sizes.items():
        wcol[name] = off
        off += size
    lr_col = off
    zcol = {name: 3 * d + o for name, o in wcol.items()}
    zcol["mg"] = 0

    rope_tabs = _rope_tables(seq)
    cc = jnp.zeros((8, d), F32).at[:nb].set(c).at[nb].set(c_ctx)
    x2d = x.reshape(nb * seq, d)
    ctx2d = ctx.reshape(nb * lc, d)

    w_in_t = jnp.swapaxes(w_in, 1, 2)
    wbp, wbd, wbg, wo = (w.astype(BF16) for w in (w_branch_pool, w_branch_diff, w_branch_gla, w_out))
    norm_g3 = norm_g.reshape(depth, 1, d)
    b_ada3 = b_ada.reshape(depth, 1, 3 * d)
    pool_scale3 = pool_scale.reshape(depth, 1, pw)
    gains3 = jnp.stack([jnp.tile(diff_q_norm, (1, 2)), jnp.tile(diff_k_norm, (1, 2)), diff_subln], axis=1)
    lam4 = jnp.stack([diff_lam_q1, diff_lam_k1, diff_lam_q2, diff_lam_k2], axis=1)
    gla_gain3 = gla_norm.reshape(depth, 1, GLA_DV)
    per_head = lambda w: w.reshape(depth, GLA_RANK, GLA_HEADS, GLA_DK).transpose(0, 2, 1, 3)
    w2 = jnp.zeros((depth, GLA_HEADS, LANES, 2 * GLA_DK), F32)
    w2 = w2.at[:, :, :GLA_RANK, :GLA_DK].set(per_head(gla_w_gate_f))
    w2 = w2.at[:, :, GLA_RANK:2 * GLA_RANK, GLA_DK:].set(per_head(gla_w_gate_b)).astype(BF16)
    b2 = jnp.concatenate([gla_b_gate_f.reshape(depth, GLA_HEADS, 1, GLA_DK),
                          gla_b_gate_b.reshape(depth, GLA_HEADS, 1, GLA_DK)], axis=-1)

    rows_gcd = math.gcd(seq, nb * lc)
    tm_pre = _largest_tile(rows_gcd, 512, 16)
    tm_in = _largest_tile(rows_gcd, 1024, 16)
    tn_in = _largest_tile(math.gcd(lr_col, 3 * d), 1024, 256)
    row_x = lambda tm: (lambda i: (i * tm) // seq)
    row_c = lambda i: nb
    kv_cols = ((zcol["dk"], zcol["dg"]), (zcol["gk"], zcol["gg"]))
    kv_tiles = tuple(t for t in range((3 * d + lr_col) // tn_in)
                     if any(lo < (t + 1) * tn_in and t * tn_in < hi for lo, hi in kv_cols))
    ccol = zcol

    for l in range(depth):
        last = l == depth - 1
        lambda_init = 0.8 - 0.6 * math.exp(-0.3 * l)
        mod3 = _ada(cc, w_ada, b_ada3, l).reshape(8, 3, d)
        h, lr, lrc = _prenorm(x2d, ctx2d, mod3, norm_g3, w_in_t, l, lr_col, seq, nb, tm_pre)
        z, zc = _inproj(h, w_in_t, l, lr_col, 3 * d, nb * seq, tm_in, tn_in, kv_tiles if last else None)
        z3, zc3 = z.reshape(nb, seq, -1), zc.reshape(nb, lc, -1)
        lr3, lrc3 = lr.reshape(nb, seq, LANES), lrc.reshape(nb, lc, LANES)

        pool_l = _pool(z3, zcol["pu"], zcol["pg"], pool_w, pool_scale3, l, _largest_tile(seq, 512, 16))
        diff_l = _attn(z3, zcol["dq"], zcol["dg"], zc3, ccol["dk"], ccol["dv"], z3, zcol["dk"], zcol["dv"],
                       rope_tabs, gains3, lam4, l, lambda_init, _largest_tile(seq, 2048, 16))
        gla_l, gla_c = _gla(zc3, (ccol["gq"], ccol["gk"], ccol["gv"], ccol["gg"]), lrc3,
                            z3, (zcol["gq"], zcol["gk"], zcol["gv"], zcol["gg"]), lr3,
                            w2, b2, gla_gain3, l, ctx_out=not last)

        tm_m = _largest_tile(seq, 512, 16)
        y = _merge(pool_l.reshape(nb * seq, pw), diff_l.reshape(nb * seq, dw), gla_l.reshape(nb * seq, gvw),
                   z, zcol["mg"], wbp, wbd, wbg, l, tm_m)
        x2d_new = _outproj(y, wo, x2d, mod3, l, row_x(tm_m), tm_m)

        if not last:
            pool_c = _pool(zc3, zcol["pu"], zcol["pg"], pool_w, pool_scale3, l, _largest_tile(lc, 512, 16))
            diff_c = _attn(zc3, zcol["dq"], zcol["dg"], zc3, zcol["dk"], zcol["dv"], None, None, None,
                           None, gains3, lam4, l, lambda_init, _largest_tile(lc, 256, 16))
            tm_mc = _largest_tile(nb * lc, 512, 16)
            y_c = _merge(pool_c.reshape(nb * lc, pw), diff_c.reshape(nb * lc, dw), gla_c.reshape(nb * lc, gvw),
                         zc, zcol["mg"], wbp, wbd, wbg, l, tm_mc)
            ctx2d = _outproj(y_c, wo, ctx2d, mod3, l, row_c, tm_mc)
        x2d = x2d_new

    return x2d.reshape(nb, seq, d)
```

```python
import functools
import math

import jax
import jax.numpy as jnp
import numpy as np
from jax import lax
from jax.experimental import pallas as pl
from jax.experimental.pallas import tpu as pltpu

F32 = jnp.float32
BF16 = jnp.bfloat16

EPS = 1e-6
GRID_W = 64
ROPE_THETA = 10000.0

POOL_WINDOWS = (2, 4, 8, 16)
POOL_HALO = 16
DIFF_HEADS = 8
DIFF_HEAD_DIM = 64
GLA_HEADS = 4
GLA_DK = 128
GLA_DV = 256
GLA_RANK = 16
GLA_NORMALIZER = 16.0
GLA_CHUNK = 256
LANES = 128
LOG2E = math.log2(math.e)

VMEM_LIMIT = 48 * 1024 * 1024


def _cparams(*sem):
    return pltpu.CompilerParams(dimension_semantics=sem, vmem_limit_bytes=VMEM_LIMIT)


def _sigmoid(x):
    return 0.5 * jnp.tanh(0.5 * x) + 0.5


def _silu(x):
    return x * _sigmoid(x)


def _dot(a, b):
    return jnp.dot(a, b, preferred_element_type=F32)


def _dot_nt(a, b):
    return lax.dot_general(a, b, (((1,), (1,)), ((), ())), preferred_element_type=F32)


def _dot_tn(a, b):
    return lax.dot_general(a, b, (((0,), (0,)), ((), ())), preferred_element_type=F32)


def _ada_kernel(cc_ref, w_ref, b_ref, o_ref):
    @pl.when(pl.program_id(0) == 0)
    def _():
        o_ref[...] = jnp.broadcast_to(b_ref[...], o_ref.shape)

    a = _silu(cc_ref[...]).astype(BF16)
    o_ref[...] += _dot(a, w_ref[...].astype(BF16))


def _ada(cc, w_ada, b_ada3, l, tk=256):
    rows, d = cc.shape
    n = w_ada.shape[2]
    return pl.pallas_call(
        _ada_kernel,
        grid=(d // tk,),
        in_specs=[pl.BlockSpec((rows, tk), lambda k: (0, k)),
                  pl.BlockSpec((None, tk, n), lambda k: (l, k, 0)),
                  pl.BlockSpec((None, 1, n), lambda k: (l, 0, 0))],
        out_specs=pl.BlockSpec((rows, n), lambda k: (0, 0)),
        out_shape=jax.ShapeDtypeStruct((rows, n), F32),
        compiler_params=_cparams("arbitrary"),
        name="ada",
    )(cc, w_ada, b_ada3)


def _prenorm_kernel(x_ref, c_ref, mod_ref, g_ref, wlr_ref, h_ref, lrx_ref, lrc_ref, *, n_x):
    def emit(src_ref, lr_ref):
        x = src_ref[...]
        ms = jnp.mean(x * x, axis=-1, keepdims=True)
        y = x * lax.rsqrt(ms + EPS) * g_ref[...]
        h = (y * (1.0 + mod_ref[1:2, :]) + mod_ref[0:1, :]).astype(BF16)
        h_ref[...] = h
        lr_ref[...] = _dot_nt(h, wlr_ref[...].astype(BF16)).astype(lr_ref.dtype)

    @pl.when(pl.program_id(0) < n_x)
    def _():
        emit(x_ref, lrx_ref)

    @pl.when(pl.program_id(0) >= n_x)
    def _():
        emit(c_ref, lrc_ref)


def _prenorm(x2d, ctx2d, mod3, norm_g3, w_in_t, l, lr_col, seq, nb, tm):
    mx, d = x2d.shape
    mc = ctx2d.shape[0]
    n_x, n_c = mx // tm, mc // tm
    return pl.pallas_call(
        functools.partial(_prenorm_kernel, n_x=n_x),
        grid=(n_x + n_c,),
        in_specs=[pl.BlockSpec((tm, d), lambda i: (jnp.minimum(i, n_x - 1), 0)),
                  pl.BlockSpec((tm, d), lambda i: (jnp.maximum(i - n_x, 0), 0)),
                  pl.BlockSpec((None, 3, d), lambda i: (jnp.where(i < n_x, (i * tm) // seq, nb), 0, 0)),
                  pl.BlockSpec((None, 1, d), lambda i: (l, 0, 0)),
                  pl.BlockSpec((None, LANES, d), lambda i: (l, lr_col // LANES, 0))],
        out_specs=[pl.BlockSpec((tm, d), lambda i: (i, 0)),
                   pl.BlockSpec((tm, LANES), lambda i: (jnp.minimum(i, n_x - 1), 0)),
                   pl.BlockSpec((tm, LANES), lambda i: (jnp.maximum(i - n_x, 0), 0))],
        out_shape=[jax.ShapeDtypeStruct((mx + mc, d), BF16),
                   jax.ShapeDtypeStruct((mx, LANES), BF16),
                   jax.ShapeDtypeStruct((mc, LANES), BF16)],
        compiler_params=_cparams("arbitrary"),
        name="prenorm",
    )(x2d, ctx2d, mod3, norm_g3, w_in_t)


def _inproj_kernel(h_ref, wa_ref, wb_ref, zx_ref, zc_ref, wbf_ref, *, n_x, n_shift, shift, ctx_tiles):
    j, i = pl.program_id(0), pl.program_id(1)

    @pl.when(i == 0)
    def _():
        @pl.when(j < n_shift)
        def _():
            wbf_ref[...] = jnp.concatenate([wa_ref[shift:, :], wb_ref[...]], axis=0).astype(BF16)

        @pl.when(j >= n_shift)
        def _():
            wbf_ref[...] = wa_ref[...].astype(BF16)

    @pl.when(i < n_x)
    def _():
        zx_ref[...] = _dot_nt(h_ref[...], wbf_ref[...]).astype(zx_ref.dtype)

    @pl.when(i >= n_x)
    def _():
        if ctx_tiles is None:
            zc_ref[...] = _dot_nt(h_ref[...], wbf_ref[...]).astype(zc_ref.dtype)
        else:
            needed = functools.reduce(jnp.logical_or, [j == t for t in ctx_tiles])

            @pl.when(needed)
            def _():
                zc_ref[...] = _dot_nt(h_ref[...], wbf_ref[...]).astype(zc_ref.dtype)

            @pl.when(jnp.logical_not(needed))
            def _():
                zc_ref[...] = jnp.zeros_like(zc_ref)


def _inproj(h, w_in_t, l, lr_col, n_gate, mx, tm, tn, ctx_tiles):
    m, d = h.shape
    n_x = mx // tm
    n_shift = n_gate // tn
    shift = w_in_t.shape[1] - lr_col - n_gate
    assert shift % 8 == 0 and lr_col % tn == 0 and tn % shift == 0
    first_gate = lr_col // tn
    return pl.pallas_call(
        functools.partial(_inproj_kernel, n_x=n_x, n_shift=n_shift, shift=shift, ctx_tiles=ctx_tiles),
        grid=(n_shift + first_gate, m // tm),
        in_specs=[pl.BlockSpec((tm, d), lambda j, i: (i, 0)),
                  pl.BlockSpec((None, tn, d), lambda j, i: (l, jnp.where(j < n_shift, first_gate + j, j - n_shift), 0)),
                  pl.BlockSpec((None, shift, d),
                               lambda j, i: (l, jnp.where(j < n_shift, (lr_col + (j + 1) * tn) // shift, 0), 0))],
        out_specs=[pl.BlockSpec((tm, tn), lambda j, i: (jnp.minimum(i, n_x - 1), j)),
                   pl.BlockSpec((tm, tn), lambda j, i: (jnp.maximum(i - n_x, 0), j))],
        out_shape=[jax.ShapeDtypeStruct((mx, n_gate + lr_col), BF16),
                   jax.ShapeDtypeStruct((m - mx, n_gate + lr_col), BF16)],
        scratch_shapes=[pltpu.VMEM((tn, d), BF16)],
        compiler_params=_cparams("arbitrary", "arbitrary"),
        name="inproj",
    )(h, w_in_t, w_in_t)


def _pool_kernel(up_ref, uc_ref, un_ref, pg_ref, w_ref, sc_ref, o_ref, *, tile, seq_len):
    base = pl.program_id(1) * tile
    gw = w_ref.shape[-1]
    width = tile + 2 * POOL_HALO
    t = base + lax.broadcasted_iota(jnp.int32, (tile, width), 0)
    s = base - POOL_HALO + lax.broadcasted_iota(jnp.int32, (tile, width), 1)
    tc = base + lax.broadcasted_iota(jnp.int32, (tile, 1), 0)
    for g, window in enumerate(POOL_WINDOWS):
        half = window // 2
        cols = slice(g * gw, (g + 1) * gw)
        u_all = jnp.concatenate([up_ref[:, cols], uc_ref[:, cols], un_ref[:, cols]], axis=0)
        inside = (s >= jnp.maximum(t - half, 0)) & (s < jnp.minimum(t + half, seq_len))
        band = jnp.where(inside, 1.0, 0.0).astype(BF16)
        wsum = _dot(band, u_all)
        cnt = (jnp.minimum(tc + half, seq_len) - jnp.maximum(tc - half, 0)).astype(F32)
        dcen = wsum / cnt - uc_ref[:, cols].astype(F32)
        y = _dot(dcen.astype(BF16), w_ref[g].astype(BF16)) * sc_ref[:, cols]
        o_ref[:, cols] = (y * _silu(pg_ref[:, cols].astype(F32))).astype(o_ref.dtype)


def _pool(z3, col_u, col_g, pool_w, pool_scale3, l, tile):
    nseq, seq_len, _ = z3.shape
    ngrp = len(POOL_WINDOWS)
    gw = pool_w.shape[-1]
    pw = ngrp * gw
    cu, cg = col_u // pw, col_g // pw
    nhalo = seq_len // POOL_HALO
    per = tile // POOL_HALO
    kern = functools.partial(_pool_kernel, tile=tile, seq_len=seq_len)
    return pl.pallas_call(
        kern,
        grid=(nseq, seq_len // tile),
        in_specs=[
            pl.BlockSpec((None, POOL_HALO, pw), lambda s, r: (s, jnp.maximum(r * per - 1, 0), cu)),
            pl.BlockSpec((None, tile, pw), lambda s, r: (s, r, cu)),
            pl.BlockSpec((None, POOL_HALO, pw), lambda s, r: (s, jnp.minimum((r + 1) * per, nhalo - 1), cu)),
            pl.BlockSpec((None, tile, pw), lambda s, r: (s, r, cg)),
            pl.BlockSpec((None, ngrp, gw, gw), lambda s, r: (l, 0, 0, 0)),
            pl.BlockSpec((None, 1, pw), lambda s, r: (l, 0, 0)),
        ],
        out_specs=pl.BlockSpec((None, tile, pw), lambda s, r: (s, r, 0)),
        out_shape=jax.ShapeDtypeStruct((nseq, seq_len, pw), BF16),
        compiler_params=_cparams("arbitrary", "arbitrary"),
        name="pool",
    )(z3, z3, z3, z3, pool_w, pool_scale3)


def _head_norm(t, gain, gmat):
    sq = t * t
    hi = sq.astype(BF16)
    lo = (sq - hi.astype(F32)).astype(BF16)
    ssq = _dot(hi, gmat) + _dot(lo, gmat)
    return t * lax.rsqrt(ssq * (1.0 / DIFF_HEAD_DIM) + EPS) * gain


def _rope(t, cos_t, sin_s, first_half):
    half = DIFF_HEAD_DIM // 2
    lower = pltpu.roll(t, half, 1)
    upper = pltpu.roll(t, LANES - half, 1)
    return t * cos_t + jnp.where(first_half, upper, lower) * sin_s


def _attn_kernel(*refs, tq, row_group, rope_q, has_latent, lambda_init):
    if has_latent:
        (q_ref, dg_ref, kc_ref, vc_ref, kl_ref, vl_ref, cosq_ref, sinq_ref, cosk_ref, sink_ref,
         gains_ref, lam_ref, o_ref, ks_ref, vs_ref) = refs
    else:
        (q_ref, dg_ref, kc_ref, vc_ref, gains_ref, lam_ref, o_ref, ks_ref, vs_ref) = refs
    hw = 2 * DIFF_HEAD_DIM
    lc = kc_ref.shape[0]
    q_gain, k_gain, sub_gain = gains_ref[0:1, :], gains_ref[1:2, :], gains_ref[2:3, :]

    lane = lax.broadcasted_iota(jnp.int32, (1, LANES), 1)
    first_half = (lane & (DIFF_HEAD_DIM - 1)) < (DIFF_HEAD_DIM // 2)
    sub1 = lane < DIFF_HEAD_DIM
    gi = jnp.where(lax.broadcasted_iota(jnp.int32, (LANES, LANES), 0) < DIFF_HEAD_DIM, 1.0, 0.0)
    gj = jnp.where(lax.broadcasted_iota(jnp.int32, (LANES, LANES), 1) < DIFF_HEAD_DIM, 1.0, 0.0)
    gmat = (gi * gj + (1.0 - gi) * (1.0 - gj)).astype(BF16)

    @pl.when(pl.program_id(2) == 0)
    def _():
        ks_ref[:lc, :] = _head_norm(kc_ref[...].astype(F32), k_gain, gmat).astype(BF16)
        vs_ref[:lc, :hw] = vc_ref[...]
        vs_ref[:, hw:] = jnp.ones((vs_ref.shape[0], hw), BF16)
        if has_latent:
            kl = _head_norm(kl_ref[...].astype(F32), k_gain, gmat)
            ks_ref[lc:, :] = _rope(kl, cosk_ref[...], sink_ref[...], first_half).astype(BF16)
            vs_ref[lc:, :hw] = vl_ref[...]

    q = _head_norm(q_ref[...].astype(F32), q_gain, gmat)
    if rope_q:
        q = _rope(q, cosq_ref[...], sinq_ref[...], first_half)
    q = q * (DIFF_HEAD_DIM ** -0.5 * LOG2E)
    qq = jnp.concatenate([jnp.where(sub1, q, 0.0), jnp.where(sub1, 0.0, q)], axis=0).astype(BF16)

    pvs = []
    for r0 in range(0, 2 * tq, row_group):
        s = _dot_nt(qq[r0:r0 + row_group], ks_ref[...])
        e = jnp.exp2(s - jnp.max(s, axis=-1, keepdims=True)).astype(BF16)
        acc = _dot(e, vs_ref[...])
        pvs.append(acc[:, :hw] / acc[:, hw:hw + 1])
    pv = jnp.concatenate(pvs, axis=0)

    lam_p = lam_ref[...]
    lam = (jnp.exp(jnp.sum(lam_p[0:1] * lam_p[1:2], axis=-1, keepdims=True))
           - jnp.exp(jnp.sum(lam_p[2:3] * lam_p[3:4], axis=-1, keepdims=True)) + lambda_init)
    o = pv[:tq] - lam * pv[tq:]

    ms = jnp.mean(o * o, axis=-1, keepdims=True)
    y = o * lax.rsqrt(ms + EPS) * sub_gain * (1.0 - lambda_init)
    o_ref[...] = (y * _silu(dg_ref[...].astype(F32))).astype(o_ref.dtype)


def _attn(zq3, cq, cdg, zc3, ckc, cvc, zl3, ckl, cvl, rope_tabs, gains3, lam4, l, lambda_init, tq):
    nb, lq, _ = zq3.shape
    lc = zc3.shape[1]
    has_latent = zl3 is not None
    hw = 2 * DIFF_HEAD_DIM
    b_of = lambda c: c // hw
    qmap = lambda off: (lambda b, h, i: (b, i, b_of(off) + h))
    kmap = lambda off: (lambda b, h, i: (b, 0, b_of(off) + h))
    const = lambda b, h, i: (0, 0)
    in_specs = [pl.BlockSpec((None, tq, hw), qmap(cq)),
                pl.BlockSpec((None, tq, hw), qmap(cdg)),
                pl.BlockSpec((None, lc, hw), kmap(ckc)),
                pl.BlockSpec((None, lc, hw), kmap(cvc))]
    args = [zq3, zq3, zc3, zc3]
    lk = lc + (zl3.shape[1] if has_latent else 0)
    scratch = [pltpu.VMEM((lk, hw), BF16), pltpu.VMEM((lk, 2 * hw), BF16)]
    if has_latent:
        ll = zl3.shape[1]
        cos_t, sin_s = rope_tabs
        in_specs += [pl.BlockSpec((None, ll, hw), kmap(ckl)),
                     pl.BlockSpec((None, ll, hw), kmap(cvl)),
                     pl.BlockSpec((tq, hw), lambda b, h, i: (i, 0)),
                     pl.BlockSpec((tq, hw), lambda b, h, i: (i, 0)),
                     pl.BlockSpec((ll, hw), const),
                     pl.BlockSpec((ll, hw), const)]
        args += [zl3, zl3, cos_t, sin_s, cos_t, sin_s]
    in_specs += [pl.BlockSpec((None, 3, hw), lambda b, h, i: (l, 0, 0)),
                 pl.BlockSpec((None, 4, DIFF_HEAD_DIM), lambda b, h, i: (l, 0, 0))]
    args += [gains3, lam4]
    kern = functools.partial(_attn_kernel, tq=tq, row_group=min(2 * tq, LANES), rope_q=has_latent, has_latent=has_latent,
                             lambda_init=lambda_init)
    return pl.pallas_call(
        kern,
        grid=(nb, DIFF_HEADS, lq // tq),
        in_specs=in_specs,
        out_specs=pl.BlockSpec((None, tq, hw), lambda b, h, i: (b, i, h)),
        out_shape=jax.ShapeDtypeStruct((nb, lq, DIFF_HEADS * hw), BF16),
        scratch_shapes=scratch,
        compiler_params=_cparams("arbitrary", "arbitrary", "arbitrary"),
        name="diff_attn",
    )(*args)


def _log_sigmoid(x):
    return jnp.minimum(x, 0.0) - jnp.log1p(jnp.exp(-jnp.abs(x)))


GLA_LEVELS = tuple(1 << i for i in range(GLA_CHUNK.bit_length() - 1))
GLA_TAB_LEVELS = tuple(m for m in GLA_LEVELS if m < 8)
GLA_ROW_LEVELS = tuple(m for m in GLA_LEVELS if m >= 8)
GLA_SLOT_CUM = len(GLA_TAB_LEVELS)
GLA_ATT_ROWS = 32
GLA_UNROLL = 4
GLA_TOT_ROWS = 16


def _gla_boundary(m, rev):
    return m if rev else m - 1


@functools.lru_cache(maxsize=None)
def _gla_tables_np():
    c = GLA_CHUNK
    t = np.arange(c)[:, None]
    j = np.arange(c)[None, :]
    sums, masks = [], []
    for rev in (False, True):
        rows, mk = [], []
        for m in GLA_LEVELS:
            base = t & ~(2 * m - 1)
            bd = base + _gla_boundary(m, rev)
            is_q = ((t & m) == 0) if rev else ((t & m) != 0)
            if rev:
                between = np.where(is_q, (j >= t) & (j < bd), (j >= bd) & (j < t))
            else:
                between = np.where(is_q, (j > bd) & (j <= t), (j > t) & (j <= bd))
            if m in GLA_TAB_LEVELS:
                rows.append(between)
            mk.append((base == (j & ~(2 * m - 1))) & is_q & (((j & m) != 0) if rev else ((j & m) == 0)))
        rows.append(j >= t if rev else j <= t)
        rows.append(np.ones((GLA_TOT_ROWS, c), bool))
        mk.append(t == j)
        sums.append(np.concatenate(rows, axis=0))
        masks.append(np.stack(mk))
    return np.stack(sums), np.stack(masks)


def _gla_masks_np():
    return _gla_tables_np()[1]


def _gla_tables():
    sums, masks = _gla_tables_np()
    return jnp.asarray(sums, BF16), jnp.asarray(masks, F32)


def _gla_chunk(q, k, v, ghl, st, tab_ref, mask_ref, rev, need_out):
    c = GLA_CHUNK
    first = 0 if need_out else GLA_SLOT_CUM
    x2 = _dot(tab_ref[first * c:, :], ghl)
    x = x2[:, :GLA_DK] + x2[:, GLA_DK:]
    cum = x[(GLA_SLOT_CUM - first) * c:(GLA_SLOT_CUM - first + 1) * c]
    tot = x[-GLA_TOT_ROWS:1 - GLA_TOT_ROWS]
    kd = (k * jnp.exp2(tot - cum)).astype(BF16)
    st_new = st * jnp.exp2(tot) + _dot_tn(v, kd)
    if not need_out:
        return None, st_new

    out = _dot_nt((q * jnp.exp2(cum)).astype(BF16), st.astype(BF16))

    own = _gla_masks_np()[int(rev)]
    n_rb, n_lt = c // GLA_ATT_ROWS, c // LANES
    regions = [[None] * n_lt for _ in range(n_rb)]

    def accumulate(lvl, prod):
        for rb in range(n_rb):
            rs = slice(rb * GLA_ATT_ROWS, (rb + 1) * GLA_ATT_ROWS)
            for lt in range(n_lt):
                ls = slice(lt * LANES, (lt + 1) * LANES)
                if not own[lvl, rs, ls].any():
                    continue
                part = prod[rs, ls]
                if not own[lvl, rs, ls].all():
                    part = part * mask_ref[lvl, rs, ls]
                regions[rb][lt] = part if regions[rb][lt] is None else regions[rb][lt] + part

    accumulate(len(GLA_LEVELS), _dot_nt(q.astype(BF16), k.astype(BF16)))
    for lvl, m in enumerate(GLA_LEVELS):
        if m in GLA_TAB_LEVELS:
            xl = x[lvl * c:(lvl + 1) * c]
        else:
            anchor = jnp.concatenate(
                [jnp.broadcast_to(cum[b0 + _gla_boundary(m, rev):b0 + _gla_boundary(m, rev) + 1], (2 * m, GLA_DK))
                 for b0 in range(0, c, 2 * m)], axis=0)
            xl = -jnp.abs(cum - anchor)
        el = jnp.exp2(xl)
        accumulate(lvl, _dot_nt((q * el).astype(BF16), (k * el).astype(BF16)))
    zero = jnp.zeros((GLA_ATT_ROWS, LANES), F32)
    att = jnp.concatenate([jnp.concatenate([zero if r is None else r for r in row], axis=1) for row in regions], axis=0)
    return out + _dot(att.astype(BF16), v), st_new


def _gla_kernel(*refs, n_ctx, n_lat, ctx_out):
    (qc_ref, kc_ref, vc_ref, ggc_ref, lrc_ref, ql_ref, kl_ref, vl_ref, ggl_ref, lrl_ref,
     w2_ref, b2_ref, gain_ref, tab_ref, mask_ref) = refs[:15]
    rest = refs[15:]
    if ctx_out:
        ol_ref, oc_ref, gl_ref, gc_ref, accl_ref, accc_ref, stf_ref, stb_ref = rest
    else:
        ol_ref, gl_ref, gc_ref, accl_ref, stf_ref, stb_ref = rest
        oc_ref = accc_ref = None

    def decays(lr_ref, g_ref):
        g = _log_sigmoid(_dot(lr_ref[...], w2_ref[...]) + b2_ref[...]) * (LOG2E / GLA_NORMALIZER)
        hi = g.astype(BF16)
        lo = (g - hi.astype(F32)).astype(BF16)
        for d in range(2):
            g_ref[:, (2 * d) * GLA_DK:(2 * d + 1) * GLA_DK] = hi[:, d * GLA_DK:(d + 1) * GLA_DK]
            g_ref[:, (2 * d + 1) * GLA_DK:(2 * d + 2) * GLA_DK] = lo[:, d * GLA_DK:(d + 1) * GLA_DK]

    decays(lrl_ref, gl_ref)
    decays(lrc_ref, gc_ref)
    stf_ref[...] = jnp.zeros_like(stf_ref)
    stb_ref[...] = jnp.zeros_like(stb_ref)
    scale = GLA_DK ** -0.5

    def scan(n, q_ref, k_ref, v_ref, g_ref, acc_ref):
        need_out = acc_ref is not None

        def step(i, assign_up, assign_down):
            for c_idx, d, st_ref, first_visit in ((i, 0, stf_ref, assign_up), (n - 1 - i, 1, stb_ref, assign_down)):
                rows = pl.ds(pl.multiple_of(c_idx * GLA_CHUNK, GLA_CHUNK), GLA_CHUNK)
                q = q_ref[rows, :].astype(F32) * scale
                k = k_ref[rows, :].astype(F32)
                ghl = g_ref[rows, 2 * d * GLA_DK:(2 * d + 2) * GLA_DK]
                out, st_new = _gla_chunk(q, k, v_ref[rows, :], ghl, st_ref[...], tab_ref.at[d], mask_ref.at[d],
                                         d == 1, need_out)
                st_ref[...] = st_new
                if need_out:
                    if first_visit:
                        acc_ref[rows, :] = out
                    else:
                        acc_ref[rows, :] += out

        def first(i, carry):
            step(i, True, True)
            return carry

        def second(i, carry):
            step(i, False, False)
            return carry

        if n // 2:
            lax.fori_loop(0, n // 2, first, 0, unroll=min(GLA_UNROLL, n // 2))
        if n % 2:
            step(n // 2, True, False)
        if n // 2:
            lax.fori_loop((n + 1) // 2, n, second, 0, unroll=min(GLA_UNROLL, n // 2))

    def finish(acc_ref, gg_ref, o_ref):
        o = acc_ref[...]
        ms = jnp.mean(o * o, axis=-1, keepdims=True)
        y = o * lax.rsqrt(ms + EPS) * gain_ref[...]
        o_ref[...] = (y * _silu(gg_ref[...].astype(F32))).astype(o_ref.dtype)

    scan(n_ctx, qc_ref, kc_ref, vc_ref, gc_ref, accc_ref)
    scan(n_lat, ql_ref, kl_ref, vl_ref, gl_ref, accl_ref)
    finish(accl_ref, ggl_ref, ol_ref)
    if ctx_out:
        finish(accc_ref, ggc_ref, oc_ref)


def _gla(zc3, cc, lrc3, zl3, cl, lrl3, w2, b2, gain3, l, ctx_out):
    nb, lc, _ = zc3.shape
    ll = zl3.shape[1]
    assert lc % GLA_CHUNK == 0 and ll % GLA_CHUNK == 0
    tabs, masks = _gla_tables()

    def seq_specs(z3, cols, lr3, slen):
        qo, ko, vo, go = cols
        return ([pl.BlockSpec((None, slen, GLA_DK), lambda b, h: (b, 0, qo // GLA_DK + h)),
                 pl.BlockSpec((None, slen, GLA_DK), lambda b, h: (b, 0, ko // GLA_DK + h)),
                 pl.BlockSpec((None, slen, GLA_DV), lambda b, h: (b, 0, vo // GLA_DV + h)),
                 pl.BlockSpec((None, slen, GLA_DV), lambda b, h: (b, 0, (0 if go is None else go) // GLA_DV + h)),
                 pl.BlockSpec((None, slen, LANES), lambda b, h: (b, 0, 0))],
                [z3, z3, z3, z3, lr3])

    sc, ac = seq_specs(zc3, cc, lrc3, lc)
    sl, al = seq_specs(zl3, cl, lrl3, ll)
    in_specs = sc + sl + [pl.BlockSpec((None, None, LANES, 2 * GLA_DK), lambda b, h: (l, h, 0, 0)),
                          pl.BlockSpec((None, None, 1, 2 * GLA_DK), lambda b, h: (l, h, 0, 0)),
                          pl.BlockSpec((None, 1, GLA_DV), lambda b, h: (l, 0, 0)),
                          pl.BlockSpec(tabs.shape, lambda b, h: (0, 0, 0)),
                          pl.BlockSpec(masks.shape, lambda b, h: (0, 0, 0, 0))]
    args = ac + al + [w2, b2, gain3, tabs, masks]
    out_specs = [pl.BlockSpec((None, ll, GLA_DV), lambda b, h: (b, 0, h))]
    out_shape = [jax.ShapeDtypeStruct((nb, ll, GLA_HEADS * GLA_DV), BF16)]
    scratch = [pltpu.VMEM((ll, 4 * GLA_DK), BF16), pltpu.VMEM((lc, 4 * GLA_DK), BF16),
               pltpu.VMEM((ll, GLA_DV), F32)]
    if ctx_out:
        out_specs.append(pl.BlockSpec((None, lc, GLA_DV), lambda b, h: (b, 0, h)))
        out_shape.append(jax.ShapeDtypeStruct((nb, lc, GLA_HEADS * GLA_DV), BF16))
        scratch.append(pltpu.VMEM((lc, GLA_DV), F32))
    scratch += [pltpu.VMEM((GLA_DV, GLA_DK), F32), pltpu.VMEM((GLA_DV, GLA_DK), F32)]
    kern = functools.partial(_gla_kernel, n_ctx=lc // GLA_CHUNK, n_lat=ll // GLA_CHUNK, ctx_out=ctx_out)
    outs = pl.pallas_call(
        kern,
        grid=(nb, GLA_HEADS),
        in_specs=in_specs,
        out_specs=out_specs,
        out_shape=out_shape,
        scratch_shapes=scratch,
        compiler_params=_cparams("arbitrary", "arbitrary"),
        name="gla",
    )(*args)
    return (outs[0], outs[1]) if ctx_out else (outs[0], None)


def _merge_kernel(p_ref, d_ref, g_ref, wp_ref, wd_ref, wg_ref, mp_ref, md_ref, mg_ref, y_ref):
    y = (_sigmoid(mp_ref[...].astype(F32)) * _dot(p_ref[...], wp_ref[...])
         + _sigmoid(md_ref[...].astype(F32)) * _dot(d_ref[...], wd_ref[...])
         + _sigmoid(mg_ref[...].astype(F32)) * _dot(g_ref[...], wg_ref[...]))
    y_ref[...] = y.astype(y_ref.dtype)


def _merge(pool_o, diff_o, gla_o, z2d, col_mg, wbp, wbd, wbg, l, tm):
    m, kw = pool_o.shape
    d = wbp.shape[2]
    act = pl.BlockSpec((tm, kw), lambda i: (i, 0))
    wsp = pl.BlockSpec((None, kw, d), lambda i: (l, 0, 0), pipeline_mode=pl.Buffered(1))
    gate = lambda k: pl.BlockSpec((tm, d), lambda i: (i, col_mg // d + k))
    return pl.pallas_call(
        _merge_kernel,
        grid=(m // tm,),
        in_specs=[act, act, act, wsp, wsp, wsp, gate(0), gate(1), gate(2)],
        out_specs=pl.BlockSpec((tm, d), lambda i: (i, 0)),
        out_shape=jax.ShapeDtypeStruct((m, d), BF16),
        compiler_params=_cparams("arbitrary"),
        name="merge",
    )(pool_o, diff_o, gla_o, wbp, wbd, wbg, z2d, z2d, z2d)


def _outproj_kernel(y_ref, w_ref, x_ref, mod_ref, o_ref):
    o_ref[...] = x_ref[...] + mod_ref[2:3, :] * _dot(y_ref[...], w_ref[...])


def _outproj(y, w_out, x2d, mod3, l, row_of_tile, tm):
    m, d = x2d.shape
    return pl.pallas_call(
        _outproj_kernel,
        grid=(m // tm,),
        in_specs=[pl.BlockSpec((tm, d), lambda i: (i, 0)),
                  pl.BlockSpec((None, d, d), lambda i: (l, 0, 0), pipeline_mode=pl.Buffered(1)),
                  pl.BlockSpec((tm, d), lambda i: (i, 0)),
                  pl.BlockSpec((None, 3, d), lambda i: (row_of_tile(i), 0, 0))],
        out_specs=pl.BlockSpec((tm, d), lambda i: (i, 0)),
        out_shape=jax.ShapeDtypeStruct((m, d), F32),
        compiler_params=_cparams("arbitrary"),
        name="outproj",
    )(y, w_out, x2d, mod3)


def _rope_tables(seq_len):
    n_freq = DIFF_HEAD_DIM // 4
    t = jnp.arange(seq_len)
    inv = ROPE_THETA ** (-jnp.arange(n_freq, dtype=F32) / n_freq)
    ang = jnp.concatenate([(t // GRID_W).astype(F32)[:, None] * inv,
                           (t % GRID_W).astype(F32)[:, None] * inv], axis=-1)
    cos, sin = jnp.cos(ang), jnp.sin(ang)
    return jnp.tile(cos, (1, 4)), jnp.tile(jnp.concatenate([-sin, sin], axis=-1), (1, 2))


def _largest_tile(n, cap, mult):
    t = min(n, cap)
    while n % t or t % mult:
        t -= mult
    return t


def kernel(x, c, ctx, c_ctx, norm_g, w_ada, b_ada, w_in, pool_w, pool_scale, diff_q_norm, diff_k_norm, diff_lam_q1, diff_lam_k1, diff_lam_q2, diff_lam_k2, diff_subln, gla_w_gate_f, gla_b_gate_f, gla_w_gate_b, gla_b_gate_b, gla_norm, w_branch_pool, w_branch_diff, w_branch_gla, w_out):
    nb, seq, d = x.shape
    lc = ctx.shape[1]
    depth = w_in.shape[0]
    pw = pool_scale.shape[1]
    dw = DIFF_HEADS * 2 * DIFF_HEAD_DIM
    gkw, gvw = GLA_HEADS * GLA_DK, GLA_HEADS * GLA_DV

    sizes = dict(pu=pw, pg=pw, dq=dw, dk=dw, dv=dw, dg=dw, gq=gkw, gk=gkw, gv=gvw, gg=gvw)
    wcol, off = {}, 0
    for name, size in sizes.items():
        wcol[name] = off
        off += size
    lr_col = off
    zcol = {name: 3 * d + o for name, o in wcol.items()}
    zcol["mg"] = 0

    rope_tabs = _rope_tables(seq)
    cc = jnp.zeros((8, d), F32).at[:nb].set(c).at[nb].set(c_ctx)
    x2d = x.reshape(nb * seq, d)
    ctx2d = ctx.reshape(nb * lc, d)

    w_in_t = jnp.swapaxes(w_in, 1, 2)
    wbp, wbd, wbg, wo = (w.astype(BF16) for w in (w_branch_pool, w_branch_diff, w_branch_gla, w_out))
    norm_g3 = norm_g.reshape(depth, 1, d)
    b_ada3 = b_ada.reshape(depth, 1, 3 * d)
    pool_scale3 = pool_scale.reshape(depth, 1, pw)
    gains3 = jnp.stack([jnp.tile(diff_q_norm, (1, 2)), jnp.tile(diff_k_norm, (1, 2)), diff_subln], axis=1)
    lam4 = jnp.stack([diff_lam_q1, diff_lam_k1, diff_lam_q2, diff_lam_k2], axis=1)
    gla_gain3 = gla_norm.reshape(depth, 1, GLA_DV)
    per_head = lambda w: w.reshape(depth, GLA_RANK, GLA_HEADS, GLA_DK).transpose(0, 2, 1, 3)
    w2 = jnp.zeros((depth, GLA_HEADS, LANES, 2 * GLA_DK), F32)
    w2 = w2.at[:, :, :GLA_RANK, :GLA_DK].set(per_head(gla_w_gate_f))
    w2 = w2.at[:, :, GLA_RANK:2 * GLA_RANK, GLA_DK:].set(per_head(gla_w_gate_b)).astype(BF16)
    b2 = jnp.concatenate([gla_b_gate_f.reshape(depth, GLA_HEADS, 1, GLA_DK),
                          gla_b_gate_b.reshape(depth, GLA_HEADS, 1, GLA_DK)], axis=-1)

    rows_gcd = math.gcd(seq, nb * lc)
    tm_pre = _largest_tile(rows_gcd, 512, 16)
    tm_in = _largest_tile(rows_gcd, 1024, 16)
    tn_in = _largest_tile(math.gcd(lr_col, 3 * d), 1024, 256)
    row_x = lambda tm: (lambda i: (i * tm) // seq)
    row_c = lambda i: nb
    kv_cols = ((zcol["dk"], zcol["dg"]), (zcol["gk"], zcol["gg"]))
    kv_tiles = tuple(t for t in range((3 * d + lr_col) // tn_in)
                     if any(lo < (t + 1) * tn_in and t * tn_in < hi for lo, hi in kv_cols))
    ccol = zcol

    for l in range(depth):
        last = l == depth - 1
        lambda_init = 0.8 - 0.6 * math.exp(-0.3 * l)
        mod3 = _ada(cc, w_ada, b_ada3, l).reshape(8, 3, d)
        h, lr, lrc = _prenorm(x2d, ctx2d, mod3, norm_g3, w_in_t, l, lr_col, seq, nb, tm_pre)
        z, zc = _inproj(h, w_in_t, l, lr_col, 3 * d, nb * seq, tm_in, tn_in, kv_tiles if last else None)
        z3, zc3 = z.reshape(nb, seq, -1), zc.reshape(nb, lc, -1)
        lr3, lrc3 = lr.reshape(nb, seq, LANES), lrc.reshape(nb, lc, LANES)

        pool_l = _pool(z3, zcol["pu"], zcol["pg"], pool_w, pool_scale3, l, _largest_tile(seq, 512, 16))
        diff_l = _attn(z3, zcol["dq"], zcol["dg"], zc3, ccol["dk"], ccol["dv"], z3, zcol["dk"], zcol["dv"],
                       rope_tabs, gains3, lam4, l, lambda_init, _largest_tile(seq, 2048, 16))
        gla_l, gla_c = _gla(zc3, (ccol["gq"], ccol["gk"], ccol["gv"], ccol["gg"]), lrc3,
                            z3, (zcol["gq"], zcol["gk"], zcol["gv"], zcol["gg"]), lr3,
                            w2, b2, gla_gain3, l, ctx_out=not last)

        tm_m = _largest_tile(seq, 512, 16)
        y = _merge(pool_l.reshape(nb * seq, pw), diff_l.reshape(nb * seq, dw), gla_l.reshape(nb * seq, gvw),
                   z, zcol["mg"], wbp, wbd, wbg, l, tm_m)
        x2d_new = _outproj(y, wo, x2d, mod3, l, row_x(tm_m), tm_m)

        if not last:
            pool_c = _pool(zc3, zcol["pu"], zcol["pg"], pool_w, pool_scale3, l, _largest_tile(lc, 512, 16))
            diff_c = _attn(zc3, zcol["dq"], zcol["dg"], zc3, zcol["dk"], zcol["dv"], None, None, None,
                           None, gains3, lam4, l, lambda_init, _largest_tile(lc, 256, 16))
            tm_mc = _largest_tile(nb * lc, 512, 16)
            y_c = _merge(pool_c.reshape(nb * lc, pw), diff_c.reshape(nb * lc, dw), gla_c.reshape(nb * lc, gvw),
                         zc, zcol["mg"], wbp, wbd, wbg, l, tm_mc)
            ctx2d = _outproj(y_c, wo, ctx2d, mod3, l, row_c, tm_mc)
        x2d = x2d_new

    return x2d.reshape(nb, seq, d)
```

```python
import functools
import math

import jax
import jax.numpy as jnp
import numpy as np
from jax import lax
from jax.experimental import pallas as pl
from jax.experimental.pallas import tpu as pltpu

F32 = jnp.float32
BF16 = jnp.bfloat16

EPS = 1e-6
GRID_W = 64
ROPE_THETA = 10000.0

POOL_WINDOWS = (2, 4, 8, 16)
POOL_HALO = 16
DIFF_HEADS = 8
DIFF_HEAD_DIM = 64
GLA_HEADS = 4
GLA_DK = 128
GLA_DV = 256
GLA_RANK = 16
GLA_NORMALIZER = 16.0
GLA_CHUNK = 256
LANES = 128
LOG2E = math.log2(math.e)

VMEM_LIMIT = 48 * 1024 * 1024


VMEM_LIMIT_WIDE = 58 * 1024 * 1024


def _cparams(*sem, vmem=VMEM_LIMIT):
    return pltpu.CompilerParams(dimension_semantics=sem, vmem_limit_bytes=vmem)


def _sigmoid(x):
    return 0.5 * jnp.tanh(0.5 * x) + 0.5


def _silu(x):
    return x * _sigmoid(x)


def _dot(a, b):
    return jnp.dot(a, b, preferred_element_type=F32)


def _dot_nt(a, b):
    return lax.dot_general(a, b, (((1,), (1,)), ((), ())), preferred_element_type=F32)


def _dot_tn(a, b):
    return lax.dot_general(a, b, (((0,), (0,)), ((), ())), preferred_element_type=F32)


def _ada_kernel(cc_ref, w_ref, b_ref, o_ref):
    @pl.when(pl.program_id(0) == 0)
    def _():
        o_ref[...] = jnp.broadcast_to(b_ref[...], o_ref.shape)

    a = _silu(cc_ref[...]).astype(BF16)
    o_ref[...] += _dot(a, w_ref[...].astype(BF16))


def _ada(cc, w_ada, b_ada3, l, tk=256):
    rows, d = cc.shape
    n = w_ada.shape[2]
    return pl.pallas_call(
        _ada_kernel,
        grid=(d // tk,),
        in_specs=[pl.BlockSpec((rows, tk), lambda k: (0, k)),
                  pl.BlockSpec((None, tk, n), lambda k: (l, k, 0)),
                  pl.BlockSpec((None, 1, n), lambda k: (l, 0, 0))],
        out_specs=pl.BlockSpec((rows, n), lambda k: (0, 0)),
        out_shape=jax.ShapeDtypeStruct((rows, n), F32),
        compiler_params=_cparams("arbitrary"),
        name="ada",
    )(cc, w_ada, b_ada3)


def _prenorm_kernel(x_ref, c_ref, mod_ref, g_ref, wlr_ref, h_ref, lrx_ref, lrc_ref, *, n_x):
    def emit(src_ref, lr_ref):
        x = src_ref[...]
        ms = jnp.mean(x * x, axis=-1, keepdims=True)
        y = x * lax.rsqrt(ms + EPS) * g_ref[...]
        h = (y * (1.0 + mod_ref[1:2, :]) + mod_ref[0:1, :]).astype(BF16)
        h_ref[...] = h
        lr_ref[...] = _dot_nt(h, wlr_ref[...].astype(BF16)).astype(lr_ref.dtype)

    @pl.when(pl.program_id(0) < n_x)
    def _():
        emit(x_ref, lrx_ref)

    @pl.when(pl.program_id(0) >= n_x)
    def _():
        emit(c_ref, lrc_ref)


def _prenorm(x2d, ctx2d, mod3, norm_g3, w_in_t, l, lr_col, seq, nb, tm):
    mx, d = x2d.shape
    mc = ctx2d.shape[0]
    n_x, n_c = mx // tm, mc // tm
    return pl.pallas_call(
        functools.partial(_prenorm_kernel, n_x=n_x),
        grid=(n_x + n_c,),
        in_specs=[pl.BlockSpec((tm, d), lambda i: (jnp.minimum(i, n_x - 1), 0)),
                  pl.BlockSpec((tm, d), lambda i: (jnp.maximum(i - n_x, 0), 0)),
                  pl.BlockSpec((None, 3, d), lambda i: (jnp.where(i < n_x, (i * tm) // seq, nb), 0, 0)),
                  pl.BlockSpec((None, 1, d), lambda i: (l, 0, 0)),
                  pl.BlockSpec((None, LANES, d), lambda i: (l, lr_col // LANES, 0))],
        out_specs=[pl.BlockSpec((tm, d), lambda i: (i, 0)),
                   pl.BlockSpec((tm, LANES), lambda i: (jnp.minimum(i, n_x - 1), 0)),
                   pl.BlockSpec((tm, LANES), lambda i: (jnp.maximum(i - n_x, 0), 0))],
        out_shape=[jax.ShapeDtypeStruct((mx + mc, d), BF16),
                   jax.ShapeDtypeStruct((mx, LANES), BF16),
                   jax.ShapeDtypeStruct((mc, LANES), BF16)],
        compiler_params=_cparams("arbitrary"),
        name="prenorm",
    )(x2d, ctx2d, mod3, norm_g3, w_in_t)


def _inproj_kernel(h_ref, wa_ref, wb_ref, zx_ref, zc_ref, wbf_ref, *, n_x, n_shift, shift, ctx_tiles):
    j, i = pl.program_id(0), pl.program_id(1)

    @pl.when(i == 0)
    def _():
        @pl.when(j < n_shift)
        def _():
            wbf_ref[...] = jnp.concatenate([wa_ref[shift:, :], wb_ref[...]], axis=0).astype(BF16)

        @pl.when(j >= n_shift)
        def _():
            wbf_ref[...] = wa_ref[...].astype(BF16)

    @pl.when(i < n_x)
    def _():
        zx_ref[...] = _dot_nt(h_ref[...], wbf_ref[...]).astype(zx_ref.dtype)

    @pl.when(i >= n_x)
    def _():
        if ctx_tiles is None:
            zc_ref[...] = _dot_nt(h_ref[...], wbf_ref[...]).astype(zc_ref.dtype)
        else:
            needed = functools.reduce(jnp.logical_or, [j == t for t in ctx_tiles])

            @pl.when(needed)
            def _():
                zc_ref[...] = _dot_nt(h_ref[...], wbf_ref[...]).astype(zc_ref.dtype)

            @pl.when(jnp.logical_not(needed))
            def _():
                zc_ref[...] = jnp.zeros_like(zc_ref)


def _inproj(h, w_in_t, l, lr_col, n_gate, mx, tm, tn, ctx_tiles):
    m, d = h.shape
    n_x = mx // tm
    n_shift = n_gate // tn
    shift = w_in_t.shape[1] - lr_col - n_gate
    assert shift % 8 == 0 and lr_col % tn == 0 and tn % shift == 0
    first_gate = lr_col // tn
    return pl.pallas_call(
        functools.partial(_inproj_kernel, n_x=n_x, n_shift=n_shift, shift=shift, ctx_tiles=ctx_tiles),
        grid=(n_shift + first_gate, m // tm),
        in_specs=[pl.BlockSpec((tm, d), lambda j, i: (i, 0)),
                  pl.BlockSpec((None, tn, d), lambda j, i: (l, jnp.where(j < n_shift, first_gate + j, j - n_shift), 0)),
                  pl.BlockSpec((None, shift, d),
                               lambda j, i: (l, jnp.where(j < n_shift, (lr_col + (j + 1) * tn) // shift, 0), 0))],
        out_specs=[pl.BlockSpec((tm, tn), lambda j, i: (jnp.minimum(i, n_x - 1), j)),
                   pl.BlockSpec((tm, tn), lambda j, i: (jnp.maximum(i - n_x, 0), j))],
        out_shape=[jax.ShapeDtypeStruct((mx, n_gate + lr_col), BF16),
                   jax.ShapeDtypeStruct((m - mx, n_gate + lr_col), BF16)],
        scratch_shapes=[pltpu.VMEM((tn, d), BF16)],
        compiler_params=_cparams("arbitrary", "arbitrary", vmem=VMEM_LIMIT_WIDE),
        name="inproj",
    )(h, w_in_t, w_in_t)


def _pool_bands(tile):
    t = np.arange(tile)[:, None]
    s = np.arange(tile + 2 * POOL_HALO)[None, :] - POOL_HALO
    return np.stack([(s >= t - w // 2) & (s < t + w // 2) for w in POOL_WINDOWS])


def _pool_kernel(up_ref, uc_ref, un_ref, pg_ref, band_ref, w_ref, sc_ref, o_ref, *, tile, seq_len):
    base = pl.program_id(1) * tile
    gw = w_ref.shape[-1]
    tc = base + lax.broadcasted_iota(jnp.int32, (tile, 1), 0)
    up = jnp.where(base > 0, up_ref[...], jnp.zeros_like(up_ref))
    un = jnp.where(base + tile < seq_len, un_ref[...], jnp.zeros_like(un_ref))
    for g, window in enumerate(POOL_WINDOWS):
        half = window // 2
        cols = slice(g * gw, (g + 1) * gw)
        u_all = jnp.concatenate([up[:, cols], uc_ref[:, cols], un[:, cols]], axis=0)
        wsum = _dot(band_ref[g], u_all)
        cnt = (jnp.minimum(tc + half, seq_len) - jnp.maximum(tc - half, 0)).astype(F32)
        dcen = wsum / cnt - uc_ref[:, cols].astype(F32)
        y = _dot(dcen.astype(BF16), w_ref[g].astype(BF16)) * sc_ref[:, cols]
        o_ref[:, cols] = (y * _silu(pg_ref[:, cols].astype(F32))).astype(o_ref.dtype)


def _pool(z3, col_u, col_g, pool_w, pool_scale3, l, tile):
    nseq, seq_len, _ = z3.shape
    ngrp = len(POOL_WINDOWS)
    gw = pool_w.shape[-1]
    pw = ngrp * gw
    cu, cg = col_u // pw, col_g // pw
    nhalo = seq_len // POOL_HALO
    per = tile // POOL_HALO
    kern = functools.partial(_pool_kernel, tile=tile, seq_len=seq_len)
    bands = jnp.asarray(_pool_bands(tile), BF16)
    return pl.pallas_call(
        kern,
        grid=(nseq, seq_len // tile),
        in_specs=[
            pl.BlockSpec((None, POOL_HALO, pw), lambda s, r: (s, jnp.maximum(r * per - 1, 0), cu)),
            pl.BlockSpec((None, tile, pw), lambda s, r: (s, r, cu)),
            pl.BlockSpec((None, POOL_HALO, pw), lambda s, r: (s, jnp.minimum((r + 1) * per, nhalo - 1), cu)),
            pl.BlockSpec((None, tile, pw), lambda s, r: (s, r, cg)),
            pl.BlockSpec(bands.shape, lambda s, r: (0, 0, 0)),
            pl.BlockSpec((None, ngrp, gw, gw), lambda s, r: (l, 0, 0, 0)),
            pl.BlockSpec((None, 1, pw), lambda s, r: (l, 0, 0)),
        ],
        out_specs=pl.BlockSpec((None, tile, pw), lambda s, r: (s, r, 0)),
        out_shape=jax.ShapeDtypeStruct((nseq, seq_len, pw), BF16),
        compiler_params=_cparams("arbitrary", "arbitrary"),
        name="pool",
    )(z3, z3, z3, z3, bands, pool_w, pool_scale3)


def _head_norm(t, gain, gmat):
    sq = t * t
    hi = sq.astype(BF16)
    lo = (sq - hi.astype(F32)).astype(BF16)
    ssq = _dot(hi, gmat) + _dot(lo, gmat)
    return t * lax.rsqrt(ssq * (1.0 / DIFF_HEAD_DIM) + EPS) * gain


def _rope(t, cos_t, sin_s, first_half):
    half = DIFF_HEAD_DIM // 2
    lower = pltpu.roll(t, half, 1)
    upper = pltpu.roll(t, LANES - half, 1)
    return t * cos_t + jnp.where(first_half, upper, lower) * sin_s


def _attn_kernel(*refs, tq, row_group, rope_q, has_latent, lambda_init):
    if has_latent:
        (q_ref, dg_ref, kc_ref, vc_ref, kl_ref, vl_ref, cosq_ref, sinq_ref, cosk_ref, sink_ref,
         gains_ref, lam_ref, o_ref, ks_ref, vs_ref) = refs
    else:
        (q_ref, dg_ref, kc_ref, vc_ref, gains_ref, lam_ref, o_ref, ks_ref, vs_ref) = refs
    hw = 2 * DIFF_HEAD_DIM
    lc = kc_ref.shape[0]
    q_gain, k_gain, sub_gain = gains_ref[0:1, :], gains_ref[1:2, :], gains_ref[2:3, :]

    lane = lax.broadcasted_iota(jnp.int32, (1, LANES), 1)
    first_half = (lane & (DIFF_HEAD_DIM - 1)) < (DIFF_HEAD_DIM // 2)
    sub1 = lane < DIFF_HEAD_DIM
    gi = jnp.where(lax.broadcasted_iota(jnp.int32, (LANES, LANES), 0) < DIFF_HEAD_DIM, 1.0, 0.0)
    gj = jnp.where(lax.broadcasted_iota(jnp.int32, (LANES, LANES), 1) < DIFF_HEAD_DIM, 1.0, 0.0)
    gmat = (gi * gj + (1.0 - gi) * (1.0 - gj)).astype(BF16)

    @pl.when(pl.program_id(2) == 0)
    def _():
        ks_ref[:lc, :] = _head_norm(kc_ref[...].astype(F32), k_gain, gmat).astype(BF16)
        vs_ref[:lc, :hw] = vc_ref[...]
        vs_ref[:, hw:] = jnp.ones((vs_ref.shape[0], hw), BF16)
        if has_latent:
            kl = _head_norm(kl_ref[...].astype(F32), k_gain, gmat)
            ks_ref[lc:, :] = _rope(kl, cosk_ref[...], sink_ref[...], first_half).astype(BF16)
            vs_ref[lc:, :hw] = vl_ref[...]

    q = _head_norm(q_ref[...].astype(F32), q_gain, gmat)
    if rope_q:
        q = _rope(q, cosq_ref[...], sinq_ref[...], first_half)
    q = q * (DIFF_HEAD_DIM ** -0.5 * LOG2E)
    qq = jnp.concatenate([jnp.where(sub1, q, 0.0), jnp.where(sub1, 0.0, q)], axis=0).astype(BF16)

    pvs = []
    for r0 in range(0, 2 * tq, row_group):
        s = _dot_nt(qq[r0:r0 + row_group], ks_ref[...])
        e = jnp.exp2(s - jnp.max(s, axis=-1, keepdims=True)).astype(BF16)
        acc = _dot(e, vs_ref[...])
        pvs.append(acc[:, :hw] / acc[:, hw:hw + 1])
    pv = jnp.concatenate(pvs, axis=0)

    lam_p = lam_ref[...]
    lam = (jnp.exp(jnp.sum(lam_p[0:1] * lam_p[1:2], axis=-1, keepdims=True))
           - jnp.exp(jnp.sum(lam_p[2:3] * lam_p[3:4], axis=-1, keepdims=True)) + lambda_init)
    o = pv[:tq] - lam * pv[tq:]

    ms = jnp.mean(o * o, axis=-1, keepdims=True)
    y = o * lax.rsqrt(ms + EPS) * sub_gain * (1.0 - lambda_init)
    o_ref[...] = (y * _silu(dg_ref[...].astype(F32))).astype(o_ref.dtype)


def _attn(zq3, cq, cdg, zc3, ckc, cvc, zl3, ckl, cvl, rope_tabs, gains3, lam4, l, lambda_init, tq):
    nb, lq, _ = zq3.shape
    lc = zc3.shape[1]
    has_latent = zl3 is not None
    hw = 2 * DIFF_HEAD_DIM
    b_of = lambda c: c // hw
    qmap = lambda off: (lambda b, h, i: (b, i, b_of(off) + h))
    kmap = lambda off: (lambda b, h, i: (b, 0, b_of(off) + h))
    const = lambda b, h, i: (0, 0)
    in_specs = [pl.BlockSpec((None, tq, hw), qmap(cq)),
                pl.BlockSpec((None, tq, hw), qmap(cdg)),
                pl.BlockSpec((None, lc, hw), kmap(ckc)),
                pl.BlockSpec((None, lc, hw), kmap(cvc))]
    args = [zq3, zq3, zc3, zc3]
    lk = lc + (zl3.shape[1] if has_latent else 0)
    scratch = [pltpu.VMEM((lk, hw), BF16), pltpu.VMEM((lk, 2 * hw), BF16)]
    if has_latent:
        ll = zl3.shape[1]
        cos_t, sin_s = rope_tabs
        in_specs += [pl.BlockSpec((None, ll, hw), kmap(ckl)),
                     pl.BlockSpec((None, ll, hw), kmap(cvl)),
                     pl.BlockSpec((tq, hw), lambda b, h, i: (i, 0)),
                     pl.BlockSpec((tq, hw), lambda b, h, i: (i, 0)),
                     pl.BlockSpec((ll, hw), const),
                     pl.BlockSpec((ll, hw), const)]
        args += [zl3, zl3, cos_t, sin_s, cos_t, sin_s]
    in_specs += [pl.BlockSpec((None, 3, hw), lambda b, h, i: (l, 0, 0)),
                 pl.BlockSpec((None, 4, DIFF_HEAD_DIM), lambda b, h, i: (l, 0, 0))]
    args += [gains3, lam4]
    kern = functools.partial(_attn_kernel, tq=tq, row_group=min(2 * tq, LANES), rope_q=has_latent, has_latent=has_latent,
                             lambda_init=lambda_init)
    return pl.pallas_call(
        kern,
        grid=(nb, DIFF_HEADS, lq // tq),
        in_specs=in_specs,
        out_specs=pl.BlockSpec((None, tq, hw), lambda b, h, i: (b, i, h)),
        out_shape=jax.ShapeDtypeStruct((nb, lq, DIFF_HEADS * hw), BF16),
        scratch_shapes=scratch,
        compiler_params=_cparams("arbitrary", "arbitrary", "arbitrary"),
        name="diff_attn",
    )(*args)


def _log_sigmoid(x):
    return jnp.minimum(x, 0.0) - jnp.log1p(jnp.exp(-jnp.abs(x)))


GLA_LEVELS = tuple(1 << i for i in range(GLA_CHUNK.bit_length() - 1))
GLA_TAB_LEVELS = tuple(m for m in GLA_LEVELS if m < 8)
GLA_ROW_LEVELS = tuple(m for m in GLA_LEVELS if m >= 8)
GLA_SLOT_CUM = len(GLA_TAB_LEVELS)
GLA_ATT_ROWS = 32
GLA_UNROLL = 4
GLA_TOT_ROWS = 16


def _gla_boundary(m, rev):
    return m if rev else m - 1


@functools.lru_cache(maxsize=None)
def _gla_tables_np():
    c = GLA_CHUNK
    t = np.arange(c)[:, None]
    j = np.arange(c)[None, :]
    sums, masks = [], []
    for rev in (False, True):
        rows, mk = [], []
        for m in GLA_LEVELS:
            base = t & ~(2 * m - 1)
            bd = base + _gla_boundary(m, rev)
            is_q = ((t & m) == 0) if rev else ((t & m) != 0)
            if rev:
                between = np.where(is_q, (j >= t) & (j < bd), (j >= bd) & (j < t))
            else:
                between = np.where(is_q, (j > bd) & (j <= t), (j > t) & (j <= bd))
            if m in GLA_TAB_LEVELS:
                rows.append(between)
            mk.append((base == (j & ~(2 * m - 1))) & is_q & (((j & m) != 0) if rev else ((j & m) == 0)))
        rows.append(j >= t if rev else j <= t)
        rows.append(np.ones((GLA_TOT_ROWS, c), bool))
        mk.append(t == j)
        sums.append(np.concatenate(rows, axis=0))
        masks.append(np.stack(mk))
    return np.stack(sums), np.stack(masks)


def _gla_masks_np():
    return _gla_tables_np()[1]


def _gla_tables():
    sums, masks = _gla_tables_np()
    return jnp.asarray(sums, BF16), jnp.asarray(masks, F32)


def _gla_chunk(q, k, v, ghl, st, tab_ref, mask_ref, rev, need_out):
    c = GLA_CHUNK
    first = 0 if need_out else GLA_SLOT_CUM
    x2 = _dot(tab_ref[first * c:, :], ghl)
    x = x2[:, :GLA_DK] + x2[:, GLA_DK:]
    cum = x[(GLA_SLOT_CUM - first) * c:(GLA_SLOT_CUM - first + 1) * c]
    tot = x[-GLA_TOT_ROWS:1 - GLA_TOT_ROWS]
    kd = (k * jnp.exp2(tot - cum)).astype(BF16)
    st_new = st * jnp.exp2(tot) + _dot_tn(v, kd)
    if not need_out:
        return None, st_new

    out = _dot_nt((q * jnp.exp2(cum)).astype(BF16), st.astype(BF16))

    own = _gla_masks_np()[int(rev)]
    n_rb, n_lt = c // GLA_ATT_ROWS, c // LANES
    regions = [[None] * n_lt for _ in range(n_rb)]

    def accumulate(lvl, prod):
        for rb in range(n_rb):
            rs = slice(rb * GLA_ATT_ROWS, (rb + 1) * GLA_ATT_ROWS)
            for lt in range(n_lt):
                ls = slice(lt * LANES, (lt + 1) * LANES)
                if not own[lvl, rs, ls].any():
                    continue
                part = prod[rs, ls]
                if not own[lvl, rs, ls].all():
                    part = part * mask_ref[lvl, rs, ls]
                regions[rb][lt] = part if regions[rb][lt] is None else regions[rb][lt] + part

    accumulate(len(GLA_LEVELS), _dot_nt(q.astype(BF16), k.astype(BF16)))
    for lvl, m in enumerate(GLA_LEVELS):
        if m in GLA_TAB_LEVELS:
            xl = x[lvl * c:(lvl + 1) * c]
        else:
            anchor = jnp.concatenate(
                [jnp.broadcast_to(cum[b0 + _gla_boundary(m, rev):b0 + _gla_boundary(m, rev) + 1], (2 * m, GLA_DK))
                 for b0 in range(0, c, 2 * m)], axis=0)
            xl = -jnp.abs(cum - anchor)
        el = jnp.exp2(xl)
        accumulate(lvl, _dot_nt((q * el).astype(BF16), (k * el).astype(BF16)))
    zero = jnp.zeros((GLA_ATT_ROWS, LANES), F32)
    att = jnp.concatenate([jnp.concatenate([zero if r is None else r for r in row], axis=1) for row in regions], axis=0)
    return out + _dot(att.astype(BF16), v), st_new


def _gla_kernel(*refs, n_ctx, n_lat, ctx_out):
    (qc_ref, kc_ref, vc_ref, ggc_ref, lrc_ref, ql_ref, kl_ref, vl_ref, ggl_ref, lrl_ref,
     w2_ref, b2_ref, gain_ref, tab_ref, mask_ref) = refs[:15]
    rest = refs[15:]
    if ctx_out:
        ol_ref, oc_ref, gl_ref, gc_ref, accl_ref, accc_ref, stf_ref, stb_ref = rest
    else:
        ol_ref, gl_ref, gc_ref, accl_ref, stf_ref, stb_ref = rest
        oc_ref = accc_ref = None

    def decays(lr_ref, g_ref):
        g = _log_sigmoid(_dot(lr_ref[...], w2_ref[...]) + b2_ref[...]) * (LOG2E / GLA_NORMALIZER)
        hi = g.astype(BF16)
        lo = (g - hi.astype(F32)).astype(BF16)
        for d in range(2):
            g_ref[:, (2 * d) * GLA_DK:(2 * d + 1) * GLA_DK] = hi[:, d * GLA_DK:(d + 1) * GLA_DK]
            g_ref[:, (2 * d + 1) * GLA_DK:(2 * d + 2) * GLA_DK] = lo[:, d * GLA_DK:(d + 1) * GLA_DK]

    decays(lrl_ref, gl_ref)
    decays(lrc_ref, gc_ref)
    stf_ref[...] = jnp.zeros_like(stf_ref)
    stb_ref[...] = jnp.zeros_like(stb_ref)
    scale = GLA_DK ** -0.5

    def scan(n, q_ref, k_ref, v_ref, g_ref, acc_ref):
        need_out = acc_ref is not None

        def step(i, assign_up, assign_down):
            for c_idx, d, st_ref, first_visit in ((i, 0, stf_ref, assign_up), (n - 1 - i, 1, stb_ref, assign_down)):
                rows = pl.ds(pl.multiple_of(c_idx * GLA_CHUNK, GLA_CHUNK), GLA_CHUNK)
                q = q_ref[rows, :].astype(F32) * scale
                k = k_ref[rows, :].astype(F32)
                ghl = g_ref[rows, 2 * d * GLA_DK:(2 * d + 2) * GLA_DK]
                out, st_new = _gla_chunk(q, k, v_ref[rows, :], ghl, st_ref[...], tab_ref.at[d], mask_ref.at[d],
                                         d == 1, need_out)
                st_ref[...] = st_new
                if need_out:
                    if first_visit:
                        acc_ref[rows, :] = out
                    else:
                        acc_ref[rows, :] += out

        def first(i, carry):
            step(i, True, True)
            return carry

        def second(i, carry):
            step(i, False, False)
            return carry

        if n // 2:
            lax.fori_loop(0, n // 2, first, 0, unroll=min(GLA_UNROLL, n // 2))
        if n % 2:
            step(n // 2, True, False)
        if n // 2:
            lax.fori_loop((n + 1) // 2, n, second, 0, unroll=min(GLA_UNROLL, n // 2))

    def finish(acc_ref, gg_ref, o_ref):
        o = acc_ref[...]
        ms = jnp.mean(o * o, axis=-1, keepdims=True)
        y = o * lax.rsqrt(ms + EPS) * gain_ref[...]
        o_ref[...] = (y * _silu(gg_ref[...].astype(F32))).astype(o_ref.dtype)

    scan(n_ctx, qc_ref, kc_ref, vc_ref, gc_ref, accc_ref)
    scan(n_lat, ql_ref, kl_ref, vl_ref, gl_ref, accl_ref)
    finish(accl_ref, ggl_ref, ol_ref)
    if ctx_out:
        finish(accc_ref, ggc_ref, oc_ref)


def _gla(zc3, cc, lrc3, zl3, cl, lrl3, w2, b2, gain3, l, ctx_out):
    nb, lc, _ = zc3.shape
    ll = zl3.shape[1]
    assert lc % GLA_CHUNK == 0 and ll % GLA_CHUNK == 0
    tabs, masks = _gla_tables()

    def seq_specs(z3, cols, lr3, slen):
        qo, ko, vo, go = cols
        return ([pl.BlockSpec((None, slen, GLA_DK), lambda b, h: (b, 0, qo // GLA_DK + h)),
                 pl.BlockSpec((None, slen, GLA_DK), lambda b, h: (b, 0, ko // GLA_DK + h)),
                 pl.BlockSpec((None, slen, GLA_DV), lambda b, h: (b, 0, vo // GLA_DV + h)),
                 pl.BlockSpec((None, slen, GLA_DV), lambda b, h: (b, 0, (0 if go is None else go) // GLA_DV + h)),
                 pl.BlockSpec((None, slen, LANES), lambda b, h: (b, 0, 0))],
                [z3, z3, z3, z3, lr3])

    sc, ac = seq_specs(zc3, cc, lrc3, lc)
    sl, al = seq_specs(zl3, cl, lrl3, ll)
    in_specs = sc + sl + [pl.BlockSpec((None, None, LANES, 2 * GLA_DK), lambda b, h: (l, h, 0, 0)),
                          pl.BlockSpec((None, None, 1, 2 * GLA_DK), lambda b, h: (l, h, 0, 0)),
                          pl.BlockSpec((None, 1, GLA_DV), lambda b, h: (l, 0, 0)),
                          pl.BlockSpec(tabs.shape, lambda b, h: (0, 0, 0)),
                          pl.BlockSpec(masks.shape, lambda b, h: (0, 0, 0, 0))]
    args = ac + al + [w2, b2, gain3, tabs, masks]
    out_specs = [pl.BlockSpec((None, ll, GLA_DV), lambda b, h: (b, 0, h))]
    out_shape = [jax.ShapeDtypeStruct((nb, ll, GLA_HEADS * GLA_DV), BF16)]
    scratch = [pltpu.VMEM((ll, 4 * GLA_DK), BF16), pltpu.VMEM((lc, 4 * GLA_DK), BF16),
               pltpu.VMEM((ll, GLA_DV), F32)]
    if ctx_out:
        out_specs.append(pl.BlockSpec((None, lc, GLA_DV), lambda b, h: (b, 0, h)))
        out_shape.append(jax.ShapeDtypeStruct((nb, lc, GLA_HEADS * GLA_DV), BF16))
        scratch.append(pltpu.VMEM((lc, GLA_DV), F32))
    scratch += [pltpu.VMEM((GLA_DV, GLA_DK), F32), pltpu.VMEM((GLA_DV, GLA_DK), F32)]
    kern = functools.partial(_gla_kernel, n_ctx=lc // GLA_CHUNK, n_lat=ll // GLA_CHUNK, ctx_out=ctx_out)
    outs = pl.pallas_call(
        kern,
        grid=(nb, GLA_HEADS),
        in_specs=in_specs,
        out_specs=out_specs,
        out_shape=out_shape,
        scratch_shapes=scratch,
        compiler_params=_cparams("arbitrary", "arbitrary"),
        name="gla",
    )(*args)
    return (outs[0], outs[1]) if ctx_out else (outs[0], None)


def _merge_kernel(p_ref, d_ref, g_ref, wp_ref, wd_ref, wg_ref, mp_ref, md_ref, mg_ref, y_ref):
    y = (_sigmoid(mp_ref[...].astype(F32)) * _dot(p_ref[...], wp_ref[...])
         + _sigmoid(md_ref[...].astype(F32)) * _dot(d_ref[...], wd_ref[...])
         + _sigmoid(mg_ref[...].astype(F32)) * _dot(g_ref[...], wg_ref[...]))
    y_ref[...] = y.astype(y_ref.dtype)


def _merge(pool_o, diff_o, gla_o, z2d, col_mg, wbp, wbd, wbg, l, tm):
    m, kw = pool_o.shape
    d = wbp.shape[2]
    act = pl.BlockSpec((tm, kw), lambda i: (i, 0))
    wsp = pl.BlockSpec((None, kw, d), lambda i: (l, 0, 0), pipeline_mode=pl.Buffered(1))
    gate = lambda k: pl.BlockSpec((tm, d), lambda i: (i, col_mg // d + k))
    return pl.pallas_call(
        _merge_kernel,
        grid=(m // tm,),
        in_specs=[act, act, act, wsp, wsp, wsp, gate(0), gate(1), gate(2)],
        out_specs=pl.BlockSpec((tm, d), lambda i: (i, 0)),
        out_shape=jax.ShapeDtypeStruct((m, d), BF16),
        compiler_params=_cparams("arbitrary"),
        name="merge",
    )(pool_o, diff_o, gla_o, wbp, wbd, wbg, z2d, z2d, z2d)


def _outproj_kernel(y_ref, w_ref, x_ref, mod_ref, o_ref):
    o_ref[...] = x_ref[...] + mod_ref[2:3, :] * _dot(y_ref[...], w_ref[...])


def _outproj(y, w_out, x2d, mod3, l, row_of_tile, tm):
    m, d = x2d.shape
    return pl.pallas_call(
        _outproj_kernel,
        grid=(m // tm,),
        in_specs=[pl.BlockSpec((tm, d), lambda i: (i, 0)),
                  pl.BlockSpec((None, d, d), lambda i: (l, 0, 0), pipeline_mode=pl.Buffered(1)),
                  pl.BlockSpec((tm, d), lambda i: (i, 0)),
                  pl.BlockSpec((None, 3, d), lambda i: (row_of_tile(i), 0, 0))],
        out_specs=pl.BlockSpec((tm, d), lambda i: (i, 0)),
        out_shape=jax.ShapeDtypeStruct((m, d), F32),
        compiler_params=_cparams("arbitrary"),
        name="outproj",
    )(y, w_out, x2d, mod3)


def _rope_tables(seq_len):
    n_freq = DIFF_HEAD_DIM // 4
    t = jnp.arange(seq_len)
    inv = ROPE_THETA ** (-jnp.arange(n_freq, dtype=F32) / n_freq)
    ang = jnp.concatenate([(t // GRID_W).astype(F32)[:, None] * inv,
                           (t % GRID_W).astype(F32)[:, None] * inv], axis=-1)
    cos, sin = jnp.cos(ang), jnp.sin(ang)
    return jnp.tile(cos, (1, 4)), jnp.tile(jnp.concatenate([-sin, sin], axis=-1), (1, 2))


def _largest_tile(n, cap, mult):
    t = min(n, cap)
    while n % t or t % mult:
        t -= mult
    return t


def kernel(x, c, ctx, c_ctx, norm_g, w_ada, b_ada, w_in, pool_w, pool_scale, diff_q_norm, diff_k_norm, diff_lam_q1, diff_lam_k1, diff_lam_q2, diff_lam_k2, diff_subln, gla_w_gate_f, gla_b_gate_f, gla_w_gate_b, gla_b_gate_b, gla_norm, w_branch_pool, w_branch_diff, w_branch_gla, w_out):
    nb, seq, d = x.shape
    lc = ctx.shape[1]
    depth = w_in.shape[0]
    pw = pool_scale.shape[1]
    dw = DIFF_HEADS * 2 * DIFF_HEAD_DIM
    gkw, gvw = GLA_HEADS * GLA_DK, GLA_HEADS * GLA_DV

    sizes = dict(pu=pw, pg=pw, dq=dw, dk=dw, dv=dw, dg=dw, gq=gkw, gk=gkw, gv=gvw, gg=gvw)
    wcol, off = {}, 0
    for name, size in sizes.items():
        wcol[name] = off
        off += size
    lr_col = off
    zcol = {name: 3 * d + o for name, o in wcol.items()}
    zcol["mg"] = 0

    rope_tabs = _rope_tables(seq)
    cc = jnp.zeros((8, d), F32).at[:nb].set(c).at[nb].set(c_ctx)
    x2d = x.reshape(nb * seq, d)
    ctx2d = ctx.reshape(nb * lc, d)

    w_in_t = jnp.swapaxes(w_in, 1, 2)
    wbp, wbd, wbg, wo = (w.astype(BF16) for w in (w_branch_pool, w_branch_diff, w_branch_gla, w_out))
    norm_g3 = norm_g.reshape(depth, 1, d)
    b_ada3 = b_ada.reshape(depth, 1, 3 * d)
    pool_scale3 = pool_scale.reshape(depth, 1, pw)
    gains3 = jnp.stack([jnp.tile(diff_q_norm, (1, 2)), jnp.tile(diff_k_norm, (1, 2)), diff_subln], axis=1)
    lam4 = jnp.stack([diff_lam_q1, diff_lam_k1, diff_lam_q2, diff_lam_k2], axis=1)
    gla_gain3 = gla_norm.reshape(depth, 1, GLA_DV)
    per_head = lambda w: w.reshape(depth, GLA_RANK, GLA_HEADS, GLA_DK).transpose(0, 2, 1, 3)
    w2 = jnp.zeros((depth, GLA_HEADS, LANES, 2 * GLA_DK), F32)
    w2 = w2.at[:, :, :GLA_RANK, :GLA_DK].set(per_head(gla_w_gate_f))
    w2 = w2.at[:, :, GLA_RANK:2 * GLA_RANK, GLA_DK:].set(per_head(gla_w_gate_b)).astype(BF16)
    b2 = jnp.concatenate([gla_b_gate_f.reshape(depth, GLA_HEADS, 1, GLA_DK),
                          gla_b_gate_b.reshape(depth, GLA_HEADS, 1, GLA_DK)], axis=-1)

    rows_gcd = math.gcd(seq, nb * lc)
    tm_pre = _largest_tile(rows_gcd, 512, 16)
    tm_in = _largest_tile(rows_gcd, 1024, 16)
    tn_in = _largest_tile(math.gcd(lr_col, 3 * d), 1536, 256)
    row_x = lambda tm: (lambda i: (i * tm) // seq)
    row_c = lambda i: nb
    kv_cols = ((zcol["dk"], zcol["dg"]), (zcol["gk"], zcol["gg"]))
    kv_tiles = tuple(t for t in range((3 * d + lr_col) // tn_in)
                     if any(lo < (t + 1) * tn_in and t * tn_in < hi for lo, hi in kv_cols))
    ccol = zcol

    for l in range(depth):
        last = l == depth - 1
        lambda_init = 0.8 - 0.6 * math.exp(-0.3 * l)
        mod3 = _ada(cc, w_ada, b_ada3, l).reshape(8, 3, d)
        h, lr, lrc = _prenorm(x2d, ctx2d, mod3, norm_g3, w_in_t, l, lr_col, seq, nb, tm_pre)
        z, zc = _inproj(h, w_in_t, l, lr_col, 3 * d, nb * seq, tm_in, tn_in, kv_tiles if last else None)
        z3, zc3 = z.reshape(nb, seq, -1), zc.reshape(nb, lc, -1)
        lr3, lrc3 = lr.reshape(nb, seq, LANES), lrc.reshape(nb, lc, LANES)

        pool_l = _pool(z3, zcol["pu"], zcol["pg"], pool_w, pool_scale3, l, _largest_tile(seq, 512, 16))
        diff_l = _attn(z3, zcol["dq"], zcol["dg"], zc3, ccol["dk"], ccol["dv"], z3, zcol["dk"], zcol["dv"],
                       rope_tabs, gains3, lam4, l, lambda_init, _largest_tile(seq, 2048, 16))
        gla_l, gla_c = _gla(zc3, (ccol["gq"], ccol["gk"], ccol["gv"], ccol["gg"]), lrc3,
                            z3, (zcol["gq"], zcol["gk"], zcol["gv"], zcol["gg"]), lr3,
                            w2, b2, gla_gain3, l, ctx_out=not last)

        tm_m = _largest_tile(seq, 512, 16)
        y = _merge(pool_l.reshape(nb * seq, pw), diff_l.reshape(nb * seq, dw), gla_l.reshape(nb * seq, gvw),
                   z, zcol["mg"], wbp, wbd, wbg, l, tm_m)
        x2d_new = _outproj(y, wo, x2d, mod3, l, row_x(tm_m), tm_m)

        if not last:
            pool_c = _pool(zc3, zcol["pu"], zcol["pg"], pool_w, pool_scale3, l, _largest_tile(lc, 512, 16))
            diff_c = _attn(zc3, zcol["dq"], zcol["dg"], zc3, zcol["dk"], zcol["dv"], None, None, None,
                           None, gains3, lam4, l, lambda_init, _largest_tile(lc, 256, 16))
            tm_mc = _largest_tile(nb * lc, 512, 16)
            y_c = _merge(pool_c.reshape(nb * lc, pw), diff_c.reshape(nb * lc, dw), gla_c.reshape(nb * lc, gvw),
                         zc, zcol["mg"], wbp, wbd, wbg, l, tm_mc)
            ctx2d = _outproj(y_c, wo, ctx2d, mod3, l, row_c, tm_mc)
        x2d = x2d_new

    return x2d.reshape(nb, seq, d)
```

```python
import functools
import math

import jax
import jax.numpy as jnp
import numpy as np
from jax import lax
from jax.experimental import pallas as pl
from jax.experimental.pallas import tpu as pltpu

F32 = jnp.float32
BF16 = jnp.bfloat16

EPS = 1e-6
GRID_W = 64
ROPE_THETA = 10000.0

POOL_WINDOWS = (2, 4, 8, 16)
POOL_HALO = 16
DIFF_HEADS = 8
DIFF_HEAD_DIM = 64
GLA_HEADS = 4
GLA_DK = 128
GLA_DV = 256
GLA_RANK = 16
GLA_NORMALIZER = 16.0
GLA_CHUNK = 256
LANES = 128
LOG2E = math.log2(math.e)

VMEM_LIMIT = 48 * 1024 * 1024


VMEM_LIMIT_WIDE = 58 * 1024 * 1024


def _cparams(*sem, vmem=VMEM_LIMIT):
    return pltpu.CompilerParams(dimension_semantics=sem, vmem_limit_bytes=vmem)


def _sigmoid(x):
    return 0.5 * jnp.tanh(0.5 * x) + 0.5


def _silu(x):
    return x * _sigmoid(x)


def _dot(a, b):
    return jnp.dot(a, b, preferred_element_type=F32)


def _dot_nt(a, b):
    return lax.dot_general(a, b, (((1,), (1,)), ((), ())), preferred_element_type=F32)


def _dot_tn(a, b):
    return lax.dot_general(a, b, (((0,), (0,)), ((), ())), preferred_element_type=F32)


def _ada_kernel(cc_ref, w_ref, b_ref, o_ref):
    @pl.when(pl.program_id(0) == 0)
    def _():
        o_ref[...] = jnp.broadcast_to(b_ref[...], o_ref.shape)

    a = _silu(cc_ref[...]).astype(BF16)
    o_ref[...] += _dot(a, w_ref[...].astype(BF16))


def _ada(cc, w_ada, b_ada3, l, tk=256):
    rows, d = cc.shape
    n = w_ada.shape[2]
    return pl.pallas_call(
        _ada_kernel,
        grid=(d // tk,),
        in_specs=[pl.BlockSpec((rows, tk), lambda k: (0, k)),
                  pl.BlockSpec((None, tk, n), lambda k: (l, k, 0)),
                  pl.BlockSpec((None, 1, n), lambda k: (l, 0, 0))],
        out_specs=pl.BlockSpec((rows, n), lambda k: (0, 0)),
        out_shape=jax.ShapeDtypeStruct((rows, n), F32),
        compiler_params=_cparams("arbitrary"),
        name="ada",
    )(cc, w_ada, b_ada3)


def _prenorm_kernel(x_ref, c_ref, mod_ref, g_ref, wlr_ref, h_ref, lrx_ref, lrc_ref, *, n_x):
    def emit(src_ref, lr_ref):
        x = src_ref[...]
        ms = jnp.mean(x * x, axis=-1, keepdims=True)
        y = x * lax.rsqrt(ms + EPS) * g_ref[...]
        h = (y * (1.0 + mod_ref[1:2, :]) + mod_ref[0:1, :]).astype(BF16)
        h_ref[...] = h
        lr_ref[...] = _dot_nt(h, wlr_ref[...].astype(BF16)).astype(lr_ref.dtype)

    @pl.when(pl.program_id(0) < n_x)
    def _():
        emit(x_ref, lrx_ref)

    @pl.when(pl.program_id(0) >= n_x)
    def _():
        emit(c_ref, lrc_ref)


def _prenorm(x2d, ctx2d, mod3, norm_g3, w_in_t, l, lr_col, seq, nb, tm):
    mx, d = x2d.shape
    mc = ctx2d.shape[0]
    n_x, n_c = mx // tm, mc // tm
    return pl.pallas_call(
        functools.partial(_prenorm_kernel, n_x=n_x),
        grid=(n_x + n_c,),
        in_specs=[pl.BlockSpec((tm, d), lambda i: (jnp.minimum(i, n_x - 1), 0)),
                  pl.BlockSpec((tm, d), lambda i: (jnp.maximum(i - n_x, 0), 0)),
                  pl.BlockSpec((None, 3, d), lambda i: (jnp.where(i < n_x, (i * tm) // seq, nb), 0, 0)),
                  pl.BlockSpec((None, 1, d), lambda i: (l, 0, 0)),
                  pl.BlockSpec((None, LANES, d), lambda i: (l, lr_col // LANES, 0))],
        out_specs=[pl.BlockSpec((tm, d), lambda i: (i, 0)),
                   pl.BlockSpec((tm, LANES), lambda i: (jnp.minimum(i, n_x - 1), 0)),
                   pl.BlockSpec((tm, LANES), lambda i: (jnp.maximum(i - n_x, 0), 0))],
        out_shape=[jax.ShapeDtypeStruct((mx + mc, d), BF16),
                   jax.ShapeDtypeStruct((mx, LANES), BF16),
                   jax.ShapeDtypeStruct((mc, LANES), BF16)],
        compiler_params=_cparams("arbitrary"),
        name="prenorm",
    )(x2d, ctx2d, mod3, norm_g3, w_in_t)


def _inproj_kernel(h_ref, wa_ref, wb_ref, zx_ref, zc_ref, wbf_ref, *, n_x, n_shift, shift, ctx_tiles):
    j, i = pl.program_id(0), pl.program_id(1)

    @pl.when(i == 0)
    def _():
        @pl.when(j < n_shift)
        def _():
            wbf_ref[...] = jnp.concatenate([wa_ref[shift:, :], wb_ref[...]], axis=0).astype(BF16)

        @pl.when(j >= n_shift)
        def _():
            wbf_ref[...] = wa_ref[...].astype(BF16)

    @pl.when(i < n_x)
    def _():
        zx_ref[...] = _dot_nt(h_ref[...], wbf_ref[...]).astype(zx_ref.dtype)

    @pl.when(i >= n_x)
    def _():
        if ctx_tiles is None:
            zc_ref[...] = _dot_nt(h_ref[...], wbf_ref[...]).astype(zc_ref.dtype)
        else:
            needed = functools.reduce(jnp.logical_or, [j == t for t in ctx_tiles])

            @pl.when(needed)
            def _():
                zc_ref[...] = _dot_nt(h_ref[...], wbf_ref[...]).astype(zc_ref.dtype)

            @pl.when(jnp.logical_not(needed))
            def _():
                zc_ref[...] = jnp.zeros_like(zc_ref)


def _inproj(h, w_in_t, l, lr_col, n_gate, mx, tm, tn, ctx_tiles):
    m, d = h.shape
    n_x = mx // tm
    n_shift = n_gate // tn
    shift = w_in_t.shape[1] - lr_col - n_gate
    assert shift % 8 == 0 and lr_col % tn == 0 and tn % shift == 0
    first_gate = lr_col // tn
    return pl.pallas_call(
        functools.partial(_inproj_kernel, n_x=n_x, n_shift=n_shift, shift=shift, ctx_tiles=ctx_tiles),
        grid=(n_shift + first_gate, m // tm),
        in_specs=[pl.BlockSpec((tm, d), lambda j, i: (i, 0)),
                  pl.BlockSpec((None, tn, d), lambda j, i: (l, jnp.where(j < n_shift, first_gate + j, j - n_shift), 0)),
                  pl.BlockSpec((None, shift, d),
                               lambda j, i: (l, jnp.where(j < n_shift, (lr_col + (j + 1) * tn) // shift, 0), 0))],
        out_specs=[pl.BlockSpec((tm, tn), lambda j, i: (jnp.minimum(i, n_x - 1), j)),
                   pl.BlockSpec((tm, tn), lambda j, i: (jnp.maximum(i - n_x, 0), j))],
        out_shape=[jax.ShapeDtypeStruct((mx, n_gate + lr_col), BF16),
                   jax.ShapeDtypeStruct((m - mx, n_gate + lr_col), BF16)],
        scratch_shapes=[pltpu.VMEM((tn, d), BF16)],
        compiler_params=_cparams("arbitrary", "arbitrary", vmem=VMEM_LIMIT_WIDE),
        name="inproj",
    )(h, w_in_t, w_in_t)


def _pool_bands(tile):
    t = np.arange(tile)[:, None]
    s = np.arange(tile + 2 * POOL_HALO)[None, :] - POOL_HALO
    return np.stack([(s >= t - w // 2) & (s < t + w // 2) for w in POOL_WINDOWS])


def _pool_kernel(up_ref, uc_ref, un_ref, pg_ref, band_ref, w_ref, sc_ref, o_ref, *, tile, seq_len):
    base = pl.program_id(1) * tile
    gw = w_ref.shape[-1]
    tc = base + lax.broadcasted_iota(jnp.int32, (tile, 1), 0)
    up = jnp.where(base > 0, up_ref[...], jnp.zeros_like(up_ref))
    un = jnp.where(base + tile < seq_len, un_ref[...], jnp.zeros_like(un_ref))
    for g, window in enumerate(POOL_WINDOWS):
        half = window // 2
        cols = slice(g * gw, (g + 1) * gw)
        u_all = jnp.concatenate([up[:, cols], uc_ref[:, cols], un[:, cols]], axis=0)
        wsum = _dot(band_ref[g], u_all)
        cnt = (jnp.minimum(tc + half, seq_len) - jnp.maximum(tc - half, 0)).astype(F32)
        dcen = wsum / cnt - uc_ref[:, cols].astype(F32)
        y = _dot(dcen.astype(BF16), w_ref[g].astype(BF16)) * sc_ref[:, cols]
        o_ref[:, cols] = (y * _silu(pg_ref[:, cols].astype(F32))).astype(o_ref.dtype)


def _pool(z3, col_u, col_g, pool_w, pool_scale3, l, tile):
    nseq, seq_len, _ = z3.shape
    ngrp = len(POOL_WINDOWS)
    gw = pool_w.shape[-1]
    pw = ngrp * gw
    cu, cg = col_u // pw, col_g // pw
    nhalo = seq_len // POOL_HALO
    per = tile // POOL_HALO
    kern = functools.partial(_pool_kernel, tile=tile, seq_len=seq_len)
    bands = jnp.asarray(_pool_bands(tile), BF16)
    return pl.pallas_call(
        kern,
        grid=(nseq, seq_len // tile),
        in_specs=[
            pl.BlockSpec((None, POOL_HALO, pw), lambda s, r: (s, jnp.maximum(r * per - 1, 0), cu)),
            pl.BlockSpec((None, tile, pw), lambda s, r: (s, r, cu)),
            pl.BlockSpec((None, POOL_HALO, pw), lambda s, r: (s, jnp.minimum((r + 1) * per, nhalo - 1), cu)),
            pl.BlockSpec((None, tile, pw), lambda s, r: (s, r, cg)),
            pl.BlockSpec(bands.shape, lambda s, r: (0, 0, 0)),
            pl.BlockSpec((None, ngrp, gw, gw), lambda s, r: (l, 0, 0, 0)),
            pl.BlockSpec((None, 1, pw), lambda s, r: (l, 0, 0)),
        ],
        out_specs=pl.BlockSpec((None, tile, pw), lambda s, r: (s, r, 0)),
        out_shape=jax.ShapeDtypeStruct((nseq, seq_len, pw), BF16),
        compiler_params=_cparams("arbitrary", "arbitrary"),
        name="pool",
    )(z3, z3, z3, z3, bands, pool_w, pool_scale3)


def _head_norm(t, gain, gmat):
    sq = t * t
    hi = sq.astype(BF16)
    lo = (sq - hi.astype(F32)).astype(BF16)
    ssq = _dot(hi, gmat) + _dot(lo, gmat)
    return t * lax.rsqrt(ssq * (1.0 / DIFF_HEAD_DIM) + EPS) * gain


def _rope(t, cos_t, sin_s, first_half):
    half = DIFF_HEAD_DIM // 2
    lower = pltpu.roll(t, half, 1)
    upper = pltpu.roll(t, LANES - half, 1)
    return t * cos_t + jnp.where(first_half, upper, lower) * sin_s


def _attn_kernel(*refs, tq, row_group, rope_q, has_latent, lambda_init):
    if has_latent:
        (q_ref, dg_ref, kc_ref, vc_ref, kl_ref, vl_ref, cosq_ref, sinq_ref, cosk_ref, sink_ref,
         gains_ref, lam_ref, o_ref, ks_ref, vs_ref) = refs
    else:
        (q_ref, dg_ref, kc_ref, vc_ref, gains_ref, lam_ref, o_ref, ks_ref, vs_ref) = refs
    hw = 2 * DIFF_HEAD_DIM
    lc = kc_ref.shape[0]
    q_gain, k_gain, sub_gain = gains_ref[0:1, :], gains_ref[1:2, :], gains_ref[2:3, :]

    lane = lax.broadcasted_iota(jnp.int32, (1, LANES), 1)
    first_half = (lane & (DIFF_HEAD_DIM - 1)) < (DIFF_HEAD_DIM // 2)
    sub1 = lane < DIFF_HEAD_DIM
    gi = jnp.where(lax.broadcasted_iota(jnp.int32, (LANES, LANES), 0) < DIFF_HEAD_DIM, 1.0, 0.0)
    gj = jnp.where(lax.broadcasted_iota(jnp.int32, (LANES, LANES), 1) < DIFF_HEAD_DIM, 1.0, 0.0)
    gmat = (gi * gj + (1.0 - gi) * (1.0 - gj)).astype(BF16)

    @pl.when(pl.program_id(2) == 0)
    def _():
        ks_ref[:lc, :] = _head_norm(kc_ref[...].astype(F32), k_gain, gmat).astype(BF16)
        vs_ref[:lc, :hw] = vc_ref[...]
        vs_ref[:, hw:] = jnp.ones((vs_ref.shape[0], hw), BF16)
        if has_latent:
            kl = _head_norm(kl_ref[...].astype(F32), k_gain, gmat)
            ks_ref[lc:, :] = _rope(kl, cosk_ref[...], sink_ref[...], first_half).astype(BF16)
            vs_ref[lc:, :hw] = vl_ref[...]

    q = _head_norm(q_ref[...].astype(F32), q_gain, gmat)
    if rope_q:
        q = _rope(q, cosq_ref[...], sinq_ref[...], first_half)
    q = q * (DIFF_HEAD_DIM ** -0.5 * LOG2E)
    qq = jnp.concatenate([jnp.where(sub1, q, 0.0), jnp.where(sub1, 0.0, q)], axis=0).astype(BF16)

    pvs = []
    for r0 in range(0, 2 * tq, row_group):
        s = _dot_nt(qq[r0:r0 + row_group], ks_ref[...])
        e = jnp.exp2(s - jnp.max(s, axis=-1, keepdims=True)).astype(BF16)
        acc = _dot(e, vs_ref[...])
        pvs.append(acc[:, :hw] / acc[:, hw:hw + 1])
    pv = jnp.concatenate(pvs, axis=0)

    lam_p = lam_ref[...]
    lam = (jnp.exp(jnp.sum(lam_p[0:1] * lam_p[1:2], axis=-1, keepdims=True))
           - jnp.exp(jnp.sum(lam_p[2:3] * lam_p[3:4], axis=-1, keepdims=True)) + lambda_init)
    o = pv[:tq] - lam * pv[tq:]

    ms = jnp.mean(o * o, axis=-1, keepdims=True)
    y = o * lax.rsqrt(ms + EPS) * sub_gain * (1.0 - lambda_init)
    o_ref[...] = (y * _silu(dg_ref[...].astype(F32))).astype(o_ref.dtype)


def _attn(zq3, cq, cdg, zc3, ckc, cvc, zl3, ckl, cvl, rope_tabs, gains3, lam4, l, lambda_init, tq):
    nb, lq, _ = zq3.shape
    lc = zc3.shape[1]
    has_latent = zl3 is not None
    hw = 2 * DIFF_HEAD_DIM
    b_of = lambda c: c // hw
    qmap = lambda off: (lambda b, h, i: (b, i, b_of(off) + h))
    kmap = lambda off: (lambda b, h, i: (b, 0, b_of(off) + h))
    const = lambda b, h, i: (0, 0)
    in_specs = [pl.BlockSpec((None, tq, hw), qmap(cq)),
                pl.BlockSpec((None, tq, hw), qmap(cdg)),
                pl.BlockSpec((None, lc, hw), kmap(ckc)),
                pl.BlockSpec((None, lc, hw), kmap(cvc))]
    args = [zq3, zq3, zc3, zc3]
    lk = lc + (zl3.shape[1] if has_latent else 0)
    scratch = [pltpu.VMEM((lk, hw), BF16), pltpu.VMEM((lk, 2 * hw), BF16)]
    if has_latent:
        ll = zl3.shape[1]
        cos_t, sin_s = rope_tabs
        in_specs += [pl.BlockSpec((None, ll, hw), kmap(ckl)),
                     pl.BlockSpec((None, ll, hw), kmap(cvl)),
                     pl.BlockSpec((tq, hw), lambda b, h, i: (i, 0)),
                     pl.BlockSpec((tq, hw), lambda b, h, i: (i, 0)),
                     pl.BlockSpec((ll, hw), const),
                     pl.BlockSpec((ll, hw), const)]
        args += [zl3, zl3, cos_t, sin_s, cos_t, sin_s]
    in_specs += [pl.BlockSpec((None, 3, hw), lambda b, h, i: (l, 0, 0)),
                 pl.BlockSpec((None, 4, DIFF_HEAD_DIM), lambda b, h, i: (l, 0, 0))]
    args += [gains3, lam4]
    kern = functools.partial(_attn_kernel, tq=tq, row_group=min(2 * tq, LANES), rope_q=has_latent, has_latent=has_latent,
                             lambda_init=lambda_init)
    return pl.pallas_call(
        kern,
        grid=(nb, DIFF_HEADS, lq // tq),
        in_specs=in_specs,
        out_specs=pl.BlockSpec((None, tq, hw), lambda b, h, i: (b, i, h)),
        out_shape=jax.ShapeDtypeStruct((nb, lq, DIFF_HEADS * hw), BF16),
        scratch_shapes=scratch,
        compiler_params=_cparams("arbitrary", "arbitrary", "arbitrary"),
        name="diff_attn",
    )(*args)


def _log_sigmoid(x):
    return jnp.minimum(x, 0.0) - jnp.log1p(jnp.exp(-jnp.abs(x)))


GLA_LEVELS = tuple(1 << i for i in range(GLA_CHUNK.bit_length() - 1))
GLA_TAB_LEVELS = tuple(m for m in GLA_LEVELS if m < 8)
GLA_SLOT_CUM = len(GLA_TAB_LEVELS)
GLA_ATT_ROWS = 32
GLA_UNROLL = 4
GLA_TOT_ROWS = 16


def _gla_boundary(m, rev):
    return m if rev else m - 1


@functools.lru_cache(maxsize=None)
def _gla_tables_np():
    c = GLA_CHUNK
    t = np.arange(c)[:, None]
    j = np.arange(c)[None, :]
    sums, masks = [], []
    for rev in (False, True):
        rows, mk = [], []
        for m in GLA_LEVELS:
            base = t & ~(2 * m - 1)
            bd = base + _gla_boundary(m, rev)
            is_q = ((t & m) == 0) if rev else ((t & m) != 0)
            if rev:
                between = np.where(is_q, (j >= t) & (j < bd), (j >= bd) & (j < t))
            else:
                between = np.where(is_q, (j > bd) & (j <= t), (j > t) & (j <= bd))
            if m in GLA_TAB_LEVELS:
                rows.append(between)
            mk.append((base == (j & ~(2 * m - 1))) & is_q & (((j & m) != 0) if rev else ((j & m) == 0)))
        rows.append(j >= t if rev else j <= t)
        rows.append(np.ones((GLA_TOT_ROWS, c), bool))
        mk.append(t == j)
        sums.append(np.concatenate(rows, axis=0))
        masks.append(np.stack(mk))
    return np.stack(sums), np.stack(masks)


def _gla_masks_np():
    return _gla_tables_np()[1]


def _gla_tables():
    sums, masks = _gla_tables_np()
    return jnp.asarray(sums, BF16), jnp.asarray(masks, F32)


def _gla_chunk(q, k, v, ghl, st, tab_ref, mask_ref, rev, need_out):
    c = GLA_CHUNK
    first = 0 if need_out else GLA_SLOT_CUM
    x2 = _dot(tab_ref[first * c:, :], ghl)
    x = x2[:, :GLA_DK] + x2[:, GLA_DK:]
    cum = x[(GLA_SLOT_CUM - first) * c:(GLA_SLOT_CUM - first + 1) * c]
    tot = x[-GLA_TOT_ROWS:1 - GLA_TOT_ROWS]
    kd = (k * jnp.exp2(tot - cum)).astype(BF16)
    st_new = st * jnp.exp2(tot) + _dot_tn(v, kd)
    if not need_out:
        return None, st_new

    out = _dot_nt((q * jnp.exp2(cum)).astype(BF16), st.astype(BF16))

    own = _gla_masks_np()[int(rev)]
    n_rb, n_lt = c // GLA_ATT_ROWS, c // LANES
    regions = [[None] * n_lt for _ in range(n_rb)]

    def accumulate(lvl, prod):
        for rb in range(n_rb):
            rs = slice(rb * GLA_ATT_ROWS, (rb + 1) * GLA_ATT_ROWS)
            for lt in range(n_lt):
                ls = slice(lt * LANES, (lt + 1) * LANES)
                if not own[lvl, rs, ls].any():
                    continue
                part = prod[rs, ls]
                if not own[lvl, rs, ls].all():
                    part = part * mask_ref[lvl, rs, ls]
                regions[rb][lt] = part if regions[rb][lt] is None else regions[rb][lt] + part

    accumulate(len(GLA_LEVELS), _dot_nt(q.astype(BF16), k.astype(BF16)))
    for lvl, m in enumerate(GLA_LEVELS):
        if m in GLA_TAB_LEVELS:
            xl = x[lvl * c:(lvl + 1) * c]
        else:
            anchor = jnp.concatenate(
                [jnp.broadcast_to(cum[b0 + _gla_boundary(m, rev):b0 + _gla_boundary(m, rev) + 1], (2 * m, GLA_DK))
                 for b0 in range(0, c, 2 * m)], axis=0)
            xl = -jnp.abs(cum - anchor)
        el = jnp.exp2(xl)
        accumulate(lvl, _dot_nt((q * el).astype(BF16), (k * el).astype(BF16)))
    zero = jnp.zeros((GLA_ATT_ROWS, LANES), F32)
    att = jnp.concatenate([jnp.concatenate([zero if r is None else r for r in row], axis=1) for row in regions], axis=0)
    return out + _dot(att.astype(BF16), v), st_new


def _gla_kernel(*refs, n_ctx, n_lat, ctx_out):
    (qc_ref, kc_ref, vc_ref, ggc_ref, lrc_ref, ql_ref, kl_ref, vl_ref, ggl_ref, lrl_ref,
     w2_ref, b2_ref, gain_ref, tab_ref, mask_ref) = refs[:15]
    rest = refs[15:]
    if ctx_out:
        ol_ref, oc_ref, gl_ref, gc_ref, accl_ref, accc_ref, stf_ref, stb_ref = rest
    else:
        ol_ref, gl_ref, gc_ref, accl_ref, stf_ref, stb_ref = rest
        oc_ref = accc_ref = None

    def decays(lr_ref, g_ref):
        g = _log_sigmoid(_dot(lr_ref[...], w2_ref[...]) + b2_ref[...]) * (LOG2E / GLA_NORMALIZER)
        hi = g.astype(BF16)
        lo = (g - hi.astype(F32)).astype(BF16)
        for d in range(2):
            g_ref[:, (2 * d) * GLA_DK:(2 * d + 1) * GLA_DK] = hi[:, d * GLA_DK:(d + 1) * GLA_DK]
            g_ref[:, (2 * d + 1) * GLA_DK:(2 * d + 2) * GLA_DK] = lo[:, d * GLA_DK:(d + 1) * GLA_DK]

    decays(lrl_ref, gl_ref)
    decays(lrc_ref, gc_ref)
    stf_ref[...] = jnp.zeros_like(stf_ref)
    stb_ref[...] = jnp.zeros_like(stb_ref)
    scale = GLA_DK ** -0.5

    def scan(n, q_ref, k_ref, v_ref, g_ref, acc_ref):
        need_out = acc_ref is not None

        def step(i, assign_up, assign_down):
            for c_idx, d, st_ref, first_visit in ((i, 0, stf_ref, assign_up), (n - 1 - i, 1, stb_ref, assign_down)):
                rows = pl.ds(pl.multiple_of(c_idx * GLA_CHUNK, GLA_CHUNK), GLA_CHUNK)
                q = q_ref[rows, :].astype(F32) * scale
                k = k_ref[rows, :].astype(F32)
                ghl = g_ref[rows, 2 * d * GLA_DK:(2 * d + 2) * GLA_DK]
                out, st_new = _gla_chunk(q, k, v_ref[rows, :], ghl, st_ref[...], tab_ref.at[d], mask_ref.at[d],
                                         d == 1, need_out)
                st_ref[...] = st_new
                if need_out:
                    if first_visit:
                        acc_ref[rows, :] = out
                    else:
                        acc_ref[rows, :] += out

        def first(i, carry):
            step(i, True, True)
            return carry

        def second(i, carry):
            step(i, False, False)
            return carry

        if n // 2:
            lax.fori_loop(0, n // 2, first, 0, unroll=min(GLA_UNROLL, n // 2))
        if n % 2:
            step(n // 2, True, False)
        if n // 2:
            lax.fori_loop((n + 1) // 2, n, second, 0, unroll=min(GLA_UNROLL, n // 2))

    def finish(acc_ref, gg_ref, o_ref):
        o = acc_ref[...]
        ms = jnp.mean(o * o, axis=-1, keepdims=True)
        y = o * lax.rsqrt(ms + EPS) * gain_ref[...]
        o_ref[...] = (y * _silu(gg_ref[...].astype(F32))).astype(o_ref.dtype)

    scan(n_ctx, qc_ref, kc_ref, vc_ref, gc_ref, accc_ref)
    scan(n_lat, ql_ref, kl_ref, vl_ref, gl_ref, accl_ref)
    finish(accl_ref, ggl_ref, ol_ref)
    if ctx_out:
        finish(accc_ref, ggc_ref, oc_ref)


def _gla(zc3, cc, lrc3, zl3, cl, lrl3, w2, b2, gain3, l, ctx_out):
    nb, lc, _ = zc3.shape
    ll = zl3.shape[1]
    assert lc % GLA_CHUNK == 0 and ll % GLA_CHUNK == 0
    tabs, masks = _gla_tables()

    def seq_specs(z3, cols, lr3, slen):
        qo, ko, vo, go = cols
        return ([pl.BlockSpec((None, slen, GLA_DK), lambda b, h: (b, 0, qo // GLA_DK + h)),
                 pl.BlockSpec((None, slen, GLA_DK), lambda b, h: (b, 0, ko // GLA_DK + h)),
                 pl.BlockSpec((None, slen, GLA_DV), lambda b, h: (b, 0, vo // GLA_DV + h)),
                 pl.BlockSpec((None, slen, GLA_DV), lambda b, h: (b, 0, (0 if go is None else go) // GLA_DV + h)),
                 pl.BlockSpec((None, slen, LANES), lambda b, h: (b, 0, 0))],
                [z3, z3, z3, z3, lr3])

    sc, ac = seq_specs(zc3, cc, lrc3, lc)
    sl, al = seq_specs(zl3, cl, lrl3, ll)
    in_specs = sc + sl + [pl.BlockSpec((None, None, LANES, 2 * GLA_DK), lambda b, h: (l, h, 0, 0)),
                          pl.BlockSpec((None, None, 1, 2 * GLA_DK), lambda b, h: (l, h, 0, 0)),
                          pl.BlockSpec((None, 1, GLA_DV), lambda b, h: (l, 0, 0)),
                          pl.BlockSpec(tabs.shape, lambda b, h: (0, 0, 0)),
                          pl.BlockSpec(masks.shape, lambda b, h: (0, 0, 0, 0))]
    args = ac + al + [w2, b2, gain3, tabs, masks]
    out_specs = [pl.BlockSpec((None, ll, GLA_DV), lambda b, h: (b, 0, h))]
    out_shape = [jax.ShapeDtypeStruct((nb, ll, GLA_HEADS * GLA_DV), BF16)]
    scratch = [pltpu.VMEM((ll, 4 * GLA_DK), BF16), pltpu.VMEM((lc, 4 * GLA_DK), BF16),
               pltpu.VMEM((ll, GLA_DV), F32)]
    if ctx_out:
        out_specs.append(pl.BlockSpec((None, lc, GLA_DV), lambda b, h: (b, 0, h)))
        out_shape.append(jax.ShapeDtypeStruct((nb, lc, GLA_HEADS * GLA_DV), BF16))
        scratch.append(pltpu.VMEM((lc, GLA_DV), F32))
    scratch += [pltpu.VMEM((GLA_DV, GLA_DK), F32), pltpu.VMEM((GLA_DV, GLA_DK), F32)]
    kern = functools.partial(_gla_kernel, n_ctx=lc // GLA_CHUNK, n_lat=ll // GLA_CHUNK, ctx_out=ctx_out)
    outs = pl.pallas_call(
        kern,
        grid=(nb, GLA_HEADS),
        in_specs=in_specs,
        out_specs=out_specs,
        out_shape=out_shape,
        scratch_shapes=scratch,
        compiler_params=_cparams("arbitrary", "arbitrary"),
        name="gla",
    )(*args)
    return (outs[0], outs[1]) if ctx_out else (outs[0], None)


def _merge_kernel(p_ref, d_ref, g_ref, wp_ref, wd_ref, wg_ref, mp_ref, md_ref, mg_ref, y_ref):
    y = (_sigmoid(mp_ref[...].astype(F32)) * _dot(p_ref[...], wp_ref[...])
         + _sigmoid(md_ref[...].astype(F32)) * _dot(d_ref[...], wd_ref[...])
         + _sigmoid(mg_ref[...].astype(F32)) * _dot(g_ref[...], wg_ref[...]))
    y_ref[...] = y.astype(y_ref.dtype)


def _merge(pool_o, diff_o, gla_o, z2d, col_mg, wbp, wbd, wbg, l, tm):
    m, kw = pool_o.shape
    d = wbp.shape[2]
    act = pl.BlockSpec((tm, kw), lambda i: (i, 0))
    wsp = pl.BlockSpec((None, kw, d), lambda i: (l, 0, 0), pipeline_mode=pl.Buffered(1))
    gate = lambda k: pl.BlockSpec((tm, d), lambda i: (i, col_mg // d + k))
    return pl.pallas_call(
        _merge_kernel,
        grid=(m // tm,),
        in_specs=[act, act, act, wsp, wsp, wsp, gate(0), gate(1), gate(2)],
        out_specs=pl.BlockSpec((tm, d), lambda i: (i, 0)),
        out_shape=jax.ShapeDtypeStruct((m, d), BF16),
        compiler_params=_cparams("arbitrary"),
        name="merge",
    )(pool_o, diff_o, gla_o, wbp, wbd, wbg, z2d, z2d, z2d)


def _outproj_kernel(y_ref, w_ref, x_ref, mod_ref, o_ref):
    o_ref[...] = x_ref[...] + mod_ref[2:3, :] * _dot(y_ref[...], w_ref[...])


def _outproj(y, w_out, x2d, mod3, l, row_of_tile, tm):
    m, d = x2d.shape
    return pl.pallas_call(
        _outproj_kernel,
        grid=(m // tm,),
        in_specs=[pl.BlockSpec((tm, d), lambda i: (i, 0)),
                  pl.BlockSpec((None, d, d), lambda i: (l, 0, 0), pipeline_mode=pl.Buffered(1)),
                  pl.BlockSpec((tm, d), lambda i: (i, 0)),
                  pl.BlockSpec((None, 3, d), lambda i: (row_of_tile(i), 0, 0))],
        out_specs=pl.BlockSpec((tm, d), lambda i: (i, 0)),
        out_shape=jax.ShapeDtypeStruct((m, d), F32),
        compiler_params=_cparams("arbitrary"),
        name="outproj",
    )(y, w_out, x2d, mod3)


def _rope_tables(seq_len):
    n_freq = DIFF_HEAD_DIM // 4
    t = jnp.arange(seq_len)
    inv = ROPE_THETA ** (-jnp.arange(n_freq, dtype=F32) / n_freq)
    ang = jnp.concatenate([(t // GRID_W).astype(F32)[:, None] * inv,
                           (t % GRID_W).astype(F32)[:, None] * inv], axis=-1)
    cos, sin = jnp.cos(ang), jnp.sin(ang)
    return jnp.tile(cos, (1, 4)), jnp.tile(jnp.concatenate([-sin, sin], axis=-1), (1, 2))


def _largest_tile(n, cap, mult):
    t = min(n, cap)
    while n % t or t % mult:
        t -= mult
    return t


def kernel(x, c, ctx, c_ctx, norm_g, w_ada, b_ada, w_in, pool_w, pool_scale, diff_q_norm, diff_k_norm, diff_lam_q1, diff_lam_k1, diff_lam_q2, diff_lam_k2, diff_subln, gla_w_gate_f, gla_b_gate_f, gla_w_gate_b, gla_b_gate_b, gla_norm, w_branch_pool, w_branch_diff, w_branch_gla, w_out):
    nb, seq, d = x.shape
    lc = ctx.shape[1]
    depth = w_in.shape[0]
    pw = pool_scale.shape[1]
    dw = DIFF_HEADS * 2 * DIFF_HEAD_DIM
    gkw, gvw = GLA_HEADS * GLA_DK, GLA_HEADS * GLA_DV

    sizes = dict(pu=pw, pg=pw, dq=dw, dk=dw, dv=dw, dg=dw, gq=gkw, gk=gkw, gv=gvw, gg=gvw)
    wcol, off = {}, 0
    for name, size in sizes.items():
        wcol[name] = off
        off += size
    lr_col = off
    zcol = {name: 3 * d + o for name, o in wcol.items()}
    zcol["mg"] = 0

    rope_tabs = _rope_tables(seq)
    cc = jnp.zeros((8, d), F32).at[:nb].set(c).at[nb].set(c_ctx)
    x2d = x.reshape(nb * seq, d)
    ctx2d = ctx.reshape(nb * lc, d)

    w_in_t = jnp.swapaxes(w_in, 1, 2)
    wbp, wbd, wbg, wo = (w.astype(BF16) for w in (w_branch_pool, w_branch_diff, w_branch_gla, w_out))
    norm_g3 = norm_g.reshape(depth, 1, d)
    b_ada3 = b_ada.reshape(depth, 1, 3 * d)
    pool_scale3 = pool_scale.reshape(depth, 1, pw)
    gains3 = jnp.stack([jnp.tile(diff_q_norm, (1, 2)), jnp.tile(diff_k_norm, (1, 2)), diff_subln], axis=1)
    lam4 = jnp.stack([diff_lam_q1, diff_lam_k1, diff_lam_q2, diff_lam_k2], axis=1)
    gla_gain3 = gla_norm.reshape(depth, 1, GLA_DV)
    per_head = lambda w: w.reshape(depth, GLA_RANK, GLA_HEADS, GLA_DK).transpose(0, 2, 1, 3)
    w2 = jnp.zeros((depth, GLA_HEADS, LANES, 2 * GLA_DK), F32)
    w2 = w2.at[:, :, :GLA_RANK, :GLA_DK].set(per_head(gla_w_gate_f))
    w2 = w2.at[:, :, GLA_RANK:2 * GLA_RANK, GLA_DK:].set(per_head(gla_w_gate_b)).astype(BF16)
    b2 = jnp.concatenate([gla_b_gate_f.reshape(depth, GLA_HEADS, 1, GLA_DK),
                          gla_b_gate_b.reshape(depth, GLA_HEADS, 1, GLA_DK)], axis=-1)

    rows_gcd = math.gcd(seq, nb * lc)
    tm_pre = _largest_tile(rows_gcd, 512, 16)
    tm_in = _largest_tile(rows_gcd, 1024, 16)
    tn_in = _largest_tile(math.gcd(lr_col, 3 * d), 1536, 256)
    row_x = lambda tm: (lambda i: (i * tm) // seq)
    row_c = lambda i: nb
    kv_cols = ((zcol["dk"], zcol["dg"]), (zcol["gk"], zcol["gg"]))
    kv_tiles = tuple(t for t in range((3 * d + lr_col) // tn_in)
                     if any(lo < (t + 1) * tn_in and t * tn_in < hi for lo, hi in kv_cols))
    ccol = zcol

    for l in range(depth):
        last = l == depth - 1
        lambda_init = 0.8 - 0.6 * math.exp(-0.3 * l)
        mod3 = _ada(cc, w_ada, b_ada3, l).reshape(8, 3, d)
        h, lr, lrc = _prenorm(x2d, ctx2d, mod3, norm_g3, w_in_t, l, lr_col, seq, nb, tm_pre)
        z, zc = _inproj(h, w_in_t, l, lr_col, 3 * d, nb * seq, tm_in, tn_in, kv_tiles if last else None)
        z3, zc3 = z.reshape(nb, seq, -1), zc.reshape(nb, lc, -1)
        lr3, lrc3 = lr.reshape(nb, seq, LANES), lrc.reshape(nb, lc, LANES)

        pool_l = _pool(z3, zcol["pu"], zcol["pg"], pool_w, pool_scale3, l, _largest_tile(seq, 512, 16))
        diff_l = _attn(z3, zcol["dq"], zcol["dg"], zc3, ccol["dk"], ccol["dv"], z3, zcol["dk"], zcol["dv"],
                       rope_tabs, gains3, lam4, l, lambda_init, _largest_tile(seq, 2048, 16))
        gla_l, gla_c = _gla(zc3, (ccol["gq"], ccol["gk"], ccol["gv"], ccol["gg"]), lrc3,
                            z3, (zcol["gq"], zcol["gk"], zcol["gv"], zcol["gg"]), lr3,
                            w2, b2, gla_gain3, l, ctx_out=not last)

        tm_m = _largest_tile(seq, 512, 16)
        y = _merge(pool_l.reshape(nb * seq, pw), diff_l.reshape(nb * seq, dw), gla_l.reshape(nb * seq, gvw),
                   z, zcol["mg"], wbp, wbd, wbg, l, tm_m)
        x2d_new = _outproj(y, wo, x2d, mod3, l, row_x(tm_m), tm_m)

        if not last:
            pool_c = _pool(zc3, zcol["pu"], zcol["pg"], pool_w, pool_scale3, l, _largest_tile(lc, 512, 16))
            diff_c = _attn(zc3, zcol["dq"], zcol["dg"], zc3, zcol["dk"], zcol["dv"], None, None, None,
                           None, gains3, lam4, l, lambda_init, _largest_tile(lc, 256, 16))
            tm_mc = _largest_tile(nb * lc, 512, 16)
            y_c = _merge(pool_c.reshape(nb * lc, pw), diff_c.reshape(nb * lc, dw), gla_c.reshape(nb * lc, gvw),
                         zc, zcol["mg"], wbp, wbd, wbg, l, tm_mc)
            ctx2d = _outproj(y_c, wo, ctx2d, mod3, l, row_c, tm_mc)
        x2d = x2d_new

    return x2d.reshape(nb, seq, d)
```

```python
import functools
import math

import jax
import jax.numpy as jnp
import numpy as np
from jax import lax
from jax.experimental import pallas as pl
from jax.experimental.pallas import tpu as pltpu

F32 = jnp.float32
BF16 = jnp.bfloat16

EPS = 1e-6
GRID_W = 64
ROPE_THETA = 10000.0

POOL_WINDOWS = (2, 4, 8, 16)
POOL_HALO = 16
DIFF_HEADS = 8
DIFF_HEAD_DIM = 64
GLA_HEADS = 4
GLA_DK = 128
GLA_DV = 256
GLA_RANK = 16
GLA_NORMALIZER = 16.0
GLA_CHUNK = 256
LANES = 128
LOG2E = math.log2(math.e)

VMEM_LIMIT = 48 * 1024 * 1024


VMEM_LIMIT_WIDE = 58 * 1024 * 1024


def _cparams(*sem, vmem=VMEM_LIMIT):
    return pltpu.CompilerParams(dimension_semantics=sem, vmem_limit_bytes=vmem)


def _sigmoid(x):
    return 0.5 * jnp.tanh(0.5 * x) + 0.5


def _silu(x):
    return x * _sigmoid(x)


def _dot(a, b):
    return jnp.dot(a, b, preferred_element_type=F32)


def _dot_nt(a, b):
    return lax.dot_general(a, b, (((1,), (1,)), ((), ())), preferred_element_type=F32)


def _dot_tn(a, b):
    return lax.dot_general(a, b, (((0,), (0,)), ((), ())), preferred_element_type=F32)


def _ada_kernel(cc_ref, w_ref, b_ref, o_ref):
    @pl.when(pl.program_id(0) == 0)
    def _():
        o_ref[...] = jnp.broadcast_to(b_ref[...], o_ref.shape)

    a = _silu(cc_ref[...]).astype(BF16)
    o_ref[...] += _dot(a, w_ref[...].astype(BF16))


def _ada(cc, w_ada, b_ada3, l, tk=256):
    rows, d = cc.shape
    n = w_ada.shape[2]
    return pl.pallas_call(
        _ada_kernel,
        grid=(d // tk,),
        in_specs=[pl.BlockSpec((rows, tk), lambda k: (0, k)),
                  pl.BlockSpec((None, tk, n), lambda k: (l, k, 0)),
                  pl.BlockSpec((None, 1, n), lambda k: (l, 0, 0))],
        out_specs=pl.BlockSpec((rows, n), lambda k: (0, 0)),
        out_shape=jax.ShapeDtypeStruct((rows, n), F32),
        compiler_params=_cparams("arbitrary"),
        name="ada",
    )(cc, w_ada, b_ada3)


def _prenorm_kernel(x_ref, c_ref, mod_ref, g_ref, wlr_ref, h_ref, lrx_ref, lrc_ref, *, n_x):
    def emit(src_ref, lr_ref):
        x = src_ref[...]
        ms = jnp.mean(x * x, axis=-1, keepdims=True)
        y = x * lax.rsqrt(ms + EPS) * g_ref[...]
        h = (y * (1.0 + mod_ref[1:2, :]) + mod_ref[0:1, :]).astype(BF16)
        h_ref[...] = h
        lr_ref[...] = _dot_nt(h, wlr_ref[...].astype(BF16)).astype(lr_ref.dtype)

    @pl.when(pl.program_id(0) < n_x)
    def _():
        emit(x_ref, lrx_ref)

    @pl.when(pl.program_id(0) >= n_x)
    def _():
        emit(c_ref, lrc_ref)


def _prenorm(x2d, ctx2d, mod3, norm_g3, w_in_t, l, lr_col, seq, nb, tm):
    mx, d = x2d.shape
    mc = ctx2d.shape[0]
    n_x, n_c = mx // tm, mc // tm
    return pl.pallas_call(
        functools.partial(_prenorm_kernel, n_x=n_x),
        grid=(n_x + n_c,),
        in_specs=[pl.BlockSpec((tm, d), lambda i: (jnp.minimum(i, n_x - 1), 0)),
                  pl.BlockSpec((tm, d), lambda i: (jnp.maximum(i - n_x, 0), 0)),
                  pl.BlockSpec((None, 3, d), lambda i: (jnp.where(i < n_x, (i * tm) // seq, nb), 0, 0)),
                  pl.BlockSpec((None, 1, d), lambda i: (l, 0, 0)),
                  pl.BlockSpec((None, LANES, d), lambda i: (l, lr_col // LANES, 0))],
        out_specs=[pl.BlockSpec((tm, d), lambda i: (i, 0)),
                   pl.BlockSpec((tm, LANES), lambda i: (jnp.minimum(i, n_x - 1), 0)),
                   pl.BlockSpec((tm, LANES), lambda i: (jnp.maximum(i - n_x, 0), 0))],
        out_shape=[jax.ShapeDtypeStruct((mx + mc, d), BF16),
                   jax.ShapeDtypeStruct((mx, LANES), BF16),
                   jax.ShapeDtypeStruct((mc, LANES), BF16)],
        compiler_params=_cparams("arbitrary"),
        name="prenorm",
    )(x2d, ctx2d, mod3, norm_g3, w_in_t)


def _inproj_kernel(h_ref, wa_ref, wb_ref, zx_ref, zc_ref, wbf_ref, *, n_x, n_shift, shift, ctx_tiles):
    j, i = pl.program_id(0), pl.program_id(1)

    @pl.when(i == 0)
    def _():
        @pl.when(j < n_shift)
        def _():
            wbf_ref[...] = jnp.concatenate([wa_ref[shift:, :], wb_ref[...]], axis=0).astype(BF16)

        @pl.when(j >= n_shift)
        def _():
            wbf_ref[...] = wa_ref[...].astype(BF16)

    @pl.when(i < n_x)
    def _():
        zx_ref[...] = _dot_nt(h_ref[...], wbf_ref[...]).astype(zx_ref.dtype)

    @pl.when(i >= n_x)
    def _():
        if ctx_tiles is None:
            zc_ref[...] = _dot_nt(h_ref[...], wbf_ref[...]).astype(zc_ref.dtype)
        else:
            needed = functools.reduce(jnp.logical_or, [j == t for t in ctx_tiles])

            @pl.when(needed)
            def _():
                zc_ref[...] = _dot_nt(h_ref[...], wbf_ref[...]).astype(zc_ref.dtype)

            @pl.when(jnp.logical_not(needed))
            def _():
                zc_ref[...] = jnp.zeros_like(zc_ref)


def _inproj(h, w_in_t, l, lr_col, n_gate, mx, tm, tn, ctx_tiles):
    m, d = h.shape
    n_x = mx // tm
    n_shift = n_gate // tn
    shift = w_in_t.shape[1] - lr_col - n_gate
    assert shift % 8 == 0 and lr_col % tn == 0 and tn % shift == 0
    first_gate = lr_col // tn
    return pl.pallas_call(
        functools.partial(_inproj_kernel, n_x=n_x, n_shift=n_shift, shift=shift, ctx_tiles=ctx_tiles),
        grid=(n_shift + first_gate, m // tm),
        in_specs=[pl.BlockSpec((tm, d), lambda j, i: (i, 0)),
                  pl.BlockSpec((None, tn, d), lambda j, i: (l, jnp.where(j < n_shift, first_gate + j, j - n_shift), 0)),
                  pl.BlockSpec((None, shift, d),
                               lambda j, i: (l, jnp.where(j < n_shift, (lr_col + (j + 1) * tn) // shift, 0), 0))],
        out_specs=[pl.BlockSpec((tm, tn), lambda j, i: (jnp.minimum(i, n_x - 1), j)),
                   pl.BlockSpec((tm, tn), lambda j, i: (jnp.maximum(i - n_x, 0), j))],
        out_shape=[jax.ShapeDtypeStruct((mx, n_gate + lr_col), BF16),
                   jax.ShapeDtypeStruct((m - mx, n_gate + lr_col), BF16)],
        scratch_shapes=[pltpu.VMEM((tn, d), BF16)],
        compiler_params=_cparams("arbitrary", "arbitrary", vmem=VMEM_LIMIT_WIDE),
        name="inproj",
    )(h, w_in_t, w_in_t)


def _pool_bands(tile):
    t = np.arange(tile)[:, None]
    s = np.arange(tile + 2 * POOL_HALO)[None, :] - POOL_HALO
    return np.stack([(s >= t - w // 2) & (s < t + w // 2) for w in POOL_WINDOWS])


def _pool_kernel(up_ref, uc_ref, un_ref, pg_ref, band_ref, w_ref, sc_ref, o_ref, *, tile, seq_len):
    base = pl.program_id(1) * tile
    gw = w_ref.shape[-1]
    tc = base + lax.broadcasted_iota(jnp.int32, (tile, 1), 0)
    up = jnp.where(base > 0, up_ref[...], jnp.zeros_like(up_ref))
    un = jnp.where(base + tile < seq_len, un_ref[...], jnp.zeros_like(un_ref))
    for g, window in enumerate(POOL_WINDOWS):
        half = window // 2
        cols = slice(g * gw, (g + 1) * gw)
        u_all = jnp.concatenate([up[:, cols], uc_ref[:, cols], un[:, cols]], axis=0)
        wsum = _dot(band_ref[g], u_all)
        cnt = (jnp.minimum(tc + half, seq_len) - jnp.maximum(tc - half, 0)).astype(F32)
        dcen = wsum / cnt - uc_ref[:, cols].astype(F32)
        y = _dot(dcen.astype(BF16), w_ref[g].astype(BF16)) * sc_ref[:, cols]
        o_ref[:, cols] = (y * _silu(pg_ref[:, cols].astype(F32))).astype(o_ref.dtype)


def _pool(z3, col_u, col_g, pool_w, pool_scale3, l, tile):
    nseq, seq_len, _ = z3.shape
    ngrp = len(POOL_WINDOWS)
    gw = pool_w.shape[-1]
    pw = ngrp * gw
    cu, cg = col_u // pw, col_g // pw
    nhalo = seq_len // POOL_HALO
    per = tile // POOL_HALO
    kern = functools.partial(_pool_kernel, tile=tile, seq_len=seq_len)
    bands = jnp.asarray(_pool_bands(tile), BF16)
    return pl.pallas_call(
        kern,
        grid=(nseq, seq_len // tile),
        in_specs=[
            pl.BlockSpec((None, POOL_HALO, pw), lambda s, r: (s, jnp.maximum(r * per - 1, 0), cu)),
            pl.BlockSpec((None, tile, pw), lambda s, r: (s, r, cu)),
            pl.BlockSpec((None, POOL_HALO, pw), lambda s, r: (s, jnp.minimum((r + 1) * per, nhalo - 1), cu)),
            pl.BlockSpec((None, tile, pw), lambda s, r: (s, r, cg)),
            pl.BlockSpec(bands.shape, lambda s, r: (0, 0, 0)),
            pl.BlockSpec((None, ngrp, gw, gw), lambda s, r: (l, 0, 0, 0)),
            pl.BlockSpec((None, 1, pw), lambda s, r: (l, 0, 0)),
        ],
        out_specs=pl.BlockSpec((None, tile, pw), lambda s, r: (s, r, 0)),
        out_shape=jax.ShapeDtypeStruct((nseq, seq_len, pw), BF16),
        compiler_params=_cparams("arbitrary", "arbitrary"),
        name="pool",
    )(z3, z3, z3, z3, bands, pool_w, pool_scale3)


def _head_norm(t, gain, gmat):
    sq = t * t
    hi = sq.astype(BF16)
    lo = (sq - hi.astype(F32)).astype(BF16)
    ssq = _dot(hi, gmat) + _dot(lo, gmat)
    return t * lax.rsqrt(ssq * (1.0 / DIFF_HEAD_DIM) + EPS) * gain


def _rope(t, cos_t, sin_s, first_half):
    half = DIFF_HEAD_DIM // 2
    lower = pltpu.roll(t, half, 1)
    upper = pltpu.roll(t, LANES - half, 1)
    return t * cos_t + jnp.where(first_half, upper, lower) * sin_s


def _attn_kernel(*refs, tq, row_group, rope_q, has_latent, lambda_init):
    if has_latent:
        (q_ref, dg_ref, kc_ref, vc_ref, kl_ref, vl_ref, cosq_ref, sinq_ref, cosk_ref, sink_ref,
         gains_ref, lam_ref, o_ref, ks_ref, vs_ref) = refs
    else:
        (q_ref, dg_ref, kc_ref, vc_ref, gains_ref, lam_ref, o_ref, ks_ref, vs_ref) = refs
    hw = 2 * DIFF_HEAD_DIM
    lc = kc_ref.shape[0]
    q_gain, k_gain, sub_gain = gains_ref[0:1, :], gains_ref[1:2, :], gains_ref[2:3, :]

    lane = lax.broadcasted_iota(jnp.int32, (1, LANES), 1)
    first_half = (lane & (DIFF_HEAD_DIM - 1)) < (DIFF_HEAD_DIM // 2)
    sub1 = lane < DIFF_HEAD_DIM
    gi = jnp.where(lax.broadcasted_iota(jnp.int32, (LANES, LANES), 0) < DIFF_HEAD_DIM, 1.0, 0.0)
    gj = jnp.where(lax.broadcasted_iota(jnp.int32, (LANES, LANES), 1) < DIFF_HEAD_DIM, 1.0, 0.0)
    gmat = (gi * gj + (1.0 - gi) * (1.0 - gj)).astype(BF16)

    @pl.when(pl.program_id(2) == 0)
    def _():
        ks_ref[:lc, :] = _head_norm(kc_ref[...].astype(F32), k_gain, gmat).astype(BF16)
        vs_ref[:lc, :hw] = vc_ref[...]
        vs_ref[:, hw:] = jnp.ones((vs_ref.shape[0], hw), BF16)
        if has_latent:
            kl = _head_norm(kl_ref[...].astype(F32), k_gain, gmat)
            ks_ref[lc:, :] = _rope(kl, cosk_ref[...], sink_ref[...], first_half).astype(BF16)
            vs_ref[lc:, :hw] = vl_ref[...]

    q = _head_norm(q_ref[...].astype(F32), q_gain, gmat)
    if rope_q:
        q = _rope(q, cosq_ref[...], sinq_ref[...], first_half)
    q = q * (DIFF_HEAD_DIM ** -0.5 * LOG2E)
    qq = jnp.concatenate([jnp.where(sub1, q, 0.0), jnp.where(sub1, 0.0, q)], axis=0).astype(BF16)

    pvs = []
    for r0 in range(0, 2 * tq, row_group):
        s = _dot_nt(qq[r0:r0 + row_group], ks_ref[...])
        e = jnp.exp2(s - jnp.max(s, axis=-1, keepdims=True)).astype(BF16)
        acc = _dot(e, vs_ref[...])
        pvs.append(acc[:, :hw] / acc[:, hw:hw + 1])
    pv = jnp.concatenate(pvs, axis=0)

    lam_p = lam_ref[...]
    lam = (jnp.exp(jnp.sum(lam_p[0:1] * lam_p[1:2], axis=-1, keepdims=True))
           - jnp.exp(jnp.sum(lam_p[2:3] * lam_p[3:4], axis=-1, keepdims=True)) + lambda_init)
    o = pv[:tq] - lam * pv[tq:]

    ms = jnp.mean(o * o, axis=-1, keepdims=True)
    y = o * lax.rsqrt(ms + EPS) * sub_gain * (1.0 - lambda_init)
    o_ref[...] = (y * _silu(dg_ref[...].astype(F32))).astype(o_ref.dtype)


def _attn(zq3, cq, cdg, zc3, ckc, cvc, zl3, ckl, cvl, rope_tabs, gains3, lam4, l, lambda_init, tq):
    nb, lq, _ = zq3.shape
    lc = zc3.shape[1]
    has_latent = zl3 is not None
    hw = 2 * DIFF_HEAD_DIM
    b_of = lambda c: c // hw
    qmap = lambda off: (lambda b, h, i: (b, i, b_of(off) + h))
    kmap = lambda off: (lambda b, h, i: (b, 0, b_of(off) + h))
    const = lambda b, h, i: (0, 0)
    in_specs = [pl.BlockSpec((None, tq, hw), qmap(cq)),
                pl.BlockSpec((None, tq, hw), qmap(cdg)),
                pl.BlockSpec((None, lc, hw), kmap(ckc)),
                pl.BlockSpec((None, lc, hw), kmap(cvc))]
    args = [zq3, zq3, zc3, zc3]
    lk = lc + (zl3.shape[1] if has_latent else 0)
    scratch = [pltpu.VMEM((lk, hw), BF16), pltpu.VMEM((lk, 2 * hw), BF16)]
    if has_latent:
        ll = zl3.shape[1]
        cos_t, sin_s = rope_tabs
        in_specs += [pl.BlockSpec((None, ll, hw), kmap(ckl)),
                     pl.BlockSpec((None, ll, hw), kmap(cvl)),
                     pl.BlockSpec((tq, hw), lambda b, h, i: (i, 0)),
                     pl.BlockSpec((tq, hw), lambda b, h, i: (i, 0)),
                     pl.BlockSpec((ll, hw), const),
                     pl.BlockSpec((ll, hw), const)]
        args += [zl3, zl3, cos_t, sin_s, cos_t, sin_s]
    in_specs += [pl.BlockSpec((None, 3, hw), lambda b, h, i: (l, 0, 0)),
                 pl.BlockSpec((None, 4, DIFF_HEAD_DIM), lambda b, h, i: (l, 0, 0))]
    args += [gains3, lam4]
    kern = functools.partial(_attn_kernel, tq=tq, row_group=min(2 * tq, LANES), rope_q=has_latent, has_latent=has_latent,
                             lambda_init=lambda_init)
    return pl.pallas_call(
        kern,
        grid=(nb, DIFF_HEADS, lq // tq),
        in_specs=in_specs,
        out_specs=pl.BlockSpec((None, tq, hw), lambda b, h, i: (b, i, h)),
        out_shape=jax.ShapeDtypeStruct((nb, lq, DIFF_HEADS * hw), BF16),
        scratch_shapes=scratch,
        compiler_params=_cparams("arbitrary", "arbitrary", "arbitrary"),
        name="diff_attn",
    )(*args)


def _log_sigmoid(x):
    return jnp.minimum(x, 0.0) - jnp.log1p(jnp.exp(-jnp.abs(x)))


GLA_LEVELS = tuple(1 << i for i in range(GLA_CHUNK.bit_length() - 1))
GLA_TAB_LEVELS = tuple(m for m in GLA_LEVELS if m < 8)
GLA_SLOT_CUM = len(GLA_TAB_LEVELS)
GLA_ATT_ROWS = 32
GLA_UNROLL = 4
GLA_TOT_ROWS = 16


def _gla_boundary(m, rev):
    return m if rev else m - 1


@functools.lru_cache(maxsize=None)
def _gla_tables_np():
    c = GLA_CHUNK
    t = np.arange(c)[:, None]
    j = np.arange(c)[None, :]
    sums, masks = [], []
    for rev in (False, True):
        rows, mk = [], []
        for m in GLA_LEVELS:
            base = t & ~(2 * m - 1)
            bd = base + _gla_boundary(m, rev)
            is_q = ((t & m) == 0) if rev else ((t & m) != 0)
            if rev:
                between = np.where(is_q, (j >= t) & (j < bd), (j >= bd) & (j < t))
            else:
                between = np.where(is_q, (j > bd) & (j <= t), (j > t) & (j <= bd))
            if m in GLA_TAB_LEVELS:
                rows.append(between)
            mk.append((base == (j & ~(2 * m - 1))) & is_q & (((j & m) != 0) if rev else ((j & m) == 0)))
        rows.append(j >= t if rev else j <= t)
        rows.append(np.ones((GLA_TOT_ROWS, c), bool))
        mk.append(t == j)
        sums.append(np.concatenate(rows, axis=0))
        masks.append(np.stack(mk))
    return np.stack(sums), np.stack(masks)


def _gla_masks_np():
    return _gla_tables_np()[1]


def _gla_tables():
    sums, masks = _gla_tables_np()
    return jnp.asarray(sums, BF16), jnp.asarray(masks, F32)


def _gla_chunk(q, k, v, ghl, st, tab_ref, mask_ref, rev, need_out):
    c = GLA_CHUNK
    first = 0 if need_out else GLA_SLOT_CUM
    x2 = _dot(tab_ref[first * c:, :], ghl)
    x = x2[:, :GLA_DK] + x2[:, GLA_DK:]
    cum = x[(GLA_SLOT_CUM - first) * c:(GLA_SLOT_CUM - first + 1) * c]
    tot = x[-GLA_TOT_ROWS:1 - GLA_TOT_ROWS]
    kd = (k * jnp.exp2(tot - cum)).astype(BF16)
    st_new = st * jnp.exp2(tot) + _dot_tn(v, kd)
    if not need_out:
        return None, st_new

    out = _dot_nt((q * jnp.exp2(cum)).astype(BF16), st.astype(BF16))

    own = _gla_masks_np()[int(rev)]
    n_rb, n_lt = c // GLA_ATT_ROWS, c // LANES
    regions = [[None] * n_lt for _ in range(n_rb)]

    def accumulate(lvl, prod):
        for rb in range(n_rb):
            rs = slice(rb * GLA_ATT_ROWS, (rb + 1) * GLA_ATT_ROWS)
            for lt in range(n_lt):
                ls = slice(lt * LANES, (lt + 1) * LANES)
                if not own[lvl, rs, ls].any():
                    continue
                part = prod[rs, ls]
                if not own[lvl, rs, ls].all():
                    part = part * mask_ref[lvl, rs, ls]
                regions[rb][lt] = part if regions[rb][lt] is None else regions[rb][lt] + part

    accumulate(len(GLA_LEVELS), _dot_nt(q.astype(BF16), k.astype(BF16)))
    for lvl, m in enumerate(GLA_LEVELS):
        if m in GLA_TAB_LEVELS:
            xl = x[lvl * c:(lvl + 1) * c]
        else:
            anchor = jnp.concatenate(
                [jnp.broadcast_to(cum[b0 + _gla_boundary(m, rev):b0 + _gla_boundary(m, rev) + 1], (2 * m, GLA_DK))
                 for b0 in range(0, c, 2 * m)], axis=0)
            xl = -jnp.abs(cum - anchor)
        el = jnp.exp2(xl)
        accumulate(lvl, _dot_nt((q * el).astype(BF16), (k * el).astype(BF16)))
    zero = jnp.zeros((GLA_ATT_ROWS, LANES), F32)
    att = jnp.concatenate([jnp.concatenate([zero if r is None else r for r in row], axis=1) for row in regions], axis=0)
    return out + _dot(att.astype(BF16), v), st_new


def _gla_kernel(*refs, n_ctx, n_lat, ctx_out):
    (qc_ref, kc_ref, vc_ref, ggc_ref, lrc_ref, ql_ref, kl_ref, vl_ref, ggl_ref, lrl_ref,
     w2_ref, b2_ref, gain_ref, tab_ref, mask_ref) = refs[:15]
    rest = refs[15:]
    if ctx_out:
        ol_ref, oc_ref, gl_ref, gc_ref, accl_ref, accc_ref, stf_ref, stb_ref = rest
    else:
        ol_ref, gl_ref, gc_ref, accl_ref, stf_ref, stb_ref = rest
        oc_ref = accc_ref = None

    def decays(lr_ref, g_ref):
        g = _log_sigmoid(_dot(lr_ref[...], w2_ref[...]) + b2_ref[...]) * (LOG2E / GLA_NORMALIZER)
        hi = g.astype(BF16)
        lo = (g - hi.astype(F32)).astype(BF16)
        for d in range(2):
            g_ref[:, (2 * d) * GLA_DK:(2 * d + 1) * GLA_DK] = hi[:, d * GLA_DK:(d + 1) * GLA_DK]
            g_ref[:, (2 * d + 1) * GLA_DK:(2 * d + 2) * GLA_DK] = lo[:, d * GLA_DK:(d + 1) * GLA_DK]

    decays(lrl_ref, gl_ref)
    decays(lrc_ref, gc_ref)
    stf_ref[...] = jnp.zeros_like(stf_ref)
    stb_ref[...] = jnp.zeros_like(stb_ref)
    scale = GLA_DK ** -0.5

    def scan(n, q_ref, k_ref, v_ref, g_ref, acc_ref):
        need_out = acc_ref is not None

        def step(i, assign_up, assign_down):
            for c_idx, d, st_ref, first_visit in ((i, 0, stf_ref, assign_up), (n - 1 - i, 1, stb_ref, assign_down)):
                rows = pl.ds(pl.multiple_of(c_idx * GLA_CHUNK, GLA_CHUNK), GLA_CHUNK)
                q = q_ref[rows, :].astype(F32) * scale
                k = k_ref[rows, :].astype(F32)
                ghl = g_ref[rows, 2 * d * GLA_DK:(2 * d + 2) * GLA_DK]
                out, st_new = _gla_chunk(q, k, v_ref[rows, :], ghl, st_ref[...], tab_ref.at[d], mask_ref.at[d],
                                         d == 1, need_out)
                st_ref[...] = st_new
                if need_out:
                    if first_visit:
                        acc_ref[rows, :] = out
                    else:
                        acc_ref[rows, :] += out

        def first(i, carry):
            step(i, True, True)
            return carry

        def second(i, carry):
            step(i, False, False)
            return carry

        if n // 2:
            lax.fori_loop(0, n // 2, first, 0, unroll=min(GLA_UNROLL, n // 2))
        if n % 2:
            step(n // 2, True, False)
        if n // 2:
            lax.fori_loop((n + 1) // 2, n, second, 0, unroll=min(GLA_UNROLL, n // 2))

    def finish(acc_ref, gg_ref, o_ref):
        o = acc_ref[...]
        ms = jnp.mean(o * o, axis=-1, keepdims=True)
        y = o * lax.rsqrt(ms + EPS) * gain_ref[...]
        o_ref[...] = (y * _silu(gg_ref[...].astype(F32))).astype(o_ref.dtype)

    scan(n_ctx, qc_ref, kc_ref, vc_ref, gc_ref, accc_ref)
    scan(n_lat, ql_ref, kl_ref, vl_ref, gl_ref, accl_ref)
    finish(accl_ref, ggl_ref, ol_ref)
    if ctx_out:
        finish(accc_ref, ggc_ref, oc_ref)


def _gla(zc3, cc, lrc3, zl3, cl, lrl3, w2, b2, gain3, l, ctx_out):
    nb, lc, _ = zc3.shape
    ll = zl3.shape[1]
    assert lc % GLA_CHUNK == 0 and ll % GLA_CHUNK == 0
    tabs, masks = _gla_tables()

    def seq_specs(z3, cols, lr3, slen):
        qo, ko, vo, go = cols
        return ([pl.BlockSpec((None, slen, GLA_DK), lambda b, h: (b, 0, qo // GLA_DK + h)),
                 pl.BlockSpec((None, slen, GLA_DK), lambda b, h: (b, 0, ko // GLA_DK + h)),
                 pl.BlockSpec((None, slen, GLA_DV), lambda b, h: (b, 0, vo // GLA_DV + h)),
                 pl.BlockSpec((None, slen, GLA_DV), lambda b, h: (b, 0, (0 if go is None else go) // GLA_DV + h)),
                 pl.BlockSpec((None, slen, LANES), lambda b, h: (b, 0, 0))],
                [z3, z3, z3, z3, lr3])

    sc, ac = seq_specs(zc3, cc, lrc3, lc)
    sl, al = seq_specs(zl3, cl, lrl3, ll)
    in_specs = sc + sl + [pl.BlockSpec((None, None, LANES, 2 * GLA_DK), lambda b, h: (l, h, 0, 0)),
                          pl.BlockSpec((None, None, 1, 2 * GLA_DK), lambda b, h: (l, h, 0, 0)),
                          pl.BlockSpec((None, 1, GLA_DV), lambda b, h: (l, 0, 0)),
                          pl.BlockSpec(tabs.shape, lambda b, h: (0, 0, 0)),
                          pl.BlockSpec(masks.shape, lambda b, h: (0, 0, 0, 0))]
    args = ac + al + [w2, b2, gain3, tabs, masks]
    out_specs = [pl.BlockSpec((None, ll, GLA_DV), lambda b, h: (b, 0, h))]
    out_shape = [jax.ShapeDtypeStruct((nb, ll, GLA_HEADS * GLA_DV), BF16)]
    scratch = [pltpu.VMEM((ll, 4 * GLA_DK), BF16), pltpu.VMEM((lc, 4 * GLA_DK), BF16),
               pltpu.VMEM((ll, GLA_DV), F32)]
    if ctx_out:
        out_specs.append(pl.BlockSpec((None, lc, GLA_DV), lambda b, h: (b, 0, h)))
        out_shape.append(jax.ShapeDtypeStruct((nb, lc, GLA_HEADS * GLA_DV), BF16))
        scratch.append(pltpu.VMEM((lc, GLA_DV), F32))
    scratch += [pltpu.VMEM((GLA_DV, GLA_DK), F32), pltpu.VMEM((GLA_DV, GLA_DK), F32)]
    kern = functools.partial(_gla_kernel, n_ctx=lc // GLA_CHUNK, n_lat=ll // GLA_CHUNK, ctx_out=ctx_out)
    outs = pl.pallas_call(
        kern,
        grid=(nb, GLA_HEADS),
        in_specs=in_specs,
        out_specs=out_specs,
        out_shape=out_shape,
        scratch_shapes=scratch,
        compiler_params=_cparams("arbitrary", "arbitrary"),
        name="gla",
    )(*args)
    return (outs[0], outs[1]) if ctx_out else (outs[0], None)


MERGE_SLAB = 512


def _merge_kernel(p_ref, d_ref, g_ref, wp_ref, wd_ref, wg_ref, mp_ref, md_ref, mg_ref, y_ref):
    n = y_ref.shape[1]
    for c0 in range(0, n, MERGE_SLAB):
        cs = slice(c0, c0 + MERGE_SLAB)
        y = (_sigmoid(mp_ref[:, cs].astype(F32)) * _dot(p_ref[...], wp_ref[:, cs])
             + _sigmoid(md_ref[:, cs].astype(F32)) * _dot(d_ref[...], wd_ref[:, cs])
             + _sigmoid(mg_ref[:, cs].astype(F32)) * _dot(g_ref[...], wg_ref[:, cs]))
        y_ref[:, cs] = y.astype(y_ref.dtype)


def _merge(pool_o, diff_o, gla_o, z2d, col_mg, wbp, wbd, wbg, l, tm):
    m, kw = pool_o.shape
    d = wbp.shape[2]
    act = pl.BlockSpec((tm, kw), lambda i: (i, 0))
    wsp = pl.BlockSpec((None, kw, d), lambda i: (l, 0, 0), pipeline_mode=pl.Buffered(1))
    gate = lambda k: pl.BlockSpec((tm, d), lambda i: (i, col_mg // d + k))
    return pl.pallas_call(
        _merge_kernel,
        grid=(m // tm,),
        in_specs=[act, act, act, wsp, wsp, wsp, gate(0), gate(1), gate(2)],
        out_specs=pl.BlockSpec((tm, d), lambda i: (i, 0)),
        out_shape=jax.ShapeDtypeStruct((m, d), BF16),
        compiler_params=_cparams("arbitrary"),
        name="merge",
    )(pool_o, diff_o, gla_o, wbp, wbd, wbg, z2d, z2d, z2d)


def _outproj_kernel(y_ref, w_ref, x_ref, mod_ref, o_ref):
    o_ref[...] = x_ref[...] + mod_ref[2:3, :] * _dot(y_ref[...], w_ref[...])


def _outproj(y, w_out, x2d, mod3, l, row_of_tile, tm):
    m, d = x2d.shape
    return pl.pallas_call(
        _outproj_kernel,
        grid=(m // tm,),
        in_specs=[pl.BlockSpec((tm, d), lambda i: (i, 0)),
                  pl.BlockSpec((None, d, d), lambda i: (l, 0, 0), pipeline_mode=pl.Buffered(1)),
                  pl.BlockSpec((tm, d), lambda i: (i, 0)),
                  pl.BlockSpec((None, 3, d), lambda i: (row_of_tile(i), 0, 0))],
        out_specs=pl.BlockSpec((tm, d), lambda i: (i, 0)),
        out_shape=jax.ShapeDtypeStruct((m, d), F32),
        compiler_params=_cparams("arbitrary"),
        name="outproj",
    )(y, w_out, x2d, mod3)


def _rope_tables(seq_len):
    n_freq = DIFF_HEAD_DIM // 4
    t = jnp.arange(seq_len)
    inv = ROPE_THETA ** (-jnp.arange(n_freq, dtype=F32) / n_freq)
    ang = jnp.concatenate([(t // GRID_W).astype(F32)[:, None] * inv,
                           (t % GRID_W).astype(F32)[:, None] * inv], axis=-1)
    cos, sin = jnp.cos(ang), jnp.sin(ang)
    return jnp.tile(cos, (1, 4)), jnp.tile(jnp.concatenate([-sin, sin], axis=-1), (1, 2))


def _largest_tile(n, cap, mult):
    t = min(n, cap)
    while n % t or t % mult:
        t -= mult
    return t


def kernel(x, c, ctx, c_ctx, norm_g, w_ada, b_ada, w_in, pool_w, pool_scale, diff_q_norm, diff_k_norm, diff_lam_q1, diff_lam_k1, diff_lam_q2, diff_lam_k2, diff_subln, gla_w_gate_f, gla_b_gate_f, gla_w_gate_b, gla_b_gate_b, gla_norm, w_branch_pool, w_branch_diff, w_branch_gla, w_out):
    nb, seq, d = x.shape
    lc = ctx.shape[1]
    depth = w_in.shape[0]
    pw = pool_scale.shape[1]
    dw = DIFF_HEADS * 2 * DIFF_HEAD_DIM
    gkw, gvw = GLA_HEADS * GLA_DK, GLA_HEADS * GLA_DV

    sizes = dict(pu=pw, pg=pw, dq=dw, dk=dw, dv=dw, dg=dw, gq=gkw, gk=gkw, gv=gvw, gg=gvw)
    wcol, off = {}, 0
    for name, size in sizes.items():
        wcol[name] = off
        off += size
    lr_col = off
    zcol = {name: 3 * d + o for name, o in wcol.items()}
    zcol["mg"] = 0

    rope_tabs = _rope_tables(seq)
    cc = jnp.zeros((8, d), F32).at[:nb].set(c).at[nb].set(c_ctx)
    x2d = x.reshape(nb * seq, d)
    ctx2d = ctx.reshape(nb * lc, d)

    w_in_t = jnp.swapaxes(w_in, 1, 2)
    wbp, wbd, wbg, wo = (w.astype(BF16) for w in (w_branch_pool, w_branch_diff, w_branch_gla, w_out))
    norm_g3 = norm_g.reshape(depth, 1, d)
    b_ada3 = b_ada.reshape(depth, 1, 3 * d)
    pool_scale3 = pool_scale.reshape(depth, 1, pw)
    gains3 = jnp.stack([jnp.tile(diff_q_norm, (1, 2)), jnp.tile(diff_k_norm, (1, 2)), diff_subln], axis=1)
    lam4 = jnp.stack([diff_lam_q1, diff_lam_k1, diff_lam_q2, diff_lam_k2], axis=1)
    gla_gain3 = gla_norm.reshape(depth, 1, GLA_DV)
    per_head = lambda w: w.reshape(depth, GLA_RANK, GLA_HEADS, GLA_DK).transpose(0, 2, 1, 3)
    w2 = jnp.zeros((depth, GLA_HEADS, LANES, 2 * GLA_DK), F32)
    w2 = w2.at[:, :, :GLA_RANK, :GLA_DK].set(per_head(gla_w_gate_f))
    w2 = w2.at[:, :, GLA_RANK:2 * GLA_RANK, GLA_DK:].set(per_head(gla_w_gate_b)).astype(BF16)
    b2 = jnp.concatenate([gla_b_gate_f.reshape(depth, GLA_HEADS, 1, GLA_DK),
                          gla_b_gate_b.reshape(depth, GLA_HEADS, 1, GLA_DK)], axis=-1)

    rows_gcd = math.gcd(seq, nb * lc)
    tm_pre = _largest_tile(rows_gcd, 512, 16)
    tm_in = _largest_tile(rows_gcd, 1024, 16)
    tn_in = _largest_tile(math.gcd(lr_col, 3 * d), 1536, 256)
    row_x = lambda tm: (lambda i: (i * tm) // seq)
    row_c = lambda i: nb
    kv_cols = ((zcol["dk"], zcol["dg"]), (zcol["gk"], zcol["gg"]))
    kv_tiles = tuple(t for t in range((3 * d + lr_col) // tn_in)
                     if any(lo < (t + 1) * tn_in and t * tn_in < hi for lo, hi in kv_cols))
    ccol = zcol

    for l in range(depth):
        last = l == depth - 1
        lambda_init = 0.8 - 0.6 * math.exp(-0.3 * l)
        mod3 = _ada(cc, w_ada, b_ada3, l).reshape(8, 3, d)
        h, lr, lrc = _prenorm(x2d, ctx2d, mod3, norm_g3, w_in_t, l, lr_col, seq, nb, tm_pre)
        z, zc = _inproj(h, w_in_t, l, lr_col, 3 * d, nb * seq, tm_in, tn_in, kv_tiles if last else None)
        z3, zc3 = z.reshape(nb, seq, -1), zc.reshape(nb, lc, -1)
        lr3, lrc3 = lr.reshape(nb, seq, LANES), lrc.reshape(nb, lc, LANES)

        pool_l = _pool(z3, zcol["pu"], zcol["pg"], pool_w, pool_scale3, l, _largest_tile(seq, 512, 16))
        diff_l = _attn(z3, zcol["dq"], zcol["dg"], zc3, ccol["dk"], ccol["dv"], z3, zcol["dk"], zcol["dv"],
                       rope_tabs, gains3, lam4, l, lambda_init, _largest_tile(seq, 2048, 16))
        gla_l, gla_c = _gla(zc3, (ccol["gq"], ccol["gk"], ccol["gv"], ccol["gg"]), lrc3,
                            z3, (zcol["gq"], zcol["gk"], zcol["gv"], zcol["gg"]), lr3,
                            w2, b2, gla_gain3, l, ctx_out=not last)

        tm_m = _largest_tile(seq, 512, 16)
        y = _merge(pool_l.reshape(nb * seq, pw), diff_l.reshape(nb * seq, dw), gla_l.reshape(nb * seq, gvw),
                   z, zcol["mg"], wbp, wbd, wbg, l, tm_m)
        x2d_new = _outproj(y, wo, x2d, mod3, l, row_x(tm_m), tm_m)

        if not last:
            pool_c = _pool(zc3, zcol["pu"], zcol["pg"], pool_w, pool_scale3, l, _largest_tile(lc, 512, 16))
            diff_c = _attn(zc3, zcol["dq"], zcol["dg"], zc3, zcol["dk"], zcol["dv"], None, None, None,
                           None, gains3, lam4, l, lambda_init, _largest_tile(lc, 256, 16))
            tm_mc = _largest_tile(nb * lc, 512, 16)
            y_c = _merge(pool_c.reshape(nb * lc, pw), diff_c.reshape(nb * lc, dw), gla_c.reshape(nb * lc, gvw),
                         zc, zcol["mg"], wbp, wbd, wbg, l, tm_mc)
            ctx2d = _outproj(y_c, wo, ctx2d, mod3, l, row_c, tm_mc)
        x2d = x2d_new

    return x2d.reshape(nb, seq, d)
```
